```python
import math
import numpy as np
import jax
import jax.numpy as jnp
from jax import lax

D_MODEL = 1024
BATCH = 8
SEQ = 4096
DEPTH = 2

CTX_LEN = 256
GRID_W = 64

SSD_HEADS = 8
SSD_HEAD_DIM = 64
SSD_WIDTH = SSD_HEADS * SSD_HEAD_DIM
SSD_GROUPS = 2
SSD_STATE = 64
SSD_CONV = 5
SSD_CHUNK = 128
SSD_XBC = SSD_WIDTH + 2 * SSD_GROUPS * SSD_STATE
DT_MIN = 0.001
DT_MAX = 0.1

GLA_HEADS = 4
GLA_KEY_DIM = 64
GLA_VAL_DIM = 128
GLA_QK = GLA_HEADS * GLA_KEY_DIM
GLA_WIDTH = GLA_HEADS * GLA_VAL_DIM
GLA_GATE_RANK = 16
GLA_GATE_NORM = 16.0
GLA_CHUNK = 64

MIX_WIDTH = SSD_WIDTH + GLA_WIDTH
IN_DIM = SSD_WIDTH + SSD_XBC + SSD_HEADS + 2 * GLA_QK + 2 * GLA_WIDTH + GLA_GATE_RANK

N_EXPERTS = 32
N_EXPERT_GROUPS = 8
EXPERTS_PER_GROUP = N_EXPERTS // N_EXPERT_GROUPS
TOP_K = 2
EXPERT_FF = 512
MOE_BLOCK = 128

NORM_EPS = 1e-6

kernel_name = 'hybrid_ssd_gla_moe_prefix_dit'


def rmsnorm(x, gain):
    xf = x.astype(jnp.float32)
    y = xf * lax.rsqrt(jnp.mean(xf * xf, axis=-1, keepdims=True) + NORM_EPS)
    return (y * gain.astype(jnp.float32)).astype(x.dtype)


def modulate(h, shift, scale):
    return h * (1.0 + scale) + shift


def dwconv_centred(u, w, b):
    K, C = w.shape
    pad = K // 2
    y = lax.conv_general_dilated(u, w[:, None, :].astype(u.dtype), window_strides=(1,),
                                 padding=[(pad, K - 1 - pad)],
                                 dimension_numbers=('NWC', 'WIO', 'NWC'),
                                 feature_group_count=C)
    return y + b


def conv_latent(u, w, b):
    Bsz, L, C = u.shape
    rows = L // GRID_W
    y = dwconv_centred(u.reshape(Bsz * rows, GRID_W, C), w, b)
    return y.reshape(Bsz, L, C)


def ssd_scan(x, dt, Bm, Cm, A, h0, with_output):
    f32 = jnp.float32
    Bsz, L, H, P = x.shape
    G, N = Bm.shape[2], Bm.shape[3]
    Q = SSD_CHUNK
    nc = L // Q
    xc = x.astype(f32).reshape(Bsz, nc, Q, H, P)
    dtc = dt.astype(f32).reshape(Bsz, nc, Q, H)
    Bc = jnp.repeat(Bm.astype(f32), H // G, axis=2).reshape(Bsz, nc, Q, H, N)
    Cc = jnp.repeat(Cm.astype(f32), H // G, axis=2).reshape(Bsz, nc, Q, H, N)
    a_cs = jnp.cumsum(dtc * A.astype(f32), axis=2)
    a_end = a_cs[:, :, -1]
    w_end = jnp.exp(a_end[:, :, None] - a_cs) * dtc
    chunk_states = jnp.einsum('bcjh,bcjhn,bcjhp->bchpn', w_end, Bc, xc)

    def step(h, inp):
        decay, s = inp
        return decay[..., None, None] * h + s, h

    h_final, h_in = lax.scan(step, h0.astype(f32),
                             (jnp.moveaxis(jnp.exp(a_end), 1, 0), jnp.moveaxis(chunk_states, 1, 0)))
    if not with_output:
        return None, h_final
    h_in = jnp.moveaxis(h_in, 0, 1)
    causal = jnp.tril(jnp.ones((Q, Q), dtype=bool))
    seg = a_cs[:, :, :, None, :] - a_cs[:, :, None, :, :]
    decay_ij = jnp.exp(jnp.where(causal[None, None, :, :, None], seg, -jnp.inf))
    scores = jnp.einsum('bcihn,bcjhn->bcijh', Cc, Bc) * decay_ij * dtc[:, :, None, :, :]
    y = jnp.einsum('bcijh,bcjhp->bcihp', scores, xc)
    y = y + jnp.einsum('bcihn,bchpn->bcihp', Cc, h_in) * jnp.exp(a_cs)[..., None]
    return y.reshape(Bsz, L, H, P).astype(x.dtype), h_final


def gla_scan(q, k, v, g, S0, with_output):
    f32 = jnp.float32
    Bsz, L, H, K = q.shape
    V = v.shape[-1]
    Q = GLA_CHUNK
    nc = L // Q
    qc = (q.astype(f32) * (K ** -0.5)).reshape(Bsz, nc, Q, H, K)
    kc = k.astype(f32).reshape(Bsz, nc, Q, H, K)
    vc = v.astype(f32).reshape(Bsz, nc, Q, H, V)
    b = jnp.cumsum(g.astype(f32).reshape(Bsz, nc, Q, H, K), axis=2)
    b_end = b[:, :, -1]
    k_end = kc * jnp.exp(b_end[:, :, None] - b)
    chunk_states = jnp.einsum('bcjhk,bcjhv->bchkv', k_end, vc)

    def step(S, inp):
        decay, s = inp
        return decay[..., None] * S + s, S

    S_final, S_in = lax.scan(step, S0.astype(f32),
                             (jnp.moveaxis(jnp.exp(b_end), 1, 0), jnp.moveaxis(chunk_states, 1, 0)))
    if not with_output:
        return None, S_final
    S_in = jnp.moveaxis(S_in, 0, 1)
    q_dec = qc * jnp.exp(b)
    k_inv = kc * jnp.exp(-b)
    causal = jnp.tril(jnp.ones((Q, Q), dtype=bool))
    attn = jnp.where(causal, jnp.einsum('bcihk,bcjhk->bchij', q_dec, k_inv), 0.0)
    o = jnp.einsum('bchij,bcjhv->bcihv', attn, vc) + jnp.einsum('bcihk,bchkv->bcihv', q_dec, S_in)
    return o.reshape(Bsz, L, H, V).astype(q.dtype), S_final


def run_scan(scan_fn, seqs, params, state0, reverse, with_output):
    if reverse:
        seqs = [jnp.flip(s, axis=1) for s in seqs]
    y, state = scan_fn(*seqs, *params, state0, with_output)
    if reverse and with_output:
        y = jnp.flip(y, axis=1)
    return y, state


def mixer_inputs(h, p, conv_fn):
    f32 = jnp.float32
    Bsz, L, _ = h.shape
    sizes = [SSD_WIDTH, SSD_XBC, SSD_HEADS, GLA_QK, GLA_QK, GLA_WIDTH, GLA_WIDTH, GLA_GATE_RANK]
    offsets = np.cumsum(sizes)[:-1].tolist()
    z, xbc, dt_raw, q, k, v, g_out, gk_low = jnp.split(h @ p['w_in'], offsets, axis=-1)
    xbc = jax.nn.silu(conv_fn(xbc, p['conv_w'], p['conv_b']))
    xs, bm, cm = jnp.split(xbc, [SSD_WIDTH, SSD_WIDTH + SSD_GROUPS * SSD_STATE], axis=-1)
    gf = jax.nn.log_sigmoid((gk_low @ p['gk_up_f'] + p['gk_bias_f']).astype(f32)) / GLA_GATE_NORM
    gb = jax.nn.log_sigmoid((gk_low @ p['gk_up_b'] + p['gk_bias_b']).astype(f32)) / GLA_GATE_NORM

    def heads(t, n, d):
        return t.reshape(Bsz, L, n, d)

    return {
        'z': z,
        'g_out': g_out,
        'x': heads(xs, SSD_HEADS, SSD_HEAD_DIM),
        'B': heads(bm, SSD_GROUPS, SSD_STATE),
        'C': heads(cm, SSD_GROUPS, SSD_STATE),
        'dt_f': jax.nn.softplus((dt_raw + p['dt_bias_f']).astype(f32)),
        'dt_b': jax.nn.softplus((dt_raw + p['dt_bias_b']).astype(f32)),
        'q': heads(q, GLA_HEADS, GLA_KEY_DIM),
        'k': heads(k, GLA_HEADS, GLA_KEY_DIM),
        'v': heads(v, GLA_HEADS, GLA_VAL_DIM),
        'g_f': heads(gf, GLA_HEADS, GLA_KEY_DIM),
        'g_b': heads(gb, GLA_HEADS, GLA_KEY_DIM),
    }


def merge_heads(m, y_ssd, o_gla, p):
    Bsz, L = m['z'].shape[:2]
    y = (y_ssd + p['d_skip'][:, None] * m['x']).reshape(Bsz, L, SSD_WIDTH)
    y = rmsnorm(y * jax.nn.silu(m['z']), p['ssd_norm_g'])
    o = rmsnorm(o_gla, p['gla_norm_g']).reshape(Bsz, L, GLA_WIDTH) * jax.nn.silu(m['g_out'])
    return jnp.concatenate([y, o], axis=-1) @ p['w_out']


def token_mixers(h_ctx, h_lat, p, ctx_out):
    mc = mixer_inputs(h_ctx, p, dwconv_centred)
    ml = mixer_inputs(h_lat, p, conv_latent)
    a_f = -jnp.exp(p['a_log_f'].astype(jnp.float32))
    a_b = -jnp.exp(p['a_log_b'].astype(jnp.float32))
    Bsz = h_lat.shape[0]
    hs0 = jnp.zeros((Bsz, SSD_HEADS, SSD_HEAD_DIM, SSD_STATE), jnp.float32)
    gs0 = jnp.zeros((Bsz, GLA_HEADS, GLA_KEY_DIM, GLA_VAL_DIM), jnp.float32)
    ysc_f, hs_f = run_scan(ssd_scan, [mc['x'], mc['dt_f'], mc['B'], mc['C']], [a_f], hs0, False, ctx_out)
    ysc_b, hs_b = run_scan(ssd_scan, [mc['x'], mc['dt_b'], mc['B'], mc['C']], [a_b], hs0, True, ctx_out)
    ysl_f, _ = run_scan(ssd_scan, [ml['x'], ml['dt_f'], ml['B'], ml['C']], [a_f], hs_f, False, True)
    ysl_b, _ = run_scan(ssd_scan, [ml['x'], ml['dt_b'], ml['B'], ml['C']], [a_b], hs_b, True, True)
    goc_f, gs_f = run_scan(gla_scan, [mc['q'], mc['k'], mc['v'], mc['g_f']], [], gs0, False, ctx_out)
    goc_b, gs_b = run_scan(gla_scan, [mc['q'], mc['k'], mc['v'], mc['g_b']], [], gs0, True, ctx_out)
    gol_f, _ = run_scan(gla_scan, [ml['q'], ml['k'], ml['v'], ml['g_f']], [], gs_f, False, True)
    gol_b, _ = run_scan(gla_scan, [ml['q'], ml['k'], ml['v'], ml['g_b']], [], gs_b, True, True)
    out_lat = merge_heads(ml, ysl_f + ysl_b, gol_f + gol_b, p)
    out_ctx = merge_heads(mc, ysc_f + ysc_b, goc_f + goc_b, p) if ctx_out else None
    return out_ctx, out_lat


def route(t, w_router, router_bias):
    T = t.shape[0]
    s = jax.nn.sigmoid((t @ w_router).astype(jnp.float32))
    sel = (s + router_bias.astype(jnp.float32)).reshape(T, N_EXPERT_GROUPS, EXPERTS_PER_GROUP)
    group_score = lax.top_k(sel, 2)[0].sum(axis=-1)
    grp = jnp.argmax(group_score, axis=-1)
    in_grp = sel[jnp.arange(T), grp]
    _, local = lax.top_k(in_grp, TOP_K)
    ids = grp[:, None] * EXPERTS_PER_GROUP + local
    w = jnp.take_along_axis(s, ids, axis=1)
    return ids, w / jnp.sum(w, axis=-1, keepdims=True)


def moe(t, w_router, router_bias, w_gate, w_up, w_down):
    T, D = t.shape
    ids, gates = route(t, w_router, router_bias)
    flat_e = ids.reshape(-1)
    order = jnp.argsort(flat_e)
    e_sorted = flat_e[order]
    tok = order // TOP_K
    counts = jnp.zeros((N_EXPERTS,), jnp.int32).at[flat_e].add(1)
    padded = (counts + MOE_BLOCK - 1) // MOE_BLOCK * MOE_BLOCK
    start = jnp.cumsum(counts) - counts
    pend = jnp.cumsum(padded)
    pstart = pend - padded
    dest = pstart[e_sorted] + jnp.arange(T * TOP_K) - start[e_sorted]
    n_blocks = -(-(T * TOP_K) // MOE_BLOCK) + N_EXPERTS
    buf = jnp.zeros((n_blocks * MOE_BLOCK, D), t.dtype).at[dest].set(t[tok])
    block_e = jnp.minimum(jnp.searchsorted(pend, jnp.arange(n_blocks) * MOE_BLOCK, side='right'),
                          N_EXPERTS - 1)

    def expert_block(args):
        xb, e = args
        return (jax.nn.silu(xb @ w_gate[e]) * (xb @ w_up[e])) @ w_down[e]

    yb = lax.map(expert_block, (buf.reshape(n_blocks, MOE_BLOCK, D), block_e))
    y = yb.reshape(-1, D)[dest] * gates.reshape(-1)[order][:, None].astype(t.dtype)
    return jnp.zeros_like(t).at[tok].add(y)


def setup_inputs(seed: int = 0) -> dict:
    key = jax.random.key(seed)
    ks = jax.random.split(key, 32)
    f32 = jnp.float32

    def nrm(k, shape, scale):
        return jax.random.normal(k, shape, f32) * scale

    def gain(k, shape):
        return 1.0 + 0.05 * jax.random.normal(k, shape, f32)

    def dt_bias(k):
        dt = jnp.exp(jax.random.uniform(k, (DEPTH, SSD_HEADS), f32,
                                        minval=math.log(DT_MIN), maxval=math.log(DT_MAX)))
        return dt + jnp.log(-jnp.expm1(-dt))

    def a_log(k):
        return jnp.log(jax.random.uniform(k, (DEPTH, SSD_HEADS), f32, minval=1.0, maxval=16.0))

    return {
        'x': nrm(ks[0], (BATCH, SEQ, D_MODEL), 1.0),
        'c': nrm(ks[1], (BATCH, D_MODEL), 1.0),
        'ctx': nrm(ks[2], (BATCH, CTX_LEN, D_MODEL), 1.0),
        'c_ctx': nrm(ks[3], (D_MODEL,), 1.0),
        'w_mod': nrm(ks[4], (DEPTH, D_MODEL, 6 * D_MODEL), 0.5 * D_MODEL ** -0.5),
        'b_mod': nrm(ks[5], (DEPTH, 6 * D_MODEL), 0.02),
        'norm1_g': gain(ks[6], (DEPTH, D_MODEL)),
        'norm2_g': gain(ks[7], (DEPTH, D_MODEL)),
        'w_in': nrm(ks[8], (DEPTH, D_MODEL, IN_DIM), D_MODEL ** -0.5),
        'conv_w': nrm(ks[9], (DEPTH, SSD_CONV, SSD_XBC), SSD_CONV ** -0.5),
        'conv_b': nrm(ks[10], (DEPTH, SSD_XBC), 0.02),
        'dt_bias_f': dt_bias(ks[11]),
        'dt_bias_b': dt_bias(ks[12]),
        'a_log_f': a_log(ks[13]),
        'a_log_b': a_log(ks[14]),
        'd_skip': gain(ks[15], (DEPTH, SSD_HEADS)),
        'ssd_norm_g': gain(ks[16], (DEPTH, SSD_WIDTH)),
        'gk_up_f': nrm(ks[17], (DEPTH, GLA_GATE_RANK, GLA_QK), GLA_GATE_RANK ** -0.5),
        'gk_bias_f': nrm(ks[18], (DEPTH, GLA_QK), 0.02),
        'gk_up_b': nrm(ks[19], (DEPTH, GLA_GATE_RANK, GLA_QK), GLA_GATE_RANK ** -0.5),
        'gk_bias_b': nrm(ks[20], (DEPTH, GLA_QK), 0.02),
        'gla_norm_g': gain(ks[21], (DEPTH, GLA_VAL_DIM)),
        'w_out': nrm(ks[22], (DEPTH, MIX_WIDTH, D_MODEL), MIX_WIDTH ** -0.5),
        'w_router': nrm(ks[23], (D_MODEL, N_EXPERTS), D_MODEL ** -0.5),
        'router_bias': nrm(ks[24], (N_EXPERTS,), 0.01),
        'w_gate': nrm(ks[25], (DEPTH, N_EXPERTS, D_MODEL, EXPERT_FF), D_MODEL ** -0.5),
        'w_up': nrm(ks[26], (DEPTH, N_EXPERTS, D_MODEL, EXPERT_FF), D_MODEL ** -0.5),
        'w_down': nrm(ks[27], (DEPTH, N_EXPERTS, EXPERT_FF, D_MODEL), EXPERT_FF ** -0.5),
        'final_norm_g': gain(ks[28], (D_MODEL,)),
    }


def reference(x, c, ctx, c_ctx, w_mod, b_mod, norm1_g, norm2_g, w_in, conv_w, conv_b,
              dt_bias_f, dt_bias_b, a_log_f, a_log_b, d_skip, ssd_norm_g, gk_up_f, gk_bias_f,
              gk_up_b, gk_bias_b, gla_norm_g, w_out, w_router, router_bias, w_gate, w_up, w_down,
              final_norm_g):
    sc = jax.nn.silu(c)
    scc = jax.nn.silu(c_ctx)
    for l in range(DEPTH):
        last = l == DEPTH - 1
        p = {
            'w_in': w_in[l], 'conv_w': conv_w[l], 'conv_b': conv_b[l],
            'dt_bias_f': dt_bias_f[l], 'dt_bias_b': dt_bias_b[l],
            'a_log_f': a_log_f[l], 'a_log_b': a_log_b[l], 'd_skip': d_skip[l],
            'ssd_norm_g': ssd_norm_g[l], 'gk_up_f': gk_up_f[l], 'gk_bias_f': gk_bias_f[l],
            'gk_up_b': gk_up_b[l], 'gk_bias_b': gk_bias_b[l], 'gla_norm_g': gla_norm_g[l],
            'w_out': w_out[l],
        }
        m_lat = jnp.split((sc @ w_mod[l] + b_mod[l])[:, None, :], 6, axis=-1)
        m_ctx = jnp.split(scc @ w_mod[l] + b_mod[l], 6, axis=-1)
        h_lat = modulate(rmsnorm(x, norm1_g[l]), m_lat[0], m_lat[1])
        h_ctx = modulate(rmsnorm(ctx, norm1_g[l]), m_ctx[0], m_ctx[1])
        mix_ctx, mix_lat = token_mixers(h_ctx, h_lat, p, not last)
        x = x + m_lat[2] * mix_lat
        h2_lat = modulate(rmsnorm(x, norm2_g[l]), m_lat[3], m_lat[4])
        if last:
            ffn = moe(h2_lat.reshape(-1, D_MODEL), w_router, router_bias, w_gate[l], w_up[l], w_down[l])
            x = x + m_lat[5] * ffn.reshape(x.shape)
        else:
            ctx = ctx + m_ctx[2] * mix_ctx
            h2_ctx = modulate(rmsnorm(ctx, norm2_g[l]), m_ctx[3], m_ctx[4])
            n_ctx = h2_ctx.shape[0] * h2_ctx.shape[1]
            tokens = jnp.concatenate([h2_ctx.reshape(-1, D_MODEL), h2_lat.reshape(-1, D_MODEL)], axis=0)
            ffn = moe(tokens, w_router, router_bias, w_gate[l], w_up[l], w_down[l])
            ctx = ctx + m_ctx[5] * ffn[:n_ctx].reshape(ctx.shape)
            x = x + m_lat[5] * ffn[n_ctx:].reshape(x.shape)
    return rmsnorm(x, final_norm_g)
```

```python
import functools

import jax
import jax.numpy as jnp
from jax import lax
from jax.experimental import pallas as pl
from jax.experimental.pallas import tpu as pltpu

f32 = jnp.float32
bf16 = jnp.bfloat16
i32 = jnp.int32

D_MODEL = 1024
SSD_HEADS = 8
SSD_HEAD_DIM = 64
SSD_WIDTH = 512
SSD_GROUPS = 2
SSD_STATE = 64
SSD_XBC = 768
GLA_HEADS = 4
GLA_KEY_DIM = 64
GLA_VAL_DIM = 128
GLA_QK = 256
GLA_WIDTH = 512
GLA_GATE_RANK = 16
GLA_GATE_NORM = 16.0
GRID_W = 64
N_EXPERTS = 32
N_EXPERT_GROUPS = 8
EXPERTS_PER_GROUP = 4
TOP_K = 2
EXPERT_FF = 512
NORM_EPS = 1e-6

LANES = 128
CHUNK = 128
MAIN_COLS = 2816
SMALL_COLS = LANES
TM_LAT = 512
ROUTE_TILE = 512
DISPATCH_TILE = 512
COMBINE_TILE = 256
EXPERT_BLOCK = 256
VMEM_LIMIT = 48 * 1024 * 1024

_HI = lax.Precision.HIGHEST
_NT = (((1,), (1,)), ((), ()))


def _dot(a, b, precision=None):
    return jnp.dot(a, b, preferred_element_type=f32, precision=precision)


def _silu(x):
    return x * jax.nn.sigmoid(x)


def _cparams(sem):
    return pltpu.CompilerParams(dimension_semantics=sem, vmem_limit_bytes=VMEM_LIMIT)


def _mod_kernel(c_ref, w_ref, b_ref, o_ref):
    sc = _silu(c_ref[...]).astype(bf16)
    o_ref[...] = _dot(sc, w_ref[...].astype(bf16)) + b_ref[...]


def _modulation(cvec, w_mod, b_mod):
    depth, d, n = w_mod.shape
    tn = 1536
    return pl.pallas_call(
        _mod_kernel,
        out_shape=jax.ShapeDtypeStruct((depth, 16, n), f32),
        grid=(depth, n // tn),
        in_specs=[pl.BlockSpec((16, d), lambda l, j: (0, 0)),
                  pl.BlockSpec((None, d, tn), lambda l, j: (l, 0, j)),
                  pl.BlockSpec((None, 1, tn), lambda l, j: (l, 0, j))],
        out_specs=pl.BlockSpec((None, 16, tn), lambda l, j: (l, 0, j)),
        compiler_params=_cparams(("arbitrary", "arbitrary")),
        name="modulation",
    )(cvec, w_mod, b_mod.reshape(depth, 1, n))


def _inproj_kernel(x_ref, mod_ref, g_ref, w_ref, cw_ref, cb_ref,
                   z_ref, xs_ref, bc_ref, q_ref, k_ref, v_ref, go_ref, sm_ref, *, rowlen):
    x = x_ref[...]
    tm = x.shape[0]
    inv = lax.rsqrt(jnp.mean(x * x, axis=-1, keepdims=True) + NORM_EPS)
    m = mod_ref[...]
    geff = g_ref[...] * (1.0 + m[1:2])
    h = ((x * inv) * geff + m[0:1]).astype(bf16)

    def proj(lo, hi):
        return _dot(h, w_ref[:, lo:hi])

    z_ref[...] = proj(0, 512).astype(bf16)

    xbc = proj(512, 1280)
    pos = lax.broadcasted_iota(i32, xbc.shape, 0) & (rowlen - 1)
    cw = cw_ref[...]
    acc = xbc * cw[2:3]
    for d in (-2, -1, 1, 2):
        shifted = pltpu.roll(xbc, (-d) % tm, 0)
        valid = (pos >= -d) if d < 0 else (pos <= rowlen - 1 - d)
        acc = acc + jnp.where(valid, shifted, 0.0) * cw[2 + d:3 + d]
    act = _silu(acc + cb_ref[...])
    xs_ref[...] = act[:, :SSD_WIDTH].astype(bf16)
    bc_ref[...] = act[:, SSD_WIDTH:].astype(bf16)

    qk = proj(1280, 1792)
    q_ref[...] = qk[:, :GLA_QK].astype(bf16)
    k_ref[...] = qk[:, GLA_QK:].astype(bf16)
    v_ref[...] = proj(1792, 2304).astype(bf16)
    go_ref[...] = proj(2304, 2816).astype(bf16)
    sm_ref[...] = proj(2816, 2944)


def _inproj(x2d, mod_l, norm_g, w_packed, conv_w8, conv_b, *, tm, rowlen, tiles_per_row, fixed_row):
    t = x2d.shape[0]
    nt = t // tm
    if fixed_row is None:
        mod_map = lambda i: (i // tiles_per_row, 0, 0)
    else:
        mod_map = lambda i: (fixed_row, 0, 0)
    tok = lambda w: pl.BlockSpec((tm, w), lambda i: (i, 0))
    const = lambda a: pl.BlockSpec(a.shape, lambda i: (0,) * a.ndim)
    widths = (512, 512, 256, 256, 256, 512, 512)
    return pl.pallas_call(
        functools.partial(_inproj_kernel, rowlen=rowlen),
        out_shape=tuple(jax.ShapeDtypeStruct((t, w), bf16) for w in widths)
        + (jax.ShapeDtypeStruct((t, SMALL_COLS), f32),),
        grid=(nt,),
        in_specs=[tok(D_MODEL), pl.BlockSpec((None, 6, D_MODEL), mod_map), const(norm_g),
                  const(w_packed), const(conv_w8), const(conv_b)],
        out_specs=tuple(tok(w) for w in widths) + (tok(SMALL_COLS),),
        compiler_params=_cparams(("arbitrary",)),
        name="inproj",
    )(x2d, mod_l, norm_g, w_packed, conv_w8, conv_b)


def _tri_masks(q):
    r = lax.broadcasted_iota(i32, (q, q), 0)
    c = lax.broadcasted_iota(i32, (q, q), 1)
    return r >= c, r <= c


def _ssd_kernel(xs_ref, bc_ref, sm_ref, h0_ref, dtb_ref, a_ref, dtbt_ref, at_ref, e_ref, bm_ref,
                y_ref, hfin_ref, state, *, rev, nc):
    c = pl.program_id(1)
    q = xs_ref.shape[0]

    @pl.when(c == 0)
    def _():
        state[...] = h0_ref[...]

    sm = sm_ref[...]
    dt_col = jax.nn.softplus(sm + dtb_ref[...])
    a_col = dt_col * a_ref[...]
    dt_row = jax.nn.softplus(sm.T[0:SSD_HEADS, :] + dtbt_ref[...])
    a_row = dt_row * at_ref[...]
    lower, upper = _tri_masks(q)
    lo_f = jnp.where(lower, 1.0, 0.0).astype(f32)
    up_f = jnp.where(upper, 1.0, 0.0).astype(f32)
    if not rev:
        cs_col = _dot(lo_f, a_col, _HI)
        cs_row = _dot(a_row, up_f, _HI)
        mask = lower
        a_tot = cs_col[q - 1:q, :]
    else:
        cs_col = _dot(up_f, a_col, _HI)
        cs_row = _dot(a_row, lo_f, _HI)
        mask = upper
        a_tot = cs_col[0:1, :]

    e = e_ref[...]
    w_exp = _dot((jnp.exp(a_tot - cs_col) * dt_col).astype(bf16), e)
    ecs_exp = _dot(jnp.exp(cs_col).astype(bf16), e)
    dec_exp = _dot(jnp.broadcast_to(jnp.exp(a_tot), (8, LANES)), e.astype(f32), _HI)[0:1]

    xb = xs_ref[...]
    xw = (xb.astype(f32) * w_exp).astype(bf16)
    bcv = bc_ref[...]
    b_all = bcv[:, 0:LANES]
    c_all = bcv[:, LANES:2 * LANES]
    b_t = b_all.astype(f32).T.astype(bf16)

    s_old = state[...]
    y_inter = _dot(c_all, s_old.astype(bf16)) * ecs_exp
    s_new = (s_old * dec_exp + _dot(b_t, xw)) * bm_ref[...]
    state[...] = s_new

    @pl.when(c == nc - 1)
    def _():
        hfin_ref[...] = s_new

    lane = lax.broadcasted_iota(i32, (q, LANES), 1)
    zero_b = jnp.zeros((q, LANES), bf16)
    ys = []
    for g in range(SSD_GROUPS):
        in_g = (lane >= SSD_STATE * g) & (lane < SSD_STATE * (g + 1))
        cb = _dot(jnp.where(in_g, c_all, zero_b), b_t)
        for pp in range(2):
            h0 = 4 * g + 2 * pp
            ms = []
            for h in (h0, h0 + 1):
                seg = cs_col[:, h:h + 1] - cs_row[h:h + 1, :]
                dec = jnp.exp(jnp.where(mask, seg, -1e30))
                ms.append((cb * dec * dt_row[h:h + 1, :]).astype(bf16))
            xp = xb[:, h0 * SSD_HEAD_DIM:h0 * SSD_HEAD_DIM + LANES]
            rhs = jnp.concatenate([jnp.where(lane < SSD_HEAD_DIM, xp, zero_b),
                                   jnp.where(lane >= SSD_HEAD_DIM, xp, zero_b)], axis=0)
            ys.append(_dot(jnp.concatenate(ms, axis=1), rhs))
    y_ref[...] = (jnp.concatenate(ys, axis=1) + y_inter).astype(bf16)


def _ssd_scan(xs, bc, sm, h0, dtb_row, a_row, dtb_t, a_t, e_mat, bmask, *, nb, rev):
    t = xs.shape[0]
    nc = t // (nb * CHUNK)
    if rev:
        tmap = lambda b, c: (b * nc + nc - 1 - c, 0)
    else:
        tmap = lambda b, c: (b * nc + c, 0)
    tok = lambda w: pl.BlockSpec((CHUNK, w), tmap)
    const = lambda a: pl.BlockSpec(a.shape, lambda b, c: (0,) * a.ndim)
    st = pl.BlockSpec((None, LANES, SSD_WIDTH), lambda b, c: (b, 0, 0))
    return pl.pallas_call(
        functools.partial(_ssd_kernel, rev=rev, nc=nc),
        out_shape=(jax.ShapeDtypeStruct((t, SSD_WIDTH), bf16),
                   jax.ShapeDtypeStruct((nb, LANES, SSD_WIDTH), f32)),
        grid=(nb, nc),
        in_specs=[tok(SSD_WIDTH), tok(2 * LANES), tok(SMALL_COLS), st,
                  const(dtb_row), const(a_row), const(dtb_t), const(a_t), const(e_mat), const(bmask)],
        out_specs=(tok(SSD_WIDTH), st),
        scratch_shapes=[pltpu.VMEM((LANES, SSD_WIDTH), f32)],
        compiler_params=_cparams(("arbitrary", "arbitrary")),
        name="ssd_rev" if rev else "ssd_fwd",
    )(xs, bc, sm, h0, dtb_row, a_row, dtb_t, a_t, e_mat, bmask)


def _gla_kernel(q_ref, k_ref, v_ref, sm_ref, s0_ref, gup_ref, gb_ref, bm_ref,
                o_ref, sfin_ref, state, *, rev, nc):
    c = pl.program_id(1)
    q = q_ref.shape[0]

    @pl.when(c == 0)
    def _():
        state[...] = s0_ref[...]

    gp = _dot(sm_ref[...].astype(bf16), gup_ref[...]) + gb_ref[...]
    g = jax.nn.log_sigmoid(gp) * (1.0 / GLA_GATE_NORM)
    lower, upper = _tri_masks(q)
    if not rev:
        b = _dot(jnp.where(lower, 1.0, 0.0).astype(f32), g, _HI)
        mask, mid = lower, q // 2 - 1
        b_tot = b[q - 1:q, :]
    else:
        b = _dot(jnp.where(upper, 1.0, 0.0).astype(f32), g, _HI)
        mask, mid = upper, q // 2
        b_tot = b[0:1, :]
    b_mid = b[mid:mid + 1, :]

    qf = q_ref[...].astype(f32) * (GLA_KEY_DIM ** -0.5)
    kf = k_ref[...].astype(f32)
    qd = (qf * jnp.exp(b - b_mid)).astype(bf16)
    ki = (kf * jnp.exp(b_mid - b)).astype(bf16)
    q_st = (qf * jnp.exp(b)).astype(bf16)
    k_end = (kf * jnp.exp(b_tot - b)).astype(bf16)
    vb = v_ref[...]
    v_t = vb.astype(f32).T.astype(bf16)

    s_old = state[...]
    o_inter = lax.dot_general(q_st, s_old.astype(bf16), _NT, preferred_element_type=f32)
    s_new = (s_old * jnp.exp(b_tot) + _dot(v_t, k_end)) * bm_ref[...]
    state[...] = s_new

    @pl.when(c == nc - 1)
    def _():
        sfin_ref[...] = s_new

    lane = lax.broadcasted_iota(i32, (q, LANES), 1)
    zero_b = jnp.zeros((q, LANES), bf16)
    outs = []
    for h in range(GLA_HEADS):
        p, hh = divmod(h, 2)
        qp = qd[:, p * LANES:(p + 1) * LANES]
        kp = ki[:, p * LANES:(p + 1) * LANES]
        in_h = (lane >= GLA_KEY_DIM * hh) & (lane < GLA_KEY_DIM * (hh + 1))
        s = lax.dot_general(jnp.where(in_h, qp, zero_b), kp, _NT, preferred_element_type=f32)
        attn = jnp.where(mask, s, 0.0).astype(bf16)
        outs.append(_dot(attn, vb[:, h * GLA_VAL_DIM:(h + 1) * GLA_VAL_DIM]))
    o_ref[...] = (jnp.concatenate(outs, axis=1) + o_inter).astype(bf16)


def _gla_scan(qa, ka, va, sm, s0, gup_pad, gbias, bmask, *, nb, rev):
    t = qa.shape[0]
    nc = t // (nb * CHUNK)
    if rev:
        tmap = lambda b, c: (b * nc + nc - 1 - c, 0)
    else:
        tmap = lambda b, c: (b * nc + c, 0)
    tok = lambda w: pl.BlockSpec((CHUNK, w), tmap)
    const = lambda a: pl.BlockSpec(a.shape, lambda b, c: (0,) * a.ndim)
    st = pl.BlockSpec((None, GLA_WIDTH, GLA_QK), lambda b, c: (b, 0, 0))
    return pl.pallas_call(
        functools.partial(_gla_kernel, rev=rev, nc=nc),
        out_shape=(jax.ShapeDtypeStruct((t, GLA_WIDTH), bf16),
                   jax.ShapeDtypeStruct((nb, GLA_WIDTH, GLA_QK), f32)),
        grid=(nb, nc),
        in_specs=[tok(GLA_QK), tok(GLA_QK), tok(GLA_WIDTH), tok(SMALL_COLS), st,
                  const(gup_pad), const(gbias), const(bmask)],
        out_specs=(tok(GLA_WIDTH), st),
        scratch_shapes=[pltpu.VMEM((GLA_WIDTH, GLA_QK), f32)],
        compiler_params=_cparams(("arbitrary", "arbitrary")),
        name="gla_rev" if rev else "gla_fwd",
    )(qa, ka, va, sm, s0, gup_pad, gbias, bmask)


def _outproj_kernel(yf_ref, yb_ref, xs_ref, z_ref, of_ref, ob_ref, go_ref, xres_ref, mod_ref,
                    dsk_ref, sg_ref, gg_ref, wout_ref, n2g_ref, wr_ref,
                    xnew_ref, h2_ref, lg_ref):
    up = lambda r: r[...].astype(f32)
    y = up(yf_ref) + up(yb_ref) + dsk_ref[...] * up(xs_ref)
    y = y * _silu(up(z_ref))
    y = y * lax.rsqrt(jnp.mean(y * y, axis=-1, keepdims=True) + NORM_EPS) * sg_ref[...]
    o = up(of_ref) + up(ob_ref)
    parts = []
    for h in range(GLA_HEADS):
        oh = o[:, h * GLA_VAL_DIM:(h + 1) * GLA_VAL_DIM]
        parts.append(oh * lax.rsqrt(jnp.mean(oh * oh, axis=-1, keepdims=True) + NORM_EPS))
    o = jnp.concatenate(parts, axis=1) * gg_ref[...] * _silu(up(go_ref))
    cat = jnp.concatenate([y, o], axis=1).astype(bf16)
    mix = _dot(cat, wout_ref[...])
    m = mod_ref[...]
    xn = xres_ref[...] + m[2:3] * mix
    xnew_ref[...] = xn
    inv = lax.rsqrt(jnp.mean(xn * xn, axis=-1, keepdims=True) + NORM_EPS)
    h2 = (xn * inv) * (n2g_ref[...] * (1.0 + m[4:5])) + m[3:4]
    h2_ref[...] = h2
    lg_ref[...] = _dot(h2, wr_ref[...], _HI)


def _outproj(yf, yb, xs, z, of, ob, go, xres, mod_l, dsk, sg, gg, wout, n2g, wr,
             *, tm, tiles_per_row, fixed_row):
    t = xres.shape[0]
    if fixed_row is None:
        mod_map = lambda i: (i // tiles_per_row, 0, 0)
    else:
        mod_map = lambda i: (fixed_row, 0, 0)
    tok = lambda w: pl.BlockSpec((tm, w), lambda i: (i, 0))
    const = lambda a: pl.BlockSpec(a.shape, lambda i: (0,) * a.ndim)
    return pl.pallas_call(
        _outproj_kernel,
        out_shape=(jax.ShapeDtypeStruct((t, D_MODEL), f32), jax.ShapeDtypeStruct((t, D_MODEL), f32),
                   jax.ShapeDtypeStruct((t, LANES), f32)),
        grid=(t // tm,),
        in_specs=[tok(512)] * 7 + [tok(D_MODEL), pl.BlockSpec((None, 6, D_MODEL), mod_map),
                                   const(dsk), const(sg), const(gg), const(wout), const(n2g), const(wr)],
        out_specs=(tok(D_MODEL), tok(D_MODEL), tok(LANES)),
        compiler_params=_cparams(("arbitrary",)),
        name="outproj",
    )(yf, yb, xs, z, of, ob, go, xres, mod_l, dsk, sg, gg, wout, n2g, wr)


def _route_kernel(lg_ref, bias_ref, su_ref, ids_ref, rank_ref, gcol_ref, cnt_ref, carry):
    i = pl.program_id(0)
    tm = lg_ref.shape[0]

    @pl.when(i == 0)
    def _():
        carry[...] = jnp.zeros_like(carry)

    s = jax.nn.sigmoid(lg_ref[...].T[0:N_EXPERTS, :])
    sel = s + bias_ref[...]
    a = [sel[8 * m:8 * (m + 1)] for m in range(EXPERTS_PER_GROUP)]
    sv = [s[8 * m:8 * (m + 1)] for m in range(EXPERTS_PER_GROUP)]
    hi01, lo01 = jnp.maximum(a[0], a[1]), jnp.minimum(a[0], a[1])
    hi23, lo23 = jnp.maximum(a[2], a[3]), jnp.minimum(a[2], a[3])
    gscore = jnp.maximum(hi01, hi23) + jnp.maximum(jnp.minimum(hi01, hi23), jnp.maximum(lo01, lo23))
    giota = lax.broadcasted_iota(i32, gscore.shape, 0)
    gmax = jnp.max(gscore, axis=0, keepdims=True)
    gidx = jnp.min(jnp.where(gscore == gmax, giota, N_EXPERT_GROUPS), axis=0, keepdims=True)
    pick = giota == gidx
    v = [jnp.sum(jnp.where(pick, a[m], 0.0), axis=0, keepdims=True) for m in range(4)]
    w = [jnp.sum(jnp.where(pick, sv[m], 0.0), axis=0, keepdims=True) for m in range(4)]

    def first_max(vals, excluded):
        best = vals[0]
        for m in range(1, 4):
            best = jnp.maximum(best, vals[m])
        idx = jnp.full(best.shape, 3, i32)
        for m in (2, 1, 0):
            hit = vals[m] == best
            if excluded is not None:
                hit = hit & (excluded != m)
            idx = jnp.where(hit, m, idx)
        return idx

    i1 = first_max(v, None)
    v_rest = [jnp.where(i1 == m, -jnp.inf, v[m]) for m in range(4)]
    i2 = first_max(v_rest, i1)

    def take(vals, idx):
        out = vals[3]
        for m in (2, 1, 0):
            out = jnp.where(idx == m, vals[m], out)
        return out

    w1, w2 = take(w, i1), take(w, i2)
    denom = w1 + w2
    id1 = gidx * EXPERTS_PER_GROUP + i1
    id2 = gidx * EXPERTS_PER_GROUP + i2
    row2 = lax.broadcasted_iota(i32, (TOP_K, tm), 0)
    ids_ref[...] = jnp.where(row2 == 0, id1, id2)

    eiota = lax.broadcasted_iota(i32, (N_EXPERTS, tm), 0)
    hit1, hit2 = eiota == id1, eiota == id2
    onehot = jnp.where(hit1, 1.0, 0.0) + jnp.where(hit2, 1.0, 0.0)
    before = _dot(onehot.astype(bf16), su_ref[...]) + carry[...]
    r1 = jnp.sum(jnp.where(hit1, before, 0.0), axis=0, keepdims=True)
    r2 = jnp.sum(jnp.where(hit2, before, 0.0), axis=0, keepdims=True)
    rank_ref[...] = jnp.where(row2 == 0, r1, r2).astype(i32)
    new_carry = carry[...] + jnp.sum(onehot, axis=1, keepdims=True)
    carry[...] = new_carry
    cnt_ref[...] = jnp.broadcast_to(new_carry, cnt_ref.shape)

    rows = lax.broadcasted_iota(i32, (LANES, tm), 0)
    gates = jnp.where(rows == 0, w1 / denom, jnp.where(rows == 1, w2 / denom, 0.0))
    gcol_ref[...] = gates.T


def _route(logits, bias_col, su):
    t = logits.shape[0]
    tm = ROUTE_TILE
    return pl.pallas_call(
        _route_kernel,
        out_shape=(jax.ShapeDtypeStruct((2, t), i32), jax.ShapeDtypeStruct((2, t), i32),
                   jax.ShapeDtypeStruct((t, LANES), f32), jax.ShapeDtypeStruct((N_EXPERTS, LANES), f32)),
        grid=(t // tm,),
        in_specs=[pl.BlockSpec((tm, LANES), lambda i: (i, 0)),
                  pl.BlockSpec((N_EXPERTS, 1), lambda i: (0, 0)),
                  pl.BlockSpec((tm, tm), lambda i: (0, 0))],
        out_specs=(pl.BlockSpec((2, tm), lambda i: (0, i)), pl.BlockSpec((2, tm), lambda i: (0, i)),
                   pl.BlockSpec((tm, LANES), lambda i: (i, 0)),
                   pl.BlockSpec((N_EXPERTS, LANES), lambda i: (0, 0))),
        scratch_shapes=[pltpu.VMEM((N_EXPERTS, 1), f32)],
        compiler_params=_cparams(("arbitrary",)),
        name="route",
    )(logits, bias_col, su)


def _row_copy(src_ref, src_row, dst_ref, dst_row, sem):
    return pltpu.make_async_copy(src_ref.at[pl.ds(src_row, 1)], dst_ref.at[pl.ds(dst_row, 1)], sem)


def _dispatch_kernel(dest_ref, h_ref, buf_in_ref, buf_ref, sem):
    del buf_in_ref
    i = pl.program_id(0)
    tm = dest_ref.shape[1]

    def issue(j, carry):
        t = i * tm + j
        _row_copy(h_ref, t, buf_ref, dest_ref[0, j], sem).start()
        _row_copy(h_ref, t, buf_ref, dest_ref[1, j], sem).start()
        return carry

    lax.fori_loop(0, tm, issue, 0)

    def drain(j, carry):
        _row_copy(h_ref, 0, buf_ref, 0, sem).wait()
        _row_copy(h_ref, 0, buf_ref, 0, sem).wait()
        return carry

    lax.fori_loop(0, tm, drain, 0)


def _dispatch(dest_tiles, h2, buf):
    nt, _, tm = dest_tiles.shape
    return pl.pallas_call(
        _dispatch_kernel,
        out_shape=jax.ShapeDtypeStruct(buf.shape, buf.dtype),
        grid=(nt,),
        in_specs=[pl.BlockSpec((None, 2, tm), lambda i: (i, 0, 0), memory_space=pltpu.SMEM),
                  pl.BlockSpec(memory_space=pl.ANY), pl.BlockSpec(memory_space=pl.ANY)],
        out_specs=pl.BlockSpec(memory_space=pl.ANY),
        scratch_shapes=[pltpu.SemaphoreType.DMA],
        input_output_aliases={2: 0},
        compiler_params=_cparams(("arbitrary",)),
        name="dispatch",
    )(dest_tiles, h2, buf)


def _expert_kernel(be_ref, nu_ref, x_ref, wg_ref, wu_ref, wd_ref, y_ref, wg_b, wu_b, wd_b):
    i = pl.program_id(0)
    fresh = jnp.logical_or(i == 0, be_ref[i] != be_ref[jnp.maximum(i - 1, 0)])

    @pl.when(jnp.logical_and(fresh, i < nu_ref[0]))
    def _():
        wg_b[...] = wg_ref[...].astype(bf16)
        wu_b[...] = wu_ref[...].astype(bf16)
        wd_b[...] = wd_ref[...].astype(bf16)

    @pl.when(i < nu_ref[0])
    def _():
        xb = x_ref[...].astype(bf16)
        g = _dot(xb, wg_b[...])
        u = _dot(xb, wu_b[...])
        y_ref[...] = _dot((_silu(g) * u).astype(bf16), wd_b[...])

    @pl.when(i >= nu_ref[0])
    def _():
        y_ref[...] = jnp.zeros_like(y_ref)


def _experts(block_e, n_used, buf, w_gate, w_up, w_down):
    p, d = buf.shape
    nb = p // EXPERT_BLOCK
    ff = w_gate.shape[-1]
    return pl.pallas_call(
        _expert_kernel,
        out_shape=jax.ShapeDtypeStruct((p, d), f32),
        grid_spec=pltpu.PrefetchScalarGridSpec(
            num_scalar_prefetch=2,
            grid=(nb,),
            in_specs=[pl.BlockSpec((EXPERT_BLOCK, d), lambda i, be, nu: (i, 0)),
                      pl.BlockSpec((None, d, ff), lambda i, be, nu: (be[i], 0, 0)),
                      pl.BlockSpec((None, d, ff), lambda i, be, nu: (be[i], 0, 0)),
                      pl.BlockSpec((None, ff, d), lambda i, be, nu: (be[i], 0, 0))],
            out_specs=pl.BlockSpec((EXPERT_BLOCK, d), lambda i, be, nu: (i, 0)),
            scratch_shapes=[pltpu.VMEM((d, ff), bf16), pltpu.VMEM((d, ff), bf16), pltpu.VMEM((ff, d), bf16)],
        ),
        compiler_params=_cparams(("arbitrary",)),
        name="experts",
    )(block_e, n_used, buf, w_gate, w_up, w_down)


def _combine_kernel(dest_ref, y_ref, gcol_ref, x_ref, mod_ref, fg_ref, o_ref, ybuf, sem, *, final):
    tm = dest_ref.shape[1]

    def issue(j, carry):
        _row_copy(y_ref, dest_ref[0, j], ybuf.at[0], j, sem).start()
        _row_copy(y_ref, dest_ref[1, j], ybuf.at[1], j, sem).start()
        return carry

    lax.fori_loop(0, tm, issue, 0)

    def drain(j, carry):
        _row_copy(y_ref, 0, ybuf.at[0], 0, sem).wait()
        _row_copy(y_ref, 0, ybuf.at[1], 0, sem).wait()
        return carry

    lax.fori_loop(0, tm, drain, 0)

    gc = gcol_ref[...]
    ffn = gc[:, 0:1] * ybuf[0] + gc[:, 1:2] * ybuf[1]
    x = x_ref[...] + mod_ref[...][5:6] * ffn
    if final:
        x = x * lax.rsqrt(jnp.mean(x * x, axis=-1, keepdims=True) + NORM_EPS) * fg_ref[...]
    o_ref[...] = x


def _combine(dest_tiles, y, gcol, xnew, mod_l, final_g, *, tiles_per_row, fixed_row, final):
    nt, _, tm = dest_tiles.shape
    t = xnew.shape[0]
    if fixed_row is None:
        mod_map = lambda i: (i // tiles_per_row, 0, 0)
    else:
        mod_map = lambda i: (fixed_row, 0, 0)
    return pl.pallas_call(
        functools.partial(_combine_kernel, final=final),
        out_shape=jax.ShapeDtypeStruct((t, D_MODEL), f32),
        grid=(nt,),
        in_specs=[pl.BlockSpec((None, 2, tm), lambda i: (i, 0, 0), memory_space=pltpu.SMEM),
                  pl.BlockSpec(memory_space=pl.ANY),
                  pl.BlockSpec((tm, LANES), lambda i: (i, 0)),
                  pl.BlockSpec((tm, D_MODEL), lambda i: (i, 0)),
                  pl.BlockSpec((None, 6, D_MODEL), mod_map),
                  pl.BlockSpec((1, D_MODEL), lambda i: (0, 0))],
        out_specs=pl.BlockSpec((tm, D_MODEL), lambda i: (i, 0)),
        scratch_shapes=[pltpu.VMEM((2, tm, D_MODEL), f32), pltpu.SemaphoreType.DMA],
        compiler_params=_cparams(("arbitrary",)),
        name="combine",
    )(dest_tiles, y, gcol, xnew, mod_l, final_g)


def _dest_tiles(dest, tm):
    t = dest.shape[1]
    return dest.reshape(2, t // tm, tm).transpose(1, 0, 2)


def _moe(h2_parts, logits, router_bias_col, su, w_gate, w_up, w_down):
    t = logits.shape[0]
    ids, rank, gcol, cnt = _route(logits, router_bias_col, su)
    counts = cnt[:, 0].astype(i32)
    padded = (counts + EXPERT_BLOCK - 1) // EXPERT_BLOCK * EXPERT_BLOCK
    pend = jnp.cumsum(padded)
    pstart = pend - padded
    nb = (t * TOP_K) // EXPERT_BLOCK + N_EXPERTS
    block_e = jnp.minimum(jnp.searchsorted(pend, jnp.arange(nb, dtype=i32) * EXPERT_BLOCK, side='right'),
                          N_EXPERTS - 1).astype(i32)
    n_used = (pend[-1:] // EXPERT_BLOCK).astype(i32)
    dest = pstart[ids] + rank
    buf = jnp.zeros((nb * EXPERT_BLOCK, D_MODEL), f32)
    off = 0
    dests = []
    for part in h2_parts:
        n = part.shape[0]
        d_part = dest[:, off:off + n]
        buf = _dispatch(_dest_tiles(d_part, DISPATCH_TILE), part, buf)
        dests.append((d_part, gcol[off:off + n]))
        off += n
    y = _experts(block_e, n_used, buf, w_gate, w_up, w_down)
    return dests, y


def _pack_layer(l, w_in, conv_w, conv_b, dt_bias_f, dt_bias_b, a_log_f, a_log_b, d_skip, ssd_norm_g,
                gk_up_f, gk_bias_f, gk_up_b, gk_bias_b, gla_norm_g, w_out, norm1_g, norm2_g):
    w = w_in[l]
    z, xbc, dt, q, k, v, go, gk = jnp.split(w, [512, 1280, 1288, 1544, 1800, 2312, 2824], axis=1)
    small = jnp.concatenate([dt, gk, jnp.zeros((D_MODEL, SMALL_COLS - 24), f32)], axis=1)
    w_packed = jnp.concatenate([z, xbc, q, k, v, go, small], axis=1).astype(bf16)

    def lane_row(vec):
        return jnp.zeros((1, LANES), f32).at[0, :SSD_HEADS].set(vec)

    def sub_col(vec):
        return jnp.broadcast_to(vec[:, None], (SSD_HEADS, CHUNK)).astype(f32)

    def gup(m):
        return jnp.zeros((SMALL_COLS, GLA_QK), f32).at[SSD_HEADS:SSD_HEADS + GLA_GATE_RANK].set(m).astype(bf16)

    a_f = -jnp.exp(a_log_f[l])
    a_b = -jnp.exp(a_log_b[l])
    return dict(
        w_packed=w_packed,
        conv_w8=jnp.zeros((8, SSD_XBC), f32).at[:5].set(conv_w[l]),
        conv_b=conv_b[l][None, :],
        ssd_f=(lane_row(dt_bias_f[l]), lane_row(a_f), sub_col(dt_bias_f[l]), sub_col(a_f)),
        ssd_b=(lane_row(dt_bias_b[l]), lane_row(a_b), sub_col(dt_bias_b[l]), sub_col(a_b)),
        gla_f=(gup(gk_up_f[l]), gk_bias_f[l][None, :]),
        gla_b=(gup(gk_up_b[l]), gk_bias_b[l][None, :]),
        dsk=jnp.repeat(d_skip[l], SSD_HEAD_DIM)[None, :],
        sg=ssd_norm_g[l][None, :],
        gg=jnp.tile(gla_norm_g[l], GLA_HEADS)[None, :],
        wout=w_out[l].astype(bf16),
        n1g=norm1_g[l][None, :],
        n2g=norm2_g[l][None, :],
    )


def _constants():
    r = jnp.arange(LANES)[:, None]
    c = jnp.arange(SSD_WIDTH)[None, :]
    e_mat = ((c // SSD_HEAD_DIM) == r).astype(bf16)
    ssd_mask = ((r // SSD_STATE) == (c // (SSD_WIDTH // SSD_GROUPS))).astype(f32)
    rr = jnp.arange(GLA_WIDTH)[:, None]
    cc = jnp.arange(GLA_QK)[None, :]
    gla_mask = ((rr // GLA_VAL_DIM) == (cc // GLA_KEY_DIM)).astype(f32)
    k = jnp.arange(ROUTE_TILE)
    su = (k[:, None] < k[None, :]).astype(bf16)
    return e_mat, ssd_mask, gla_mask, su


def _mixers(streams, pk, consts, nb):
    e_mat, ssd_mask, gla_mask, _ = consts
    out = {}
    h0_f = jnp.zeros((nb, LANES, SSD_WIDTH), f32)
    h0_b = h0_f
    s0_f = jnp.zeros((nb, GLA_WIDTH, GLA_QK), f32)
    s0_b = s0_f
    for name in ('ctx', 'lat'):
        z, xs, bc, q, k, v, go, sm = streams[name]
        yf, h0_f = _ssd_scan(xs, bc, sm, h0_f, *pk['ssd_f'], e_mat, ssd_mask, nb=nb, rev=False)
        yb, h0_b = _ssd_scan(xs, bc, sm, h0_b, *pk['ssd_b'], e_mat, ssd_mask, nb=nb, rev=True)
        of, s0_f = _gla_scan(q, k, v, sm, s0_f, *pk['gla_f'], gla_mask, nb=nb, rev=False)
        ob, s0_b = _gla_scan(q, k, v, sm, s0_b, *pk['gla_b'], gla_mask, nb=nb, rev=True)
        out[name] = (yf, yb, of, ob)
    return out


def kernel(x, c, ctx, c_ctx, w_mod, b_mod, norm1_g, norm2_g, w_in, conv_w, conv_b, dt_bias_f, dt_bias_b, a_log_f, a_log_b, d_skip, ssd_norm_g, gk_up_f, gk_bias_f, gk_up_b, gk_bias_b, gla_norm_g, w_out, w_router, router_bias, w_gate, w_up, w_down, final_norm_g):
    nb, seq, d = x.shape
    ctx_len = ctx.shape[1]
    depth = w_mod.shape[0]
    consts = _constants()
    su = consts[3]

    cvec = jnp.zeros((16, d), f32).at[:nb].set(c).at[nb].set(c_ctx)
    mod = _modulation(cvec, w_mod, b_mod).reshape(depth, 16, 6, d)
    ctx_row = nb

    perm = jnp.array([g * EXPERTS_PER_GROUP + m for m in range(EXPERTS_PER_GROUP)
                      for g in range(N_EXPERT_GROUPS)], dtype=i32)
    wr = jnp.zeros((d, LANES), f32).at[:, :N_EXPERTS].set(w_router[:, perm])
    bias_col = router_bias[perm][:, None]

    x2 = x.reshape(nb * seq, d)
    c2 = ctx.reshape(nb * ctx_len, d)
    lat_tiles = seq // TM_LAT

    for l in range(depth):
        last = l == depth - 1
        pk = _pack_layer(l, w_in, conv_w, conv_b, dt_bias_f, dt_bias_b, a_log_f, a_log_b, d_skip, ssd_norm_g,
                         gk_up_f, gk_bias_f, gk_up_b, gk_bias_b, gla_norm_g, w_out, norm1_g, norm2_g)
        mod_l = mod[l]
        streams = {
            'ctx': _inproj(c2, mod_l, pk['n1g'], pk['w_packed'], pk['conv_w8'], pk['conv_b'],
                           tm=ctx_len, rowlen=ctx_len, tiles_per_row=1, fixed_row=ctx_row),
            'lat': _inproj(x2, mod_l, pk['n1g'], pk['w_packed'], pk['conv_w8'], pk['conv_b'],
                           tm=TM_LAT, rowlen=GRID_W, tiles_per_row=lat_tiles, fixed_row=None),
        }
        mix = _mixers(streams, pk, consts, nb)

        def merge(name, xres, tm, tiles_per_row, fixed_row):
            z, xs, bc, q, k, v, go, sm = streams[name]
            yf, yb, of, ob = mix[name]
            return _outproj(yf, yb, xs, z, of, ob, go, xres, mod_l, pk['dsk'], pk['sg'], pk['gg'],
                            pk['wout'], pk['n2g'], wr, tm=tm, tiles_per_row=tiles_per_row, fixed_row=fixed_row)

        xn_lat, h2_lat, lg_lat = merge('lat', x2, TM_LAT, lat_tiles, None)
        if last:
            parts, logits = [h2_lat], lg_lat
        else:
            xn_ctx, h2_ctx, lg_ctx = merge('ctx', c2, ctx_len, 1, ctx_row)
            parts, logits = [h2_ctx, h2_lat], jnp.concatenate([lg_ctx, lg_lat], axis=0)
        dests, y = _moe(parts, logits, bias_col, su, w_gate[l], w_up[l], w_down[l])
        if not last:
            (d_ctx, g_ctx), (d_lat, g_lat) = dests
            c2 = _combine(_dest_tiles(d_ctx, COMBINE_TILE), y, g_ctx, xn_ctx, mod_l, final_norm_g[None, :],
                          tiles_per_row=1, fixed_row=ctx_row, final=False)
        else:
            ((d_lat, g_lat),) = dests
        x2 = _combine(_dest_tiles(d_lat, COMBINE_TILE), y, g_lat, xn_lat, mod_l, final_norm_g[None, :],
                      tiles_per_row=seq // COMBINE_TILE, fixed_row=None, final=last)
    return x2.reshape(nb, seq, d)
```

```python
import functools

import jax
import jax.numpy as jnp
from jax import lax
from jax.experimental import pallas as pl
from jax.experimental.pallas import tpu as pltpu

f32 = jnp.float32
bf16 = jnp.bfloat16
i32 = jnp.int32

D_MODEL = 1024
SSD_HEADS = 8
SSD_HEAD_DIM = 64
SSD_WIDTH = 512
SSD_GROUPS = 2
SSD_STATE = 64
SSD_XBC = 768
GLA_HEADS = 4
GLA_KEY_DIM = 64
GLA_VAL_DIM = 128
GLA_QK = 256
GLA_WIDTH = 512
GLA_GATE_RANK = 16
GLA_GATE_NORM = 16.0
GRID_W = 64
N_EXPERTS = 32
N_EXPERT_GROUPS = 8
EXPERTS_PER_GROUP = 4
TOP_K = 2
EXPERT_FF = 512
NORM_EPS = 1e-6

LANES = 128
CHUNK = 128
MAIN_COLS = 2816
SMALL_COLS = LANES
TM_LAT = 512
ROUTE_TILE = 512
MOE_TILE = 512
EXPERT_BLOCK = 256
VMEM_LIMIT = 48 * 1024 * 1024

_HI = lax.Precision.HIGHEST
_NT = (((1,), (1,)), ((), ()))


def _dot(a, b, precision=None):
    return jnp.dot(a, b, preferred_element_type=f32, precision=precision)


def _silu(x):
    return x * jax.nn.sigmoid(x)


def _cparams(sem):
    return pltpu.CompilerParams(dimension_semantics=sem, vmem_limit_bytes=VMEM_LIMIT)


def _mod_kernel(c_ref, w_ref, b_ref, o_ref):
    sc = _silu(c_ref[...]).astype(bf16)
    o_ref[...] = _dot(sc, w_ref[...].astype(bf16)) + b_ref[...]


def _modulation(cvec, w_mod, b_mod):
    depth, d, n = w_mod.shape
    tn = 1536
    return pl.pallas_call(
        _mod_kernel,
        out_shape=jax.ShapeDtypeStruct((depth, 16, n), f32),
        grid=(depth, n // tn),
        in_specs=[pl.BlockSpec((16, d), lambda l, j: (0, 0)),
                  pl.BlockSpec((None, d, tn), lambda l, j: (l, 0, j)),
                  pl.BlockSpec((None, 1, tn), lambda l, j: (l, 0, j))],
        out_specs=pl.BlockSpec((None, 16, tn), lambda l, j: (l, 0, j)),
        compiler_params=_cparams(("arbitrary", "arbitrary")),
        name="modulation",
    )(cvec, w_mod, b_mod.reshape(depth, 1, n))


def _inproj_kernel(x_ref, mod_ref, g_ref, w_ref, cw_ref, cb_ref,
                   z_ref, xs_ref, bc_ref, q_ref, k_ref, v_ref, go_ref, sm_ref, *, rowlen):
    x = x_ref[...]
    tm = x.shape[0]
    inv = lax.rsqrt(jnp.mean(x * x, axis=-1, keepdims=True) + NORM_EPS)
    m = mod_ref[...]
    geff = g_ref[...] * (1.0 + m[1:2])
    h = ((x * inv) * geff + m[0:1]).astype(bf16)

    def proj(lo, hi):
        return _dot(h, w_ref[:, lo:hi])

    z_ref[...] = proj(0, 512).astype(bf16)

    xbc = proj(512, 1280)
    pos = lax.broadcasted_iota(i32, xbc.shape, 0) & (rowlen - 1)
    cw = cw_ref[...]
    acc = xbc * cw[2:3]
    for d in (-2, -1, 1, 2):
        shifted = pltpu.roll(xbc, (-d) % tm, 0)
        valid = (pos >= -d) if d < 0 else (pos <= rowlen - 1 - d)
        acc = acc + jnp.where(valid, shifted, 0.0) * cw[2 + d:3 + d]
    act = _silu(acc + cb_ref[...])
    xs_ref[...] = act[:, :SSD_WIDTH].astype(bf16)
    bc_ref[...] = act[:, SSD_WIDTH:].astype(bf16)

    qk = proj(1280, 1792)
    q_ref[...] = qk[:, :GLA_QK].astype(bf16)
    k_ref[...] = qk[:, GLA_QK:].astype(bf16)
    v_ref[...] = proj(1792, 2304).astype(bf16)
    go_ref[...] = proj(2304, 2816).astype(bf16)
    sm_ref[...] = proj(2816, 2944)


def _inproj(x2d, mod_l, norm_g, w_packed, conv_w8, conv_b, *, tm, rowlen, tiles_per_row, fixed_row):
    t = x2d.shape[0]
    nt = t // tm
    if fixed_row is None:
        mod_map = lambda i: (i // tiles_per_row, 0, 0)
    else:
        mod_map = lambda i: (fixed_row, 0, 0)
    tok = lambda w: pl.BlockSpec((tm, w), lambda i: (i, 0))
    const = lambda a: pl.BlockSpec(a.shape, lambda i: (0,) * a.ndim)
    widths = (512, 512, 256, 256, 256, 512, 512)
    return pl.pallas_call(
        functools.partial(_inproj_kernel, rowlen=rowlen),
        out_shape=tuple(jax.ShapeDtypeStruct((t, w), bf16) for w in widths)
        + (jax.ShapeDtypeStruct((t, SMALL_COLS), f32),),
        grid=(nt,),
        in_specs=[tok(D_MODEL), pl.BlockSpec((None, 6, D_MODEL), mod_map), const(norm_g),
                  const(w_packed), const(conv_w8), const(conv_b)],
        out_specs=tuple(tok(w) for w in widths) + (tok(SMALL_COLS),),
        compiler_params=_cparams(("arbitrary",)),
        name="inproj",
    )(x2d, mod_l, norm_g, w_packed, conv_w8, conv_b)


def _tri_masks(q):
    r = lax.broadcasted_iota(i32, (q, q), 0)
    c = lax.broadcasted_iota(i32, (q, q), 1)
    return r >= c, r <= c


def _ssd_kernel(xs_ref, bc_ref, sm_ref, h0_ref, dtb_ref, a_ref, dtbt_ref, at_ref, e_ref, bm_ref,
                y_ref, hfin_ref, state, *, rev, nc):
    c = pl.program_id(1)
    q = xs_ref.shape[0]

    @pl.when(c == 0)
    def _():
        state[...] = h0_ref[...]

    sm = sm_ref[...]
    dt_col = jax.nn.softplus(sm + dtb_ref[...])
    a_col = dt_col * a_ref[...]
    dt_row = jax.nn.softplus(sm.T[0:SSD_HEADS, :] + dtbt_ref[...])
    a_row = dt_row * at_ref[...]
    lower, upper = _tri_masks(q)
    lo_f = jnp.where(lower, 1.0, 0.0).astype(f32)
    up_f = jnp.where(upper, 1.0, 0.0).astype(f32)
    if not rev:
        cs_col = _dot(lo_f, a_col, _HI)
        cs_row = _dot(a_row, up_f, _HI)
        mask = lower
        a_tot = cs_col[q - 1:q, :]
    else:
        cs_col = _dot(up_f, a_col, _HI)
        cs_row = _dot(a_row, lo_f, _HI)
        mask = upper
        a_tot = cs_col[0:1, :]

    e = e_ref[...]
    w_exp = _dot((jnp.exp(a_tot - cs_col) * dt_col).astype(bf16), e)
    ecs_exp = _dot(jnp.exp(cs_col).astype(bf16), e)
    dec_exp = _dot(jnp.broadcast_to(jnp.exp(a_tot), (8, LANES)), e.astype(f32), _HI)[0:1]

    xb = xs_ref[...]
    xw = (xb.astype(f32) * w_exp).astype(bf16)
    bcv = bc_ref[...]
    b_all = bcv[:, 0:LANES]
    c_all = bcv[:, LANES:2 * LANES]
    b_t = b_all.astype(f32).T.astype(bf16)

    s_old = state[...]
    y_inter = _dot(c_all, s_old.astype(bf16)) * ecs_exp
    s_new = (s_old * dec_exp + _dot(b_t, xw)) * bm_ref[...]
    state[...] = s_new

    @pl.when(c == nc - 1)
    def _():
        hfin_ref[...] = s_new

    lane = lax.broadcasted_iota(i32, (q, LANES), 1)
    zero_b = jnp.zeros((q, LANES), bf16)
    ys = []
    for g in range(SSD_GROUPS):
        in_g = (lane >= SSD_STATE * g) & (lane < SSD_STATE * (g + 1))
        cb = _dot(jnp.where(in_g, c_all, zero_b), b_t)
        for pp in range(2):
            h0 = 4 * g + 2 * pp
            ms = []
            for h in (h0, h0 + 1):
                seg = cs_col[:, h:h + 1] - cs_row[h:h + 1, :]
                dec = jnp.exp(jnp.where(mask, seg, -1e30))
                ms.append((cb * dec * dt_row[h:h + 1, :]).astype(bf16))
            xp = xb[:, h0 * SSD_HEAD_DIM:h0 * SSD_HEAD_DIM + LANES]
            rhs = jnp.concatenate([jnp.where(lane < SSD_HEAD_DIM, xp, zero_b),
                                   jnp.where(lane >= SSD_HEAD_DIM, xp, zero_b)], axis=0)
            ys.append(_dot(jnp.concatenate(ms, axis=1), rhs))
    y_ref[...] = (jnp.concatenate(ys, axis=1) + y_inter).astype(bf16)


def _ssd_scan(xs, bc, sm, h0, dtb_row, a_row, dtb_t, a_t, e_mat, bmask, *, nb, rev):
    t = xs.shape[0]
    nc = t // (nb * CHUNK)
    if rev:
        tmap = lambda b, c: (b * nc + nc - 1 - c, 0)
    else:
        tmap = lambda b, c: (b * nc + c, 0)
    tok = lambda w: pl.BlockSpec((CHUNK, w), tmap)
    const = lambda a: pl.BlockSpec(a.shape, lambda b, c: (0,) * a.ndim)
    st = pl.BlockSpec((None, LANES, SSD_WIDTH), lambda b, c: (b, 0, 0))
    return pl.pallas_call(
        functools.partial(_ssd_kernel, rev=rev, nc=nc),
        out_shape=(jax.ShapeDtypeStruct((t, SSD_WIDTH), bf16),
                   jax.ShapeDtypeStruct((nb, LANES, SSD_WIDTH), f32)),
        grid=(nb, nc),
        in_specs=[tok(SSD_WIDTH), tok(2 * LANES), tok(SMALL_COLS), st,
                  const(dtb_row), const(a_row), const(dtb_t), const(a_t), const(e_mat), const(bmask)],
        out_specs=(tok(SSD_WIDTH), st),
        scratch_shapes=[pltpu.VMEM((LANES, SSD_WIDTH), f32)],
        compiler_params=_cparams(("arbitrary", "arbitrary")),
        name="ssd_rev" if rev else "ssd_fwd",
    )(xs, bc, sm, h0, dtb_row, a_row, dtb_t, a_t, e_mat, bmask)


def _gla_kernel(q_ref, k_ref, v_ref, sm_ref, s0_ref, gup_ref, gb_ref, bm_ref,
                o_ref, sfin_ref, state, *, rev, nc):
    c = pl.program_id(1)
    q = q_ref.shape[0]

    @pl.when(c == 0)
    def _():
        state[...] = s0_ref[...]

    gp = _dot(sm_ref[...].astype(bf16), gup_ref[...]) + gb_ref[...]
    g = jax.nn.log_sigmoid(gp) * (1.0 / GLA_GATE_NORM)
    lower, upper = _tri_masks(q)
    if not rev:
        b = _dot(jnp.where(lower, 1.0, 0.0).astype(f32), g, _HI)
        mask, mid = lower, q // 2 - 1
        b_tot = b[q - 1:q, :]
    else:
        b = _dot(jnp.where(upper, 1.0, 0.0).astype(f32), g, _HI)
        mask, mid = upper, q // 2
        b_tot = b[0:1, :]
    b_mid = b[mid:mid + 1, :]

    qf = q_ref[...].astype(f32) * (GLA_KEY_DIM ** -0.5)
    kf = k_ref[...].astype(f32)
    qd = (qf * jnp.exp(b - b_mid)).astype(bf16)
    ki = (kf * jnp.exp(b_mid - b)).astype(bf16)
    q_st = (qf * jnp.exp(b)).astype(bf16)
    k_end = (kf * jnp.exp(b_tot - b)).astype(bf16)
    vb = v_ref[...]
    v_t = vb.astype(f32).T.astype(bf16)

    s_old = state[...]
    o_inter = lax.dot_general(q_st, s_old.astype(bf16), _NT, preferred_element_type=f32)
    s_new = (s_old * jnp.exp(b_tot) + _dot(v_t, k_end)) * bm_ref[...]
    state[...] = s_new

    @pl.when(c == nc - 1)
    def _():
        sfin_ref[...] = s_new

    lane = lax.broadcasted_iota(i32, (q, LANES), 1)
    zero_b = jnp.zeros((q, LANES), bf16)
    outs = []
    for h in range(GLA_HEADS):
        p, hh = divmod(h, 2)
        qp = qd[:, p * LANES:(p + 1) * LANES]
        kp = ki[:, p * LANES:(p + 1) * LANES]
        in_h = (lane >= GLA_KEY_DIM * hh) & (lane < GLA_KEY_DIM * (hh + 1))
        s = lax.dot_general(jnp.where(in_h, qp, zero_b), kp, _NT, preferred_element_type=f32)
        attn = jnp.where(mask, s, 0.0).astype(bf16)
        outs.append(_dot(attn, vb[:, h * GLA_VAL_DIM:(h + 1) * GLA_VAL_DIM]))
    o_ref[...] = (jnp.concatenate(outs, axis=1) + o_inter).astype(bf16)


def _gla_scan(qa, ka, va, sm, s0, gup_pad, gbias, bmask, *, nb, rev):
    t = qa.shape[0]
    nc = t // (nb * CHUNK)
    if rev:
        tmap = lambda b, c: (b * nc + nc - 1 - c, 0)
    else:
        tmap = lambda b, c: (b * nc + c, 0)
    tok = lambda w: pl.BlockSpec((CHUNK, w), tmap)
    const = lambda a: pl.BlockSpec(a.shape, lambda b, c: (0,) * a.ndim)
    st = pl.BlockSpec((None, GLA_WIDTH, GLA_QK), lambda b, c: (b, 0, 0))
    return pl.pallas_call(
        functools.partial(_gla_kernel, rev=rev, nc=nc),
        out_shape=(jax.ShapeDtypeStruct((t, GLA_WIDTH), bf16),
                   jax.ShapeDtypeStruct((nb, GLA_WIDTH, GLA_QK), f32)),
        grid=(nb, nc),
        in_specs=[tok(GLA_QK), tok(GLA_QK), tok(GLA_WIDTH), tok(SMALL_COLS), st,
                  const(gup_pad), const(gbias), const(bmask)],
        out_specs=(tok(GLA_WIDTH), st),
        scratch_shapes=[pltpu.VMEM((GLA_WIDTH, GLA_QK), f32)],
        compiler_params=_cparams(("arbitrary", "arbitrary")),
        name="gla_rev" if rev else "gla_fwd",
    )(qa, ka, va, sm, s0, gup_pad, gbias, bmask)


def _outproj_kernel(yf_ref, yb_ref, xs_ref, z_ref, of_ref, ob_ref, go_ref, xres_ref, mod_ref,
                    dsk_ref, sg_ref, gg_ref, wout_ref, n2g_ref, wr_ref,
                    xnew_ref, h2_ref, lg_ref):
    up = lambda r: r[...].astype(f32)
    y = up(yf_ref) + up(yb_ref) + dsk_ref[...] * up(xs_ref)
    y = y * _silu(up(z_ref))
    y = y * lax.rsqrt(jnp.mean(y * y, axis=-1, keepdims=True) + NORM_EPS) * sg_ref[...]
    o = up(of_ref) + up(ob_ref)
    parts = []
    for h in range(GLA_HEADS):
        oh = o[:, h * GLA_VAL_DIM:(h + 1) * GLA_VAL_DIM]
        parts.append(oh * lax.rsqrt(jnp.mean(oh * oh, axis=-1, keepdims=True) + NORM_EPS))
    o = jnp.concatenate(parts, axis=1) * gg_ref[...] * _silu(up(go_ref))
    cat = jnp.concatenate([y, o], axis=1).astype(bf16)
    mix = _dot(cat, wout_ref[...])
    m = mod_ref[...]
    xn = xres_ref[...] + m[2:3] * mix
    xnew_ref[...] = xn
    inv = lax.rsqrt(jnp.mean(xn * xn, axis=-1, keepdims=True) + NORM_EPS)
    h2 = (xn * inv) * (n2g_ref[...] * (1.0 + m[4:5])) + m[3:4]
    h2_ref[...] = h2
    lg_ref[...] = _dot(h2, wr_ref[...], _HI)


def _outproj(yf, yb, xs, z, of, ob, go, xres, mod_l, dsk, sg, gg, wout, n2g, wr,
             *, tm, tiles_per_row, fixed_row):
    t = xres.shape[0]
    if fixed_row is None:
        mod_map = lambda i: (i // tiles_per_row, 0, 0)
    else:
        mod_map = lambda i: (fixed_row, 0, 0)
    tok = lambda w: pl.BlockSpec((tm, w), lambda i: (i, 0))
    const = lambda a: pl.BlockSpec(a.shape, lambda i: (0,) * a.ndim)
    return pl.pallas_call(
        _outproj_kernel,
        out_shape=(jax.ShapeDtypeStruct((t, D_MODEL), f32), jax.ShapeDtypeStruct((t, D_MODEL), f32),
                   jax.ShapeDtypeStruct((t, LANES), f32)),
        grid=(t // tm,),
        in_specs=[tok(512)] * 7 + [tok(D_MODEL), pl.BlockSpec((None, 6, D_MODEL), mod_map),
                                   const(dsk), const(sg), const(gg), const(wout), const(n2g), const(wr)],
        out_specs=(tok(D_MODEL), tok(D_MODEL), tok(LANES)),
        compiler_params=_cparams(("arbitrary",)),
        name="outproj",
    )(yf, yb, xs, z, of, ob, go, xres, mod_l, dsk, sg, gg, wout, n2g, wr)


def _route_kernel(lg_ref, bias_ref, su_ref, ids_ref, rank_ref, gcol_ref, cnt_ref, carry):
    i = pl.program_id(0)
    tm = lg_ref.shape[0]

    @pl.when(i == 0)
    def _():
        carry[...] = jnp.zeros_like(carry)

    s = jax.nn.sigmoid(lg_ref[...].T[0:N_EXPERTS, :])
    sel = s + bias_ref[...]
    a = [sel[8 * m:8 * (m + 1)] for m in range(EXPERTS_PER_GROUP)]
    sv = [s[8 * m:8 * (m + 1)] for m in range(EXPERTS_PER_GROUP)]
    hi01, lo01 = jnp.maximum(a[0], a[1]), jnp.minimum(a[0], a[1])
    hi23, lo23 = jnp.maximum(a[2], a[3]), jnp.minimum(a[2], a[3])
    gscore = jnp.maximum(hi01, hi23) + jnp.maximum(jnp.minimum(hi01, hi23), jnp.maximum(lo01, lo23))
    giota = lax.broadcasted_iota(i32, gscore.shape, 0)
    gmax = jnp.max(gscore, axis=0, keepdims=True)
    gidx = jnp.min(jnp.where(gscore == gmax, giota, N_EXPERT_GROUPS), axis=0, keepdims=True)
    pick = giota == gidx
    v = [jnp.sum(jnp.where(pick, a[m], 0.0), axis=0, keepdims=True) for m in range(4)]
    w = [jnp.sum(jnp.where(pick, sv[m], 0.0), axis=0, keepdims=True) for m in range(4)]

    def first_max(vals, excluded):
        best = vals[0]
        for m in range(1, 4):
            best = jnp.maximum(best, vals[m])
        idx = jnp.full(best.shape, 3, i32)
        for m in (2, 1, 0):
            hit = vals[m] == best
            if excluded is not None:
                hit = hit & (excluded != m)
            idx = jnp.where(hit, m, idx)
        return idx

    i1 = first_max(v, None)
    v_rest = [jnp.where(i1 == m, -jnp.inf, v[m]) for m in range(4)]
    i2 = first_max(v_rest, i1)

    def take(vals, idx):
        out = vals[3]
        for m in (2, 1, 0):
            out = jnp.where(idx == m, vals[m], out)
        return out

    w1, w2 = take(w, i1), take(w, i2)
    denom = w1 + w2
    id1 = gidx * EXPERTS_PER_GROUP + i1
    id2 = gidx * EXPERTS_PER_GROUP + i2
    row2 = lax.broadcasted_iota(i32, (TOP_K, tm), 0)
    ids_ref[...] = jnp.where(row2 == 0, id1, id2)

    eiota = lax.broadcasted_iota(i32, (N_EXPERTS, tm), 0)
    hit1, hit2 = eiota == id1, eiota == id2
    onehot = jnp.where(hit1, 1.0, 0.0) + jnp.where(hit2, 1.0, 0.0)
    before = _dot(onehot.astype(bf16), su_ref[...]) + carry[...]
    r1 = jnp.sum(jnp.where(hit1, before, 0.0), axis=0, keepdims=True)
    r2 = jnp.sum(jnp.where(hit2, before, 0.0), axis=0, keepdims=True)
    rank_ref[...] = jnp.where(row2 == 0, r1, r2).astype(i32)
    new_carry = carry[...] + jnp.sum(onehot, axis=1, keepdims=True)
    carry[...] = new_carry
    cnt_ref[...] = jnp.broadcast_to(new_carry, cnt_ref.shape)

    rows = lax.broadcasted_iota(i32, (LANES, tm), 0)
    gates = jnp.where(rows == 0, w1 / denom, jnp.where(rows == 1, w2 / denom, 0.0))
    gcol_ref[...] = gates.T


def _route(logits, bias_col, su):
    t = logits.shape[0]
    tm = ROUTE_TILE
    return pl.pallas_call(
        _route_kernel,
        out_shape=(jax.ShapeDtypeStruct((2, t), i32), jax.ShapeDtypeStruct((2, t), i32),
                   jax.ShapeDtypeStruct((t, LANES), f32), jax.ShapeDtypeStruct((N_EXPERTS, LANES), f32)),
        grid=(t // tm,),
        in_specs=[pl.BlockSpec((tm, LANES), lambda i: (i, 0)),
                  pl.BlockSpec((N_EXPERTS, 1), lambda i: (0, 0)),
                  pl.BlockSpec((tm, tm), lambda i: (0, 0))],
        out_specs=(pl.BlockSpec((2, tm), lambda i: (0, i)), pl.BlockSpec((2, tm), lambda i: (0, i)),
                   pl.BlockSpec((tm, LANES), lambda i: (i, 0)),
                   pl.BlockSpec((N_EXPERTS, LANES), lambda i: (0, 0))),
        scratch_shapes=[pltpu.VMEM((N_EXPERTS, 1), f32)],
        compiler_params=_cparams(("arbitrary",)),
        name="route",
    )(logits, bias_col, su)


def _row_copy(src_ref, src_row, dst_ref, dst_row, sem):
    return pltpu.make_async_copy(src_ref.at[pl.ds(src_row, 1)], dst_ref.at[pl.ds(dst_row, 1)], sem)


def _dest_kernel(ids_ref, rank_ref, pstart_ref, dest_ref):
    ids = ids_ref[...]
    tm = ids.shape[1]
    eiota = lax.broadcasted_iota(i32, (N_EXPERTS, tm), 0)
    ps = pstart_ref[...]
    rows = [jnp.sum(jnp.where(eiota == ids[k:k + 1, :], ps, 0.0), axis=0, keepdims=True) for k in range(TOP_K)]
    row2 = lax.broadcasted_iota(i32, (TOP_K, tm), 0)
    dest_ref[...] = jnp.where(row2 == 0, rows[0], rows[1]).astype(i32) + rank_ref[...]


def _dest(ids, rank, pstart_col):
    t = ids.shape[1]
    tm = MOE_TILE
    return pl.pallas_call(
        _dest_kernel,
        out_shape=jax.ShapeDtypeStruct((t // tm, TOP_K, tm), i32),
        grid=(t // tm,),
        in_specs=[pl.BlockSpec((TOP_K, tm), lambda i: (0, i)), pl.BlockSpec((TOP_K, tm), lambda i: (0, i)),
                  pl.BlockSpec((N_EXPERTS, 1), lambda i: (0, 0))],
        out_specs=pl.BlockSpec((None, TOP_K, tm), lambda i: (i, 0, 0)),
        compiler_params=_cparams(("arbitrary",)),
        name="dest",
    )(ids, rank, pstart_col)


def _dispatch_kernel(dest_ref, h_ref, buf_in_ref, buf_ref, sem):
    del buf_in_ref
    tm = dest_ref.shape[1]

    def issue(j, carry):
        _row_copy(h_ref, j, buf_ref, dest_ref[0, j], sem).start()
        _row_copy(h_ref, j, buf_ref, dest_ref[1, j], sem).start()
        return carry

    lax.fori_loop(0, tm, issue, 0)

    def drain(j, carry):
        _row_copy(h_ref, 0, buf_ref, 0, sem).wait()
        _row_copy(h_ref, 0, buf_ref, 0, sem).wait()
        return carry

    lax.fori_loop(0, tm, drain, 0)


def _dispatch(dest_tiles, tile_off, h2, buf):
    tm = dest_tiles.shape[2]
    return pl.pallas_call(
        _dispatch_kernel,
        out_shape=jax.ShapeDtypeStruct(buf.shape, buf.dtype),
        grid=(h2.shape[0] // tm,),
        in_specs=[pl.BlockSpec((None, TOP_K, tm), lambda i: (i + tile_off, 0, 0), memory_space=pltpu.SMEM),
                  pl.BlockSpec((tm, D_MODEL), lambda i: (i, 0)), pl.BlockSpec(memory_space=pl.ANY)],
        out_specs=pl.BlockSpec(memory_space=pl.ANY),
        scratch_shapes=[pltpu.SemaphoreType.DMA],
        input_output_aliases={2: 0},
        compiler_params=_cparams(("arbitrary",)),
        name="dispatch",
    )(dest_tiles, h2, buf)


def _expert_kernel(be_ref, nu_ref, x_ref, wg_ref, wu_ref, wd_ref, y_ref, wg_b, wu_b, wd_b):
    i = pl.program_id(0)
    fresh = jnp.logical_or(i == 0, be_ref[i] != be_ref[jnp.maximum(i - 1, 0)])

    @pl.when(jnp.logical_and(fresh, i < nu_ref[0]))
    def _():
        wg_b[...] = wg_ref[...].astype(bf16)
        wu_b[...] = wu_ref[...].astype(bf16)
        wd_b[...] = wd_ref[...].astype(bf16)

    @pl.when(i < nu_ref[0])
    def _():
        xb = x_ref[...].astype(bf16)
        g = _dot(xb, wg_b[...])
        u = _dot(xb, wu_b[...])
        y_ref[...] = _dot((_silu(g) * u).astype(bf16), wd_b[...])

    @pl.when(i >= nu_ref[0])
    def _():
        y_ref[...] = jnp.zeros_like(y_ref)


def _experts(block_e, n_used, buf, w_gate, w_up, w_down):
    p, d = buf.shape
    nb = p // EXPERT_BLOCK
    ff = w_gate.shape[-1]
    return pl.pallas_call(
        _expert_kernel,
        out_shape=jax.ShapeDtypeStruct((p, d), f32),
        grid_spec=pltpu.PrefetchScalarGridSpec(
            num_scalar_prefetch=2,
            grid=(nb,),
            in_specs=[pl.BlockSpec((EXPERT_BLOCK, d), lambda i, be, nu: (i, 0)),
                      pl.BlockSpec((None, d, ff), lambda i, be, nu: (be[i], 0, 0)),
                      pl.BlockSpec((None, d, ff), lambda i, be, nu: (be[i], 0, 0)),
                      pl.BlockSpec((None, ff, d), lambda i, be, nu: (be[i], 0, 0))],
            out_specs=pl.BlockSpec((EXPERT_BLOCK, d), lambda i, be, nu: (i, 0)),
            scratch_shapes=[pltpu.VMEM((d, ff), bf16), pltpu.VMEM((d, ff), bf16), pltpu.VMEM((ff, d), bf16)],
        ),
        compiler_params=_cparams(("arbitrary",)),
        name="experts",
    )(block_e, n_used, buf, w_gate, w_up, w_down)


def _combine_kernel(dest_ref, y_ref, gcol_ref, x_ref, mod_ref, fg_ref, o_ref, ybuf, sem, *, final):
    tm = dest_ref.shape[1]

    def issue(j, carry):
        _row_copy(y_ref, dest_ref[0, j], ybuf.at[0], j, sem).start()
        _row_copy(y_ref, dest_ref[1, j], ybuf.at[1], j, sem).start()
        return carry

    lax.fori_loop(0, tm, issue, 0)

    def drain(j, carry):
        _row_copy(y_ref, 0, ybuf.at[0], 0, sem).wait()
        _row_copy(y_ref, 0, ybuf.at[1], 0, sem).wait()
        return carry

    lax.fori_loop(0, tm, drain, 0)

    gc = gcol_ref[...]
    ffn = gc[:, 0:1] * ybuf[0] + gc[:, 1:2] * ybuf[1]
    x = x_ref[...] + mod_ref[...][5:6] * ffn
    if final:
        x = x * lax.rsqrt(jnp.mean(x * x, axis=-1, keepdims=True) + NORM_EPS) * fg_ref[...]
    o_ref[...] = x


def _combine(dest_tiles, tile_off, y, gcol, xnew, mod_l, final_g, *, tiles_per_row, fixed_row, final):
    tm = dest_tiles.shape[2]
    t = xnew.shape[0]
    if fixed_row is None:
        mod_map = lambda i: (i // tiles_per_row, 0, 0)
    else:
        mod_map = lambda i: (fixed_row, 0, 0)
    return pl.pallas_call(
        functools.partial(_combine_kernel, final=final),
        out_shape=jax.ShapeDtypeStruct((t, D_MODEL), f32),
        grid=(t // tm,),
        in_specs=[pl.BlockSpec((None, TOP_K, tm), lambda i: (i + tile_off, 0, 0), memory_space=pltpu.SMEM),
                  pl.BlockSpec(memory_space=pl.ANY),
                  pl.BlockSpec((tm, LANES), lambda i: (i + tile_off, 0)),
                  pl.BlockSpec((tm, D_MODEL), lambda i: (i, 0)),
                  pl.BlockSpec((None, 6, D_MODEL), mod_map),
                  pl.BlockSpec((1, D_MODEL), lambda i: (0, 0))],
        out_specs=pl.BlockSpec((tm, D_MODEL), lambda i: (i, 0)),
        scratch_shapes=[pltpu.VMEM((2, tm, D_MODEL), f32), pltpu.SemaphoreType.DMA],
        compiler_params=_cparams(("arbitrary",)),
        name="combine",
    )(dest_tiles, y, gcol, xnew, mod_l, final_g)


def _moe(h2_parts, logits, router_bias_col, su, w_gate, w_up, w_down):
    t = logits.shape[0]
    ids, rank, gcol, cnt = _route(logits, router_bias_col, su)
    counts = cnt[:, 0].astype(i32)
    padded = (counts + EXPERT_BLOCK - 1) // EXPERT_BLOCK * EXPERT_BLOCK
    pend = jnp.cumsum(padded)
    pstart = pend - padded
    nb = (t * TOP_K) // EXPERT_BLOCK + N_EXPERTS
    block_pos = jnp.arange(nb, dtype=i32) * EXPERT_BLOCK
    block_e = jnp.minimum(jnp.sum((pend[None, :] <= block_pos[:, None]).astype(i32), axis=1), N_EXPERTS - 1)
    n_used = (pend[-1:] // EXPERT_BLOCK).astype(i32)
    dest_tiles = _dest(ids, rank, pstart.astype(f32)[:, None])
    buf = jnp.zeros((nb * EXPERT_BLOCK, D_MODEL), f32)
    tile_off = 0
    for part in h2_parts:
        buf = _dispatch(dest_tiles, tile_off, part, buf)
        tile_off += part.shape[0] // MOE_TILE
    y = _experts(block_e, n_used, buf, w_gate, w_up, w_down)
    return dest_tiles, gcol, y


def _pack_layer(l, w_in, conv_w, conv_b, dt_bias_f, dt_bias_b, a_log_f, a_log_b, d_skip, ssd_norm_g,
                gk_up_f, gk_bias_f, gk_up_b, gk_bias_b, gla_norm_g, w_out, norm1_g, norm2_g):
    w = w_in[l]
    z, xbc, dt, q, k, v, go, gk = jnp.split(w, [512, 1280, 1288, 1544, 1800, 2312, 2824], axis=1)
    small = jnp.concatenate([dt, gk, jnp.zeros((D_MODEL, SMALL_COLS - 24), f32)], axis=1)
    w_packed = jnp.concatenate([z, xbc, q, k, v, go, small], axis=1).astype(bf16)

    def lane_row(vec):
        return jnp.zeros((1, LANES), f32).at[0, :SSD_HEADS].set(vec)

    def sub_col(vec):
        return jnp.broadcast_to(vec[:, None], (SSD_HEADS, CHUNK)).astype(f32)

    def gup(m):
        return jnp.zeros((SMALL_COLS, GLA_QK), f32).at[SSD_HEADS:SSD_HEADS + GLA_GATE_RANK].set(m).astype(bf16)

    a_f = -jnp.exp(a_log_f[l])
    a_b = -jnp.exp(a_log_b[l])
    return dict(
        w_packed=w_packed,
        conv_w8=jnp.zeros((8, SSD_XBC), f32).at[:5].set(conv_w[l]),
        conv_b=conv_b[l][None, :],
        ssd_f=(lane_row(dt_bias_f[l]), lane_row(a_f), sub_col(dt_bias_f[l]), sub_col(a_f)),
        ssd_b=(lane_row(dt_bias_b[l]), lane_row(a_b), sub_col(dt_bias_b[l]), sub_col(a_b)),
        gla_f=(gup(gk_up_f[l]), gk_bias_f[l][None, :]),
        gla_b=(gup(gk_up_b[l]), gk_bias_b[l][None, :]),
        dsk=jnp.repeat(d_skip[l], SSD_HEAD_DIM)[None, :],
        sg=ssd_norm_g[l][None, :],
        gg=jnp.tile(gla_norm_g[l], GLA_HEADS)[None, :],
        wout=w_out[l].astype(bf16),
        n1g=norm1_g[l][None, :],
        n2g=norm2_g[l][None, :],
    )


def _constants():
    r = jnp.arange(LANES)[:, None]
    c = jnp.arange(SSD_WIDTH)[None, :]
    e_mat = ((c // SSD_HEAD_DIM) == r).astype(bf16)
    ssd_mask = ((r // SSD_STATE) == (c // (SSD_WIDTH // SSD_GROUPS))).astype(f32)
    rr = jnp.arange(GLA_WIDTH)[:, None]
    cc = jnp.arange(GLA_QK)[None, :]
    gla_mask = ((rr // GLA_VAL_DIM) == (cc // GLA_KEY_DIM)).astype(f32)
    k = jnp.arange(ROUTE_TILE)
    su = (k[:, None] < k[None, :]).astype(bf16)
    return e_mat, ssd_mask, gla_mask, su


def _mixers(streams, pk, consts, nb):
    e_mat, ssd_mask, gla_mask, _ = consts
    out = {}
    h0_f = jnp.zeros((nb, LANES, SSD_WIDTH), f32)
    h0_b = h0_f
    s0_f = jnp.zeros((nb, GLA_WIDTH, GLA_QK), f32)
    s0_b = s0_f
    for name in ('ctx', 'lat'):
        z, xs, bc, q, k, v, go, sm = streams[name]
        yf, h0_f = _ssd_scan(xs, bc, sm, h0_f, *pk['ssd_f'], e_mat, ssd_mask, nb=nb, rev=False)
        yb, h0_b = _ssd_scan(xs, bc, sm, h0_b, *pk['ssd_b'], e_mat, ssd_mask, nb=nb, rev=True)
        of, s0_f = _gla_scan(q, k, v, sm, s0_f, *pk['gla_f'], gla_mask, nb=nb, rev=False)
        ob, s0_b = _gla_scan(q, k, v, sm, s0_b, *pk['gla_b'], gla_mask, nb=nb, rev=True)
        out[name] = (yf, yb, of, ob)
    return out


def kernel(x, c, ctx, c_ctx, w_mod, b_mod, norm1_g, norm2_g, w_in, conv_w, conv_b, dt_bias_f, dt_bias_b, a_log_f, a_log_b, d_skip, ssd_norm_g, gk_up_f, gk_bias_f, gk_up_b, gk_bias_b, gla_norm_g, w_out, w_router, router_bias, w_gate, w_up, w_down, final_norm_g):
    nb, seq, d = x.shape
    ctx_len = ctx.shape[1]
    depth = w_mod.shape[0]
    consts = _constants()
    su = consts[3]

    cvec = jnp.zeros((16, d), f32).at[:nb].set(c).at[nb].set(c_ctx)
    mod = _modulation(cvec, w_mod, b_mod).reshape(depth, 16, 6, d)
    ctx_row = nb

    perm = jnp.array([g * EXPERTS_PER_GROUP + m for m in range(EXPERTS_PER_GROUP)
                      for g in range(N_EXPERT_GROUPS)], dtype=i32)
    wr = jnp.zeros((d, LANES), f32).at[:, :N_EXPERTS].set(w_router[:, perm])
    bias_col = router_bias[perm][:, None]

    x2 = x.reshape(nb * seq, d)
    c2 = ctx.reshape(nb * ctx_len, d)
    lat_tiles = seq // TM_LAT

    for l in range(depth):
        last = l == depth - 1
        pk = _pack_layer(l, w_in, conv_w, conv_b, dt_bias_f, dt_bias_b, a_log_f, a_log_b, d_skip, ssd_norm_g,
                         gk_up_f, gk_bias_f, gk_up_b, gk_bias_b, gla_norm_g, w_out, norm1_g, norm2_g)
        mod_l = mod[l]
        streams = {
            'ctx': _inproj(c2, mod_l, pk['n1g'], pk['w_packed'], pk['conv_w8'], pk['conv_b'],
                           tm=ctx_len, rowlen=ctx_len, tiles_per_row=1, fixed_row=ctx_row),
            'lat': _inproj(x2, mod_l, pk['n1g'], pk['w_packed'], pk['conv_w8'], pk['conv_b'],
                           tm=TM_LAT, rowlen=GRID_W, tiles_per_row=lat_tiles, fixed_row=None),
        }
        mix = _mixers(streams, pk, consts, nb)

        def merge(name, xres, tm, tiles_per_row, fixed_row):
            z, xs, bc, q, k, v, go, sm = streams[name]
            yf, yb, of, ob = mix[name]
            return _outproj(yf, yb, xs, z, of, ob, go, xres, mod_l, pk['dsk'], pk['sg'], pk['gg'],
                            pk['wout'], pk['n2g'], wr, tm=tm, tiles_per_row=tiles_per_row, fixed_row=fixed_row)

        xn_lat, h2_lat, lg_lat = merge('lat', x2, TM_LAT, lat_tiles, None)
        if last:
            parts, logits = [h2_lat], lg_lat
        else:
            xn_ctx, h2_ctx, lg_ctx = merge('ctx', c2, ctx_len, 1, ctx_row)
            parts, logits = [h2_ctx, h2_lat], jnp.concatenate([lg_ctx, lg_lat], axis=0)
        dest_tiles, gcol, y = _moe(parts, logits, bias_col, su, w_gate[l], w_up[l], w_down[l])
        lat_off = 0
        if not last:
            c2 = _combine(dest_tiles, 0, y, gcol, xn_ctx, mod_l, final_norm_g[None, :],
                          tiles_per_row=1, fixed_row=ctx_row, final=False)
            lat_off = xn_ctx.shape[0] // MOE_TILE
        x2 = _combine(dest_tiles, lat_off, y, gcol, xn_lat, mod_l, final_norm_g[None, :],
                      tiles_per_row=seq // MOE_TILE, fixed_row=None, final=last)
    return x2.reshape(nb, seq, d)
```

```python
import functools

import jax
import jax.numpy as jnp
from jax import lax
from jax.experimental import pallas as pl
from jax.experimental.pallas import tpu as pltpu

f32 = jnp.float32
bf16 = jnp.bfloat16
i32 = jnp.int32

D_MODEL = 1024
SSD_HEADS = 8
SSD_HEAD_DIM = 64
SSD_WIDTH = 512
SSD_GROUPS = 2
SSD_STATE = 64
SSD_XBC = 768
GLA_HEADS = 4
GLA_KEY_DIM = 64
GLA_VAL_DIM = 128
GLA_QK = 256
GLA_WIDTH = 512
GLA_GATE_RANK = 16
GLA_GATE_NORM = 16.0
GRID_W = 64
N_EXPERTS = 32
N_EXPERT_GROUPS = 8
EXPERTS_PER_GROUP = 4
TOP_K = 2
EXPERT_FF = 512
NORM_EPS = 1e-6

LANES = 128
CHUNK = 128
CHUNKS_PER_STEP = 4
SSD_SPLIT = 3
GLA_SPLIT = 2
MAIN_COLS = 2816
SMALL_COLS = LANES
TM_LAT = 512
ROUTE_TILE = 512
MOE_TILE = 512
ROW_DMA_UNROLL = 8
EXPERT_BLOCK = 256
VMEM_LIMIT = 48 * 1024 * 1024

_HI = lax.Precision.HIGHEST
_NT = (((1,), (1,)), ((), ()))


def _dot(a, b, precision=None):
    return jnp.dot(a, b, preferred_element_type=f32, precision=precision)


def _silu(x):
    return x * jax.nn.sigmoid(x)


def _cparams(sem):
    return pltpu.CompilerParams(dimension_semantics=sem, vmem_limit_bytes=VMEM_LIMIT)


def _mod_kernel(c_ref, w_ref, b_ref, o_ref):
    sc = _silu(c_ref[...]).astype(bf16)
    o_ref[...] = _dot(sc, w_ref[...].astype(bf16)) + b_ref[...]


def _modulation(cvec, w_mod, b_mod):
    depth, d, n = w_mod.shape
    tn = 1536
    return pl.pallas_call(
        _mod_kernel,
        out_shape=jax.ShapeDtypeStruct((depth, 16, n), f32),
        grid=(depth, n // tn),
        in_specs=[pl.BlockSpec((16, d), lambda l, j: (0, 0)),
                  pl.BlockSpec((None, d, tn), lambda l, j: (l, 0, j)),
                  pl.BlockSpec((None, 1, tn), lambda l, j: (l, 0, j))],
        out_specs=pl.BlockSpec((None, 16, tn), lambda l, j: (l, 0, j)),
        compiler_params=_cparams(("arbitrary", "arbitrary")),
        name="modulation",
    )(cvec, w_mod, b_mod.reshape(depth, 1, n))


def _inproj_kernel(x_ref, mod_ref, g_ref, w_ref, cw_ref, cb_ref,
                   z_ref, xs_ref, bc_ref, q_ref, k_ref, v_ref, go_ref, sm_ref, *, rowlen):
    x = x_ref[...]
    tm = x.shape[0]
    inv = lax.rsqrt(jnp.mean(x * x, axis=-1, keepdims=True) + NORM_EPS)
    m = mod_ref[...]
    geff = g_ref[...] * (1.0 + m[1:2])
    h = ((x * inv) * geff + m[0:1]).astype(bf16)

    def proj(lo, hi):
        return _dot(h, w_ref[:, lo:hi])

    z_ref[...] = proj(0, 512).astype(bf16)

    xbc = proj(512, 1280)
    pos = lax.broadcasted_iota(i32, xbc.shape, 0) & (rowlen - 1)
    cw = cw_ref[...]
    acc = xbc * cw[2:3]
    for d in (-2, -1, 1, 2):
        shifted = pltpu.roll(xbc, (-d) % tm, 0)
        valid = (pos >= -d) if d < 0 else (pos <= rowlen - 1 - d)
        acc = acc + jnp.where(valid, shifted, 0.0) * cw[2 + d:3 + d]
    act = _silu(acc + cb_ref[...])
    xs_ref[...] = act[:, :SSD_WIDTH].astype(bf16)
    bc_ref[...] = act[:, SSD_WIDTH:].astype(bf16)

    qk = proj(1280, 1792)
    q_ref[...] = qk[:, :GLA_QK].astype(bf16)
    k_ref[...] = qk[:, GLA_QK:].astype(bf16)
    v_ref[...] = proj(1792, 2304).astype(bf16)
    go_ref[...] = proj(2304, 2816).astype(bf16)
    sm_ref[...] = proj(2816, 2944)


def _inproj(x2d, mod_l, norm_g, w_packed, conv_w8, conv_b, *, tm, rowlen, tiles_per_row, fixed_row):
    t = x2d.shape[0]
    nt = t // tm
    if fixed_row is None:
        mod_map = lambda i: (i // tiles_per_row, 0, 0)
    else:
        mod_map = lambda i: (fixed_row, 0, 0)
    tok = lambda w: pl.BlockSpec((tm, w), lambda i: (i, 0))
    const = lambda a: pl.BlockSpec(a.shape, lambda i: (0,) * a.ndim)
    widths = (512, 512, 256, 256, 256, 512, 512)
    return pl.pallas_call(
        functools.partial(_inproj_kernel, rowlen=rowlen),
        out_shape=tuple(jax.ShapeDtypeStruct((t, w), bf16) for w in widths)
        + (jax.ShapeDtypeStruct((t, SMALL_COLS), f32),),
        grid=(nt,),
        in_specs=[tok(D_MODEL), pl.BlockSpec((None, 6, D_MODEL), mod_map), const(norm_g),
                  const(w_packed), const(conv_w8), const(conv_b)],
        out_specs=tuple(tok(w) for w in widths) + (tok(SMALL_COLS),),
        compiler_params=_cparams(("arbitrary",)),
        name="inproj",
    )(x2d, mod_l, norm_g, w_packed, conv_w8, conv_b)


def _dot_split(a, b, parts, *, split_lhs):
    rest = a if split_lhs else b
    acc = None
    for _ in range(parts):
        piece = rest.astype(bf16)
        rest = rest - piece.astype(f32)
        term = _dot(piece, b) if split_lhs else _dot(a, piece)
        acc = term if acc is None else acc + term
    return acc


def _tri_masks(q):
    r = lax.broadcasted_iota(i32, (q, q), 0)
    c = lax.broadcasted_iota(i32, (q, q), 1)
    return r >= c, r <= c


def _ssd_chunk(xb, bcv, sm, s_old, dtb, a_lane, dtbt, a_sub, e, bm, rev):
    q = xb.shape[0]
    dt_col = jax.nn.softplus(sm + dtb)
    a_col = dt_col * a_lane
    dt_row = jax.nn.softplus(sm.T[0:SSD_HEADS, :] + dtbt)
    a_row = dt_row * a_sub
    lower, upper = _tri_masks(q)
    lo_b = jnp.where(lower, 1.0, 0.0).astype(bf16)
    up_b = jnp.where(upper, 1.0, 0.0).astype(bf16)
    if not rev:
        cs_col = _dot_split(lo_b, a_col, SSD_SPLIT, split_lhs=False)
        cs_row = _dot_split(a_row, up_b, SSD_SPLIT, split_lhs=True)
        mask = lower
        a_tot = cs_col[q - 1:q, :]
    else:
        cs_col = _dot_split(up_b, a_col, SSD_SPLIT, split_lhs=False)
        cs_row = _dot_split(a_row, lo_b, SSD_SPLIT, split_lhs=True)
        mask = upper
        a_tot = cs_col[0:1, :]

    w_exp = _dot((jnp.exp(a_tot - cs_col) * dt_col).astype(bf16), e)
    ecs_exp = _dot(jnp.exp(cs_col).astype(bf16), e)
    dec_exp = _dot_split(jnp.broadcast_to(jnp.exp(a_tot), (8, LANES)), e, SSD_SPLIT, split_lhs=True)[0:1]

    xw = (xb.astype(f32) * w_exp).astype(bf16)
    b_all = bcv[:, 0:LANES]
    c_all = bcv[:, LANES:2 * LANES]
    b_t = b_all.astype(f32).T.astype(bf16)

    y_inter = _dot(c_all, s_old.astype(bf16)) * ecs_exp
    s_new = (s_old * dec_exp + _dot(b_t, xw)) * bm

    lane = lax.broadcasted_iota(i32, (q, LANES), 1)
    zero_b = jnp.zeros((q, LANES), bf16)
    ys = []
    for g in range(SSD_GROUPS):
        in_g = (lane >= SSD_STATE * g) & (lane < SSD_STATE * (g + 1))
        cb = _dot(jnp.where(in_g, c_all, zero_b), b_t)
        for pp in range(2):
            h0 = 4 * g + 2 * pp
            ms = []
            for h in (h0, h0 + 1):
                seg = cs_col[:, h:h + 1] - cs_row[h:h + 1, :]
                dec = jnp.exp(jnp.where(mask, seg, -1e30))
                ms.append((cb * dec * dt_row[h:h + 1, :]).astype(bf16))
            xp = xb[:, h0 * SSD_HEAD_DIM:h0 * SSD_HEAD_DIM + LANES]
            rhs = jnp.concatenate([jnp.where(lane < SSD_HEAD_DIM, xp, zero_b),
                                   jnp.where(lane >= SSD_HEAD_DIM, xp, zero_b)], axis=0)
            ys.append(_dot(jnp.concatenate(ms, axis=1), rhs))
    return (jnp.concatenate(ys, axis=1) + y_inter).astype(bf16), s_new


def _scan_steps(cps, fwd_chunk, bwd_chunk, s_fwd, s_bwd):
    for ci in range(cps):
        s_fwd = fwd_chunk(pl.ds(ci * CHUNK, CHUNK), s_fwd)
        s_bwd = bwd_chunk(pl.ds((cps - 1 - ci) * CHUNK, CHUNK), s_bwd)
    return s_fwd, s_bwd


def _ssd_kernel(xf_ref, bcf_ref, smf_ref, xr_ref, bcr_ref, smr_ref, h0f_ref, h0r_ref,
                dtbf_ref, af_ref, dtbtf_ref, atf_ref, dtbr_ref, ar_ref, dtbtr_ref, atr_ref, e_ref, bm_ref,
                yf_ref, yr_ref, hff_ref, hfr_ref, st_f, st_r, *, cps, nblk):
    j = pl.program_id(1)

    @pl.when(j == 0)
    def _():
        st_f[...] = h0f_ref[...]
        st_r[...] = h0r_ref[...]

    e = e_ref[...]
    bm = bm_ref[...]
    par_f = (dtbf_ref[...], af_ref[...], dtbtf_ref[...], atf_ref[...])
    par_r = (dtbr_ref[...], ar_ref[...], dtbtr_ref[...], atr_ref[...])

    def fwd_chunk(rows, s):
        y, s = _ssd_chunk(xf_ref[rows, :], bcf_ref[rows, :], smf_ref[rows, :], s, *par_f, e, bm, False)
        yf_ref[rows, :] = y
        return s

    def bwd_chunk(rows, s):
        y, s = _ssd_chunk(xr_ref[rows, :], bcr_ref[rows, :], smr_ref[rows, :], s, *par_r, e, bm, True)
        yr_ref[rows, :] = y
        return s

    s_f, s_r = _scan_steps(cps, fwd_chunk, bwd_chunk, st_f[...], st_r[...])
    st_f[...] = s_f
    st_r[...] = s_r

    @pl.when(j == nblk - 1)
    def _():
        hff_ref[...] = s_f
        hfr_ref[...] = s_r


def _scan_specs(t, nb):
    per_row = t // nb
    cps = min(CHUNKS_PER_STEP, per_row // CHUNK)
    nblk = per_row // (cps * CHUNK)
    fmap = lambda b, j: (b * nblk + j, 0)
    rmap = lambda b, j: (b * nblk + nblk - 1 - j, 0)
    return cps, nblk, fmap, rmap


def _ssd_scan(xs, bc, sm, h0_f, h0_r, par_f, par_r, e_mat, bmask, *, nb):
    t = xs.shape[0]
    cps, nblk, fmap, rmap = _scan_specs(t, nb)
    blk = cps * CHUNK
    const = lambda a: pl.BlockSpec(a.shape, lambda b, j: (0,) * a.ndim)
    st = pl.BlockSpec((None, LANES, SSD_WIDTH), lambda b, j: (b, 0, 0))
    widths = (SSD_WIDTH, 2 * LANES, SMALL_COLS)
    tok_in = [pl.BlockSpec((blk, w), m) for m in (fmap, rmap) for w in widths]
    consts = (*par_f, *par_r, e_mat, bmask)
    return pl.pallas_call(
        functools.partial(_ssd_kernel, cps=cps, nblk=nblk),
        out_shape=(jax.ShapeDtypeStruct((t, SSD_WIDTH), bf16), jax.ShapeDtypeStruct((t, SSD_WIDTH), bf16),
                   jax.ShapeDtypeStruct((nb, LANES, SSD_WIDTH), f32),
                   jax.ShapeDtypeStruct((nb, LANES, SSD_WIDTH), f32)),
        grid=(nb, nblk),
        in_specs=tok_in + [st, st] + [const(a) for a in consts],
        out_specs=(pl.BlockSpec((blk, SSD_WIDTH), fmap), pl.BlockSpec((blk, SSD_WIDTH), rmap), st, st),
        scratch_shapes=[pltpu.VMEM((LANES, SSD_WIDTH), f32), pltpu.VMEM((LANES, SSD_WIDTH), f32)],
        compiler_params=_cparams(("arbitrary", "arbitrary")),
        name="ssd",
    )(xs, bc, sm, xs, bc, sm, h0_f, h0_r, *consts)


def _gla_chunk(qb, kb, vb, sm, s_old, gup, gbias, bm, rev):
    q = qb.shape[0]
    gp = _dot(sm.astype(bf16), gup) + gbias
    g = jax.nn.log_sigmoid(gp) * (1.0 / GLA_GATE_NORM)
    lower, upper = _tri_masks(q)
    if not rev:
        b = _dot_split(jnp.where(lower, 1.0, 0.0).astype(bf16), g, GLA_SPLIT, split_lhs=False)
        mask, mid = lower, q // 2 - 1
        b_tot = b[q - 1:q, :]
    else:
        b = _dot_split(jnp.where(upper, 1.0, 0.0).astype(bf16), g, GLA_SPLIT, split_lhs=False)
        mask, mid = upper, q // 2
        b_tot = b[0:1, :]
    b_mid = b[mid:mid + 1, :]

    qf = qb.astype(f32) * (GLA_KEY_DIM ** -0.5)
    kf = kb.astype(f32)
    qd = (qf * jnp.exp(b - b_mid)).astype(bf16)
    ki = (kf * jnp.exp(b_mid - b)).astype(bf16)
    q_st = (qf * jnp.exp(b)).astype(bf16)
    k_end = (kf * jnp.exp(b_tot - b)).astype(bf16)
    v_t = vb.astype(f32).T.astype(bf16)

    o_inter = lax.dot_general(q_st, s_old.astype(bf16), _NT, preferred_element_type=f32)
    s_new = (s_old * jnp.exp(b_tot) + _dot(v_t, k_end)) * bm

    lane = lax.broadcasted_iota(i32, (q, LANES), 1)
    zero_b = jnp.zeros((q, LANES), bf16)
    outs = []
    for h in range(GLA_HEADS):
        p, hh = divmod(h, 2)
        qp = qd[:, p * LANES:(p + 1) * LANES]
        kp = ki[:, p * LANES:(p + 1) * LANES]
        in_h = (lane >= GLA_KEY_DIM * hh) & (lane < GLA_KEY_DIM * (hh + 1))
        s = lax.dot_general(jnp.where(in_h, qp, zero_b), kp, _NT, preferred_element_type=f32)
        attn = jnp.where(mask, s, 0.0).astype(bf16)
        outs.append(_dot(attn, vb[:, h * GLA_VAL_DIM:(h + 1) * GLA_VAL_DIM]))
    return (jnp.concatenate(outs, axis=1) + o_inter).astype(bf16), s_new


def _gla_kernel(qf_ref, kf_ref, vf_ref, smf_ref, qr_ref, kr_ref, vr_ref, smr_ref, s0f_ref, s0r_ref,
                gupf_ref, gbf_ref, gupr_ref, gbr_ref, bm_ref,
                of_ref, or_ref, sff_ref, sfr_ref, st_f, st_r, *, cps, nblk):
    j = pl.program_id(1)

    @pl.when(j == 0)
    def _():
        st_f[...] = s0f_ref[...]
        st_r[...] = s0r_ref[...]

    bm = bm_ref[...]
    par_f = (gupf_ref[...], gbf_ref[...])
    par_r = (gupr_ref[...], gbr_ref[...])

    def fwd_chunk(rows, s):
        o, s = _gla_chunk(qf_ref[rows, :], kf_ref[rows, :], vf_ref[rows, :], smf_ref[rows, :], s, *par_f, bm, False)
        of_ref[rows, :] = o
        return s

    def bwd_chunk(rows, s):
        o, s = _gla_chunk(qr_ref[rows, :], kr_ref[rows, :], vr_ref[rows, :], smr_ref[rows, :], s, *par_r, bm, True)
        or_ref[rows, :] = o
        return s

    s_f, s_r = _scan_steps(cps, fwd_chunk, bwd_chunk, st_f[...], st_r[...])
    st_f[...] = s_f
    st_r[...] = s_r

    @pl.when(j == nblk - 1)
    def _():
        sff_ref[...] = s_f
        sfr_ref[...] = s_r


def _gla_scan(qa, ka, va, sm, s0_f, s0_r, par_f, par_r, bmask, *, nb):
    t = qa.shape[0]
    cps, nblk, fmap, rmap = _scan_specs(t, nb)
    blk = cps * CHUNK
    const = lambda a: pl.BlockSpec(a.shape, lambda b, j: (0,) * a.ndim)
    st = pl.BlockSpec((None, GLA_WIDTH, GLA_QK), lambda b, j: (b, 0, 0))
    widths = (GLA_QK, GLA_QK, GLA_WIDTH, SMALL_COLS)
    tok_in = [pl.BlockSpec((blk, w), m) for m in (fmap, rmap) for w in widths]
    consts = (*par_f, *par_r, bmask)
    return pl.pallas_call(
        functools.partial(_gla_kernel, cps=cps, nblk=nblk),
        out_shape=(jax.ShapeDtypeStruct((t, GLA_WIDTH), bf16), jax.ShapeDtypeStruct((t, GLA_WIDTH), bf16),
                   jax.ShapeDtypeStruct((nb, GLA_WIDTH, GLA_QK), f32),
                   jax.ShapeDtypeStruct((nb, GLA_WIDTH, GLA_QK), f32)),
        grid=(nb, nblk),
        in_specs=tok_in + [st, st] + [const(a) for a in consts],
        out_specs=(pl.BlockSpec((blk, GLA_WIDTH), fmap), pl.BlockSpec((blk, GLA_WIDTH), rmap), st, st),
        scratch_shapes=[pltpu.VMEM((GLA_WIDTH, GLA_QK), f32), pltpu.VMEM((GLA_WIDTH, GLA_QK), f32)],
        compiler_params=_cparams(("arbitrary", "arbitrary")),
        name="gla",
    )(qa, ka, va, sm, qa, ka, va, sm, s0_f, s0_r, *consts)


def _outproj_kernel(yf_ref, yb_ref, xs_ref, z_ref, of_ref, ob_ref, go_ref, xres_ref, mod_ref,
                    dsk_ref, sg_ref, gg_ref, wout_ref, n2g_ref, wr_ref,
                    xnew_ref, h2_ref, lg_ref):
    up = lambda r: r[...].astype(f32)
    y = up(yf_ref) + up(yb_ref) + dsk_ref[...] * up(xs_ref)
    y = y * _silu(up(z_ref))
    y = y * lax.rsqrt(jnp.mean(y * y, axis=-1, keepdims=True) + NORM_EPS) * sg_ref[...]
    o = up(of_ref) + up(ob_ref)
    parts = []
    for h in range(GLA_HEADS):
        oh = o[:, h * GLA_VAL_DIM:(h + 1) * GLA_VAL_DIM]
        parts.append(oh * lax.rsqrt(jnp.mean(oh * oh, axis=-1, keepdims=True) + NORM_EPS))
    o = jnp.concatenate(parts, axis=1) * gg_ref[...] * _silu(up(go_ref))
    cat = jnp.concatenate([y, o], axis=1).astype(bf16)
    mix = _dot(cat, wout_ref[...])
    m = mod_ref[...]
    xn = xres_ref[...] + m[2:3] * mix
    xnew_ref[...] = xn
    inv = lax.rsqrt(jnp.mean(xn * xn, axis=-1, keepdims=True) + NORM_EPS)
    h2 = (xn * inv) * (n2g_ref[...] * (1.0 + m[4:5])) + m[3:4]
    h2_ref[...] = h2
    lg_ref[...] = _dot(h2, wr_ref[...], _HI)


def _outproj(yf, yb, xs, z, of, ob, go, xres, mod_l, dsk, sg, gg, wout, n2g, wr,
             *, tm, tiles_per_row, fixed_row):
    t = xres.shape[0]
    if fixed_row is None:
        mod_map = lambda i: (i // tiles_per_row, 0, 0)
    else:
        mod_map = lambda i: (fixed_row, 0, 0)
    tok = lambda w: pl.BlockSpec((tm, w), lambda i: (i, 0))
    const = lambda a: pl.BlockSpec(a.shape, lambda i: (0,) * a.ndim)
    return pl.pallas_call(
        _outproj_kernel,
        out_shape=(jax.ShapeDtypeStruct((t, D_MODEL), f32), jax.ShapeDtypeStruct((t, D_MODEL), f32),
                   jax.ShapeDtypeStruct((t, LANES), f32)),
        grid=(t // tm,),
        in_specs=[tok(512)] * 7 + [tok(D_MODEL), pl.BlockSpec((None, 6, D_MODEL), mod_map),
                                   const(dsk), const(sg), const(gg), const(wout), const(n2g), const(wr)],
        out_specs=(tok(D_MODEL), tok(D_MODEL), tok(LANES)),
        compiler_params=_cparams(("arbitrary",)),
        name="outproj",
    )(yf, yb, xs, z, of, ob, go, xres, mod_l, dsk, sg, gg, wout, n2g, wr)


def _route_kernel(lg_ref, bias_ref, su_ref, ids_ref, rank_ref, gcol_ref, cnt_ref, carry):
    i = pl.program_id(0)
    tm = lg_ref.shape[0]

    @pl.when(i == 0)
    def _():
        carry[...] = jnp.zeros_like(carry)

    s = jax.nn.sigmoid(lg_ref[...].T[0:N_EXPERTS, :])
    sel = s + bias_ref[...]
    a = [sel[8 * m:8 * (m + 1)] for m in range(EXPERTS_PER_GROUP)]
    sv = [s[8 * m:8 * (m + 1)] for m in range(EXPERTS_PER_GROUP)]
    hi01, lo01 = jnp.maximum(a[0], a[1]), jnp.minimum(a[0], a[1])
    hi23, lo23 = jnp.maximum(a[2], a[3]), jnp.minimum(a[2], a[3])
    gscore = jnp.maximum(hi01, hi23) + jnp.maximum(jnp.minimum(hi01, hi23), jnp.maximum(lo01, lo23))
    giota = lax.broadcasted_iota(i32, gscore.shape, 0)
    gmax = jnp.max(gscore, axis=0, keepdims=True)
    gidx = jnp.min(jnp.where(gscore == gmax, giota, N_EXPERT_GROUPS), axis=0, keepdims=True)
    pick = giota == gidx
    v = [jnp.sum(jnp.where(pick, a[m], 0.0), axis=0, keepdims=True) for m in range(4)]
    w = [jnp.sum(jnp.where(pick, sv[m], 0.0), axis=0, keepdims=True) for m in range(4)]

    def first_max(vals, excluded):
        best = vals[0]
        for m in range(1, 4):
            best = jnp.maximum(best, vals[m])
        idx = jnp.full(best.shape, 3, i32)
        for m in (2, 1, 0):
            hit = vals[m] == best
            if excluded is not None:
                hit = hit & (excluded != m)
            idx = jnp.where(hit, m, idx)
        return idx

    i1 = first_max(v, None)
    v_rest = [jnp.where(i1 == m, -jnp.inf, v[m]) for m in range(4)]
    i2 = first_max(v_rest, i1)

    def take(vals, idx):
        out = vals[3]
        for m in (2, 1, 0):
            out = jnp.where(idx == m, vals[m], out)
        return out

    w1, w2 = take(w, i1), take(w, i2)
    denom = w1 + w2
    id1 = gidx * EXPERTS_PER_GROUP + i1
    id2 = gidx * EXPERTS_PER_GROUP + i2
    row2 = lax.broadcasted_iota(i32, (TOP_K, tm), 0)
    ids_ref[...] = jnp.where(row2 == 0, id1, id2)

    eiota = lax.broadcasted_iota(i32, (N_EXPERTS, tm), 0)
    hit1, hit2 = eiota == id1, eiota == id2
    onehot = jnp.where(hit1, 1.0, 0.0) + jnp.where(hit2, 1.0, 0.0)
    before = _dot(onehot.astype(bf16), su_ref[...]) + carry[...]
    r1 = jnp.sum(jnp.where(hit1, before, 0.0), axis=0, keepdims=True)
    r2 = jnp.sum(jnp.where(hit2, before, 0.0), axis=0, keepdims=True)
    rank_ref[...] = jnp.where(row2 == 0, r1, r2).astype(i32)
    new_carry = carry[...] + jnp.sum(onehot, axis=1, keepdims=True)
    carry[...] = new_carry
    cnt_ref[...] = jnp.broadcast_to(new_carry, cnt_ref.shape)

    rows = lax.broadcasted_iota(i32, (LANES, tm), 0)
    gates = jnp.where(rows == 0, w1 / denom, jnp.where(rows == 1, w2 / denom, 0.0))
    gcol_ref[...] = gates.T


def _route(logits, bias_col, su):
    t = logits.shape[0]
    tm = ROUTE_TILE
    return pl.pallas_call(
        _route_kernel,
        out_shape=(jax.ShapeDtypeStruct((2, t), i32), jax.ShapeDtypeStruct((2, t), i32),
                   jax.ShapeDtypeStruct((t, LANES), f32), jax.ShapeDtypeStruct((N_EXPERTS, LANES), f32)),
        grid=(t // tm,),
        in_specs=[pl.BlockSpec((tm, LANES), lambda i: (i, 0)),
                  pl.BlockSpec((N_EXPERTS, 1), lambda i: (0, 0)),
                  pl.BlockSpec((tm, tm), lambda i: (0, 0))],
        out_specs=(pl.BlockSpec((2, tm), lambda i: (0, i)), pl.BlockSpec((2, tm), lambda i: (0, i)),
                   pl.BlockSpec((tm, LANES), lambda i: (i, 0)),
                   pl.BlockSpec((N_EXPERTS, LANES), lambda i: (0, 0))),
        scratch_shapes=[pltpu.VMEM((N_EXPERTS, 1), f32)],
        compiler_params=_cparams(("arbitrary",)),
        name="route",
    )(logits, bias_col, su)


def _row_copy(src_ref, src_row, dst_ref, dst_row, sem):
    return pltpu.make_async_copy(src_ref.at[pl.ds(src_row, 1)], dst_ref.at[pl.ds(dst_row, 1)], sem)


def _dest_kernel(ids_ref, rank_ref, pstart_ref, dest_ref):
    ids = ids_ref[...]
    tm = ids.shape[1]
    eiota = lax.broadcasted_iota(i32, (N_EXPERTS, tm), 0)
    ps = pstart_ref[...]
    rows = [jnp.sum(jnp.where(eiota == ids[k:k + 1, :], ps, 0.0), axis=0, keepdims=True) for k in range(TOP_K)]
    row2 = lax.broadcasted_iota(i32, (TOP_K, tm), 0)
    dest_ref[...] = jnp.where(row2 == 0, rows[0], rows[1]).astype(i32) + rank_ref[...]


def _dest(ids, rank, pstart_col):
    t = ids.shape[1]
    tm = MOE_TILE
    return pl.pallas_call(
        _dest_kernel,
        out_shape=jax.ShapeDtypeStruct((t // tm, TOP_K, tm), i32),
        grid=(t // tm,),
        in_specs=[pl.BlockSpec((TOP_K, tm), lambda i: (0, i)), pl.BlockSpec((TOP_K, tm), lambda i: (0, i)),
                  pl.BlockSpec((N_EXPERTS, 1), lambda i: (0, 0))],
        out_specs=pl.BlockSpec((None, TOP_K, tm), lambda i: (i, 0, 0)),
        compiler_params=_cparams(("arbitrary",)),
        name="dest",
    )(ids, rank, pstart_col)


def _dispatch_kernel(dest_ref, h_ref, buf_in_ref, buf_ref, sem):
    del buf_in_ref
    tm = dest_ref.shape[1]

    def issue(j, carry):
        _row_copy(h_ref, j, buf_ref, dest_ref[0, j], sem).start()
        _row_copy(h_ref, j, buf_ref, dest_ref[1, j], sem).start()
        return carry

    lax.fori_loop(0, tm, issue, 0, unroll=ROW_DMA_UNROLL)

    def drain(j, carry):
        _row_copy(h_ref, 0, buf_ref, 0, sem).wait()
        _row_copy(h_ref, 0, buf_ref, 0, sem).wait()
        return carry

    lax.fori_loop(0, tm, drain, 0, unroll=ROW_DMA_UNROLL)


def _dispatch(dest_tiles, tile_off, h2, buf):
    tm = dest_tiles.shape[2]
    return pl.pallas_call(
        _dispatch_kernel,
        out_shape=jax.ShapeDtypeStruct(buf.shape, buf.dtype),
        grid=(h2.shape[0] // tm,),
        in_specs=[pl.BlockSpec((None, TOP_K, tm), lambda i: (i + tile_off, 0, 0), memory_space=pltpu.SMEM),
                  pl.BlockSpec((tm, D_MODEL), lambda i: (i, 0)), pl.BlockSpec(memory_space=pl.ANY)],
        out_specs=pl.BlockSpec(memory_space=pl.ANY),
        scratch_shapes=[pltpu.SemaphoreType.DMA],
        input_output_aliases={2: 0},
        compiler_params=_cparams(("arbitrary",)),
        name="dispatch",
    )(dest_tiles, h2, buf)


def _expert_kernel(be_ref, nu_ref, x_ref, wg_ref, wu_ref, wd_ref, y_ref, wg_b, wu_b, wd_b):
    i = pl.program_id(0)
    fresh = jnp.logical_or(i == 0, be_ref[i] != be_ref[jnp.maximum(i - 1, 0)])

    @pl.when(jnp.logical_and(fresh, i < nu_ref[0]))
    def _():
        wg_b[...] = wg_ref[...].astype(bf16)
        wu_b[...] = wu_ref[...].astype(bf16)
        wd_b[...] = wd_ref[...].astype(bf16)

    @pl.when(i < nu_ref[0])
    def _():
        xb = x_ref[...].astype(bf16)
        g = _dot(xb, wg_b[...])
        u = _dot(xb, wu_b[...])
        y_ref[...] = _dot((_silu(g) * u).astype(bf16), wd_b[...])

    @pl.when(i >= nu_ref[0])
    def _():
        y_ref[...] = jnp.zeros_like(y_ref)


def _experts(block_e, n_used, buf, w_gate, w_up, w_down, layer):
    p, d = buf.shape
    nb = p // EXPERT_BLOCK
    ff = w_gate.shape[-1]
    return pl.pallas_call(
        _expert_kernel,
        out_shape=jax.ShapeDtypeStruct((p, d), f32),
        grid_spec=pltpu.PrefetchScalarGridSpec(
            num_scalar_prefetch=2,
            grid=(nb,),
            in_specs=[pl.BlockSpec((EXPERT_BLOCK, d), lambda i, be, nu: (i, 0)),
                      pl.BlockSpec((None, None, d, ff), lambda i, be, nu: (layer, be[i], 0, 0)),
                      pl.BlockSpec((None, None, d, ff), lambda i, be, nu: (layer, be[i], 0, 0)),
                      pl.BlockSpec((None, None, ff, d), lambda i, be, nu: (layer, be[i], 0, 0))],
            out_specs=pl.BlockSpec((EXPERT_BLOCK, d), lambda i, be, nu: (i, 0)),
            scratch_shapes=[pltpu.VMEM((d, ff), bf16), pltpu.VMEM((d, ff), bf16), pltpu.VMEM((ff, d), bf16)],
        ),
        compiler_params=_cparams(("arbitrary",)),
        name="experts",
    )(block_e, n_used, buf, w_gate, w_up, w_down)


def _combine_kernel(dest_ref, y_ref, gcol_ref, x_ref, mod_ref, fg_ref, o_ref, ybuf, sem, *, final):
    tm = dest_ref.shape[1]

    def issue(j, carry):
        _row_copy(y_ref, dest_ref[0, j], ybuf.at[0], j, sem).start()
        _row_copy(y_ref, dest_ref[1, j], ybuf.at[1], j, sem).start()
        return carry

    lax.fori_loop(0, tm, issue, 0, unroll=ROW_DMA_UNROLL)

    def drain(j, carry):
        _row_copy(y_ref, 0, ybuf.at[0], 0, sem).wait()
        _row_copy(y_ref, 0, ybuf.at[1], 0, sem).wait()
        return carry

    lax.fori_loop(0, tm, drain, 0, unroll=ROW_DMA_UNROLL)

    gc = gcol_ref[...]
    ffn = gc[:, 0:1] * ybuf[0] + gc[:, 1:2] * ybuf[1]
    x = x_ref[...] + mod_ref[...][5:6] * ffn
    if final:
        x = x * lax.rsqrt(jnp.mean(x * x, axis=-1, keepdims=True) + NORM_EPS) * fg_ref[...]
    o_ref[...] = x


def _combine(dest_tiles, tile_off, y, gcol, xnew, mod_l, final_g, *, tiles_per_row, fixed_row, final):
    tm = dest_tiles.shape[2]
    t = xnew.shape[0]
    if fixed_row is None:
        mod_map = lambda i: (i // tiles_per_row, 0, 0)
    else:
        mod_map = lambda i: (fixed_row, 0, 0)
    return pl.pallas_call(
        functools.partial(_combine_kernel, final=final),
        out_shape=jax.ShapeDtypeStruct((t, D_MODEL), f32),
        grid=(t // tm,),
        in_specs=[pl.BlockSpec((None, TOP_K, tm), lambda i: (i + tile_off, 0, 0), memory_space=pltpu.SMEM),
                  pl.BlockSpec(memory_space=pl.ANY),
                  pl.BlockSpec((tm, LANES), lambda i: (i + tile_off, 0)),
                  pl.BlockSpec((tm, D_MODEL), lambda i: (i, 0)),
                  pl.BlockSpec((None, 6, D_MODEL), mod_map),
                  pl.BlockSpec((1, D_MODEL), lambda i: (0, 0))],
        out_specs=pl.BlockSpec((tm, D_MODEL), lambda i: (i, 0)),
        scratch_shapes=[pltpu.VMEM((2, tm, D_MODEL), f32), pltpu.SemaphoreType.DMA],
        compiler_params=_cparams(("arbitrary",)),
        name="combine",
    )(dest_tiles, y, gcol, xnew, mod_l, final_g)


def _moe(h2_parts, logits, router_bias_col, su, w_gate, w_up, w_down, layer):
    t = logits.shape[0]
    ids, rank, gcol, cnt = _route(logits, router_bias_col, su)
    counts = cnt[:, 0].astype(i32)
    padded = (counts + EXPERT_BLOCK - 1) // EXPERT_BLOCK * EXPERT_BLOCK
    pend = jnp.cumsum(padded)
    pstart = pend - padded
    nb = (t * TOP_K) // EXPERT_BLOCK + N_EXPERTS
    block_pos = jnp.arange(nb, dtype=i32) * EXPERT_BLOCK
    block_e = jnp.minimum(jnp.sum((pend[None, :] <= block_pos[:, None]).astype(i32), axis=1), N_EXPERTS - 1)
    n_used = (pend[-1:] // EXPERT_BLOCK).astype(i32)
    dest_tiles = _dest(ids, rank, pstart.astype(f32)[:, None])
    buf = jnp.zeros((nb * EXPERT_BLOCK, D_MODEL), f32)
    tile_off = 0
    for part in h2_parts:
        buf = _dispatch(dest_tiles, tile_off, part, buf)
        tile_off += part.shape[0] // MOE_TILE
    y = _experts(block_e, n_used, buf, w_gate, w_up, w_down, layer)
    return dest_tiles, gcol, y


def _pack_layer(l, w_in, conv_w, conv_b, dt_bias_f, dt_bias_b, a_log_f, a_log_b, d_skip, ssd_norm_g,
                gk_up_f, gk_bias_f, gk_up_b, gk_bias_b, gla_norm_g, w_out, norm1_g, norm2_g):
    w = w_in[l]
    z, xbc, dt, q, k, v, go, gk = jnp.split(w, [512, 1280, 1288, 1544, 1800, 2312, 2824], axis=1)
    small = jnp.concatenate([dt, gk, jnp.zeros((D_MODEL, SMALL_COLS - 24), f32)], axis=1)
    w_packed = jnp.concatenate([z, xbc, q, k, v, go, small], axis=1).astype(bf16)

    def lane_row(vec):
        return jnp.zeros((1, LANES), f32).at[0, :SSD_HEADS].set(vec)

    def sub_col(vec):
        return jnp.broadcast_to(vec[:, None], (SSD_HEADS, CHUNK)).astype(f32)

    def gup(m):
        return jnp.zeros((SMALL_COLS, GLA_QK), f32).at[SSD_HEADS:SSD_HEADS + GLA_GATE_RANK].set(m).astype(bf16)

    a_f = -jnp.exp(a_log_f[l])
    a_b = -jnp.exp(a_log_b[l])
    return dict(
        w_packed=w_packed,
        conv_w8=jnp.zeros((8, SSD_XBC), f32).at[:5].set(conv_w[l]),
        conv_b=conv_b[l][None, :],
        ssd_f=(lane_row(dt_bias_f[l]), lane_row(a_f), sub_col(dt_bias_f[l]), sub_col(a_f)),
        ssd_b=(lane_row(dt_bias_b[l]), lane_row(a_b), sub_col(dt_bias_b[l]), sub_col(a_b)),
        gla_f=(gup(gk_up_f[l]), gk_bias_f[l][None, :]),
        gla_b=(gup(gk_up_b[l]), gk_bias_b[l][None, :]),
        dsk=jnp.repeat(d_skip[l], SSD_HEAD_DIM)[None, :],
        sg=ssd_norm_g[l][None, :],
        gg=jnp.tile(gla_norm_g[l], GLA_HEADS)[None, :],
        wout=w_out[l].astype(bf16),
        n1g=norm1_g[l][None, :],
        n2g=norm2_g[l][None, :],
    )


def _constants():
    r = jnp.arange(LANES)[:, None]
    c = jnp.arange(SSD_WIDTH)[None, :]
    e_mat = ((c // SSD_HEAD_DIM) == r).astype(bf16)
    ssd_mask = ((r // SSD_STATE) == (c // (SSD_WIDTH // SSD_GROUPS))).astype(f32)
    rr = jnp.arange(GLA_WIDTH)[:, None]
    cc = jnp.arange(GLA_QK)[None, :]
    gla_mask = ((rr // GLA_VAL_DIM) == (cc // GLA_KEY_DIM)).astype(f32)
    k = jnp.arange(ROUTE_TILE)
    su = (k[:, None] < k[None, :]).astype(bf16)
    return e_mat, ssd_mask, gla_mask, su


def _mixers(streams, pk, consts, nb):
    e_mat, ssd_mask, gla_mask, _ = consts
    out = {}
    h0_f = jnp.zeros((nb, LANES, SSD_WIDTH), f32)
    h0_b = h0_f
    s0_f = jnp.zeros((nb, GLA_WIDTH, GLA_QK), f32)
    s0_b = s0_f
    for name in ('ctx', 'lat'):
        z, xs, bc, q, k, v, go, sm = streams[name]
        yf, yb, h0_f, h0_b = _ssd_scan(xs, bc, sm, h0_f, h0_b, pk['ssd_f'], pk['ssd_b'], e_mat, ssd_mask, nb=nb)
        of, ob, s0_f, s0_b = _gla_scan(q, k, v, sm, s0_f, s0_b, pk['gla_f'], pk['gla_b'], gla_mask, nb=nb)
        out[name] = (yf, yb, of, ob)
    return out


def kernel(x, c, ctx, c_ctx, w_mod, b_mod, norm1_g, norm2_g, w_in, conv_w, conv_b, dt_bias_f, dt_bias_b, a_log_f, a_log_b, d_skip, ssd_norm_g, gk_up_f, gk_bias_f, gk_up_b, gk_bias_b, gla_norm_g, w_out, w_router, router_bias, w_gate, w_up, w_down, final_norm_g):
    nb, seq, d = x.shape
    ctx_len = ctx.shape[1]
    depth = w_mod.shape[0]
    consts = _constants()
    su = consts[3]

    cvec = jnp.zeros((16, d), f32).at[:nb].set(c).at[nb].set(c_ctx)
    mod = _modulation(cvec, w_mod, b_mod).reshape(depth, 16, 6, d)
    ctx_row = nb

    perm = jnp.array([g * EXPERTS_PER_GROUP + m for m in range(EXPERTS_PER_GROUP)
                      for g in range(N_EXPERT_GROUPS)], dtype=i32)
    wr = jnp.zeros((d, LANES), f32).at[:, :N_EXPERTS].set(w_router[:, perm])
    bias_col = router_bias[perm][:, None]

    x2 = x.reshape(nb * seq, d)
    c2 = ctx.reshape(nb * ctx_len, d)
    lat_tiles = seq // TM_LAT

    for l in range(depth):
        last = l == depth - 1
        pk = _pack_layer(l, w_in, conv_w, conv_b, dt_bias_f, dt_bias_b, a_log_f, a_log_b, d_skip, ssd_norm_g,
                         gk_up_f, gk_bias_f, gk_up_b, gk_bias_b, gla_norm_g, w_out, norm1_g, norm2_g)
        mod_l = mod[l]
        streams = {
            'ctx': _inproj(c2, mod_l, pk['n1g'], pk['w_packed'], pk['conv_w8'], pk['conv_b'],
                           tm=ctx_len, rowlen=ctx_len, tiles_per_row=1, fixed_row=ctx_row),
            'lat': _inproj(x2, mod_l, pk['n1g'], pk['w_packed'], pk['conv_w8'], pk['conv_b'],
                           tm=TM_LAT, rowlen=GRID_W, tiles_per_row=lat_tiles, fixed_row=None),
        }
        mix = _mixers(streams, pk, consts, nb)

        def merge(name, xres, tm, tiles_per_row, fixed_row):
            z, xs, bc, q, k, v, go, sm = streams[name]
            yf, yb, of, ob = mix[name]
            return _outproj(yf, yb, xs, z, of, ob, go, xres, mod_l, pk['dsk'], pk['sg'], pk['gg'],
                            pk['wout'], pk['n2g'], wr, tm=tm, tiles_per_row=tiles_per_row, fixed_row=fixed_row)

        xn_lat, h2_lat, lg_lat = merge('lat', x2, TM_LAT, lat_tiles, None)
        if last:
            parts, logits = [h2_lat], lg_lat
        else:
            xn_ctx, h2_ctx, lg_ctx = merge('ctx', c2, ctx_len, 1, ctx_row)
            parts, logits = [h2_ctx, h2_lat], jnp.concatenate([lg_ctx, lg_lat], axis=0)
        dest_tiles, gcol, y = _moe(parts, logits, bias_col, su, w_gate, w_up, w_down, l)
        lat_off = 0
        if not last:
            c2 = _combine(dest_tiles, 0, y, gcol, xn_ctx, mod_l, final_norm_g[None, :],
                          tiles_per_row=1, fixed_row=ctx_row, final=False)
            lat_off = xn_ctx.shape[0] // MOE_TILE
        x2 = _combine(dest_tiles, lat_off, y, gcol, xn_lat, mod_l, final_norm_g[None, :],
                      tiles_per_row=seq // MOE_TILE, fixed_row=None, final=last)
    return x2.reshape(nb, seq, d)
```

```python
import functools

import jax
import jax.numpy as jnp
from jax import lax
from jax.experimental import pallas as pl
from jax.experimental.pallas import tpu as pltpu

f32 = jnp.float32
bf16 = jnp.bfloat16
i32 = jnp.int32

D_MODEL = 1024
SSD_HEADS = 8
SSD_HEAD_DIM = 64
SSD_WIDTH = 512
SSD_GROUPS = 2
SSD_STATE = 64
SSD_XBC = 768
GLA_HEADS = 4
GLA_KEY_DIM = 64
GLA_VAL_DIM = 128
GLA_QK = 256
GLA_WIDTH = 512
GLA_GATE_RANK = 16
GLA_GATE_NORM = 16.0
GRID_W = 64
N_EXPERTS = 32
N_EXPERT_GROUPS = 8
EXPERTS_PER_GROUP = 4
TOP_K = 2
EXPERT_FF = 512
NORM_EPS = 1e-6

LANES = 128
CHUNK = 128
CHUNKS_PER_STEP = 4
INTERLEAVE_WAVE = 8
SSD_SPLIT = 3
GLA_SPLIT = 2
MAIN_COLS = 2816
SMALL_COLS = LANES
TM_LAT = 512
ROUTE_TILE = 512
MOE_TILE = 512
ROW_DMA_UNROLL = 8
EXPERT_BLOCK = 512
VMEM_LIMIT = 48 * 1024 * 1024

_HI = lax.Precision.HIGHEST
_NT = (((1,), (1,)), ((), ()))


def _dot(a, b, precision=None):
    return jnp.dot(a, b, preferred_element_type=f32, precision=precision)


def _silu(x):
    return x * jax.nn.sigmoid(x)


def _cparams(sem):
    return pltpu.CompilerParams(dimension_semantics=sem, vmem_limit_bytes=VMEM_LIMIT)


def _mod_kernel(c_ref, w_ref, b_ref, o_ref):
    sc = _silu(c_ref[...]).astype(bf16)
    o_ref[...] = _dot(sc, w_ref[...].astype(bf16)) + b_ref[...]


def _modulation(cvec, w_mod, b_mod):
    depth, d, n = w_mod.shape
    tn = 1536
    return pl.pallas_call(
        _mod_kernel,
        out_shape=jax.ShapeDtypeStruct((depth, 16, n), f32),
        grid=(depth, n // tn),
        in_specs=[pl.BlockSpec((16, d), lambda l, j: (0, 0)),
                  pl.BlockSpec((None, d, tn), lambda l, j: (l, 0, j)),
                  pl.BlockSpec((None, 1, tn), lambda l, j: (l, 0, j))],
        out_specs=pl.BlockSpec((None, 16, tn), lambda l, j: (l, 0, j)),
        compiler_params=_cparams(("arbitrary", "arbitrary")),
        name="modulation",
    )(cvec, w_mod, b_mod.reshape(depth, 1, n))


def _inproj_kernel(x_ref, mod_ref, g_ref, w_ref, cw_ref, cb_ref,
                   z_ref, xs_ref, bc_ref, q_ref, k_ref, v_ref, go_ref, sm_ref, *, rowlen):
    x = x_ref[...]
    tm = x.shape[0]
    inv = lax.rsqrt(jnp.mean(x * x, axis=-1, keepdims=True) + NORM_EPS)
    m = mod_ref[...]
    geff = g_ref[...] * (1.0 + m[1:2])
    h = ((x * inv) * geff + m[0:1]).astype(bf16)

    def proj(lo, hi):
        return _dot(h, w_ref[:, lo:hi])

    z_ref[...] = proj(0, 512).astype(bf16)

    xbc = proj(512, 1280)
    pos = lax.broadcasted_iota(i32, xbc.shape, 0) & (rowlen - 1)
    cw = cw_ref[...]
    acc = xbc * cw[2:3]
    for d in (-2, -1, 1, 2):
        shifted = pltpu.roll(xbc, (-d) % tm, 0)
        valid = (pos >= -d) if d < 0 else (pos <= rowlen - 1 - d)
        acc = acc + jnp.where(valid, shifted, 0.0) * cw[2 + d:3 + d]
    act = _silu(acc + cb_ref[...])
    xs_ref[...] = act[:, :SSD_WIDTH].astype(bf16)
    bc_ref[...] = act[:, SSD_WIDTH:].astype(bf16)

    qk = proj(1280, 1792)
    q_ref[...] = qk[:, :GLA_QK].astype(bf16)
    k_ref[...] = qk[:, GLA_QK:].astype(bf16)
    v_ref[...] = proj(1792, 2304).astype(bf16)
    go_ref[...] = proj(2304, 2816).astype(bf16)
    sm_ref[...] = proj(2816, 2944)


def _inproj(x2d, mod_l, norm_g, w_packed, conv_w8, conv_b, *, tm, rowlen, tiles_per_row, fixed_row):
    t = x2d.shape[0]
    nt = t // tm
    if fixed_row is None:
        mod_map = lambda i: (i // tiles_per_row, 0, 0)
    else:
        mod_map = lambda i: (fixed_row, 0, 0)
    tok = lambda w: pl.BlockSpec((tm, w), lambda i: (i, 0))
    const = lambda a: pl.BlockSpec(a.shape, lambda i: (0,) * a.ndim)
    widths = (512, 512, 256, 256, 256, 512, 512)
    return pl.pallas_call(
        functools.partial(_inproj_kernel, rowlen=rowlen),
        out_shape=tuple(jax.ShapeDtypeStruct((t, w), bf16) for w in widths)
        + (jax.ShapeDtypeStruct((t, SMALL_COLS), f32),),
        grid=(nt,),
        in_specs=[tok(D_MODEL), pl.BlockSpec((None, 6, D_MODEL), mod_map), const(norm_g),
                  const(w_packed), const(conv_w8), const(conv_b)],
        out_specs=tuple(tok(w) for w in widths) + (tok(SMALL_COLS),),
        compiler_params=_cparams(("arbitrary",)),
        name="inproj",
    )(x2d, mod_l, norm_g, w_packed, conv_w8, conv_b)


def _dot_split(a, b, parts, *, split_lhs):
    rest = a if split_lhs else b
    acc = None
    for _ in range(parts):
        piece = rest.astype(bf16)
        rest = rest - piece.astype(f32)
        term = _dot(piece, b) if split_lhs else _dot(a, piece)
        acc = term if acc is None else acc + term
    return acc


def _tri_masks(q):
    r = lax.broadcasted_iota(i32, (q, q), 0)
    c = lax.broadcasted_iota(i32, (q, q), 1)
    return r >= c, r <= c


def _ssd_chunk(load, store, carry, par, e, bm, rev):
    xb, bcv, sm = load()
    dtb, a_lane, dtbt, a_sub = par
    q = xb.shape[0]
    dt_col = jax.nn.softplus(sm + dtb)
    a_col = dt_col * a_lane
    dt_row = jax.nn.softplus(sm.T[0:SSD_HEADS, :] + dtbt)
    a_row = dt_row * a_sub
    yield
    lower, upper = _tri_masks(q)
    lo_b = jnp.where(lower, 1.0, 0.0).astype(bf16)
    up_b = jnp.where(upper, 1.0, 0.0).astype(bf16)
    if not rev:
        cs_col = _dot_split(lo_b, a_col, SSD_SPLIT, split_lhs=False)
        cs_row = _dot_split(a_row, up_b, SSD_SPLIT, split_lhs=True)
        mask = lower
        a_tot = cs_col[q - 1:q, :]
    else:
        cs_col = _dot_split(up_b, a_col, SSD_SPLIT, split_lhs=False)
        cs_row = _dot_split(a_row, lo_b, SSD_SPLIT, split_lhs=True)
        mask = upper
        a_tot = cs_col[0:1, :]
    yield

    w_exp = _dot((jnp.exp(a_tot - cs_col) * dt_col).astype(bf16), e)
    ecs_exp = _dot(jnp.exp(cs_col).astype(bf16), e)
    dec_exp = _dot_split(jnp.broadcast_to(jnp.exp(a_tot), (8, LANES)), e, SSD_SPLIT, split_lhs=True)[0:1]
    yield

    xw = (xb.astype(f32) * w_exp).astype(bf16)
    b_all = bcv[:, 0:LANES]
    c_all = bcv[:, LANES:2 * LANES]
    b_t = b_all.astype(f32).T.astype(bf16)
    yield
    s_upd = _dot(b_t, xw)
    yield

    lane = lax.broadcasted_iota(i32, (q, LANES), 1)
    zero_b = jnp.zeros((q, LANES), bf16)
    ys = []
    for g in range(SSD_GROUPS):
        in_g = (lane >= SSD_STATE * g) & (lane < SSD_STATE * (g + 1))
        cb = _dot(jnp.where(in_g, c_all, zero_b), b_t)
        yield
        for pp in range(2):
            h0 = 4 * g + 2 * pp
            ms = []
            for h in (h0, h0 + 1):
                seg = cs_col[:, h:h + 1] - cs_row[h:h + 1, :]
                dec = jnp.exp(jnp.where(mask, seg, -1e30))
                ms.append((cb * dec * dt_row[h:h + 1, :]).astype(bf16))
            xp = xb[:, h0 * SSD_HEAD_DIM:h0 * SSD_HEAD_DIM + LANES]
            rhs = jnp.concatenate([jnp.where(lane < SSD_HEAD_DIM, xp, zero_b),
                                   jnp.where(lane >= SSD_HEAD_DIM, xp, zero_b)], axis=0)
            yield
            ys.append(_dot(jnp.concatenate(ms, axis=1), rhs))
            yield
    y_intra = jnp.concatenate(ys, axis=1)
    yield

    s_old = carry['s']
    carry['s'] = (s_old * dec_exp + s_upd) * bm
    store((y_intra + _dot(c_all, s_old.astype(bf16)) * ecs_exp).astype(bf16))


def _run_interleaved(gens):
    gens = list(gens)
    done = object()
    for w in range(0, len(gens), INTERLEAVE_WAVE):
        live = gens[w:w + INTERLEAVE_WAVE]
        while live:
            live = [g for g in live if next(g, done) is not done]


def _chunk_rows(cps):
    return [(pl.ds(ci * CHUNK, CHUNK), pl.ds((cps - 1 - ci) * CHUNK, CHUNK)) for ci in range(cps)]


def _ssd_kernel(xf_ref, bcf_ref, smf_ref, xr_ref, bcr_ref, smr_ref, h0f_ref, h0r_ref,
                dtbf_ref, af_ref, dtbtf_ref, atf_ref, dtbr_ref, ar_ref, dtbtr_ref, atr_ref, e_ref, bm_ref,
                yf_ref, yr_ref, hff_ref, hfr_ref, st_f, st_r, *, cps, nblk):
    j = pl.program_id(1)

    @pl.when(j == 0)
    def _():
        st_f[...] = h0f_ref[...]
        st_r[...] = h0r_ref[...]

    e = e_ref[...]
    bm = bm_ref[...]
    par_f = (dtbf_ref[...], af_ref[...], dtbtf_ref[...], atf_ref[...])
    par_r = (dtbr_ref[...], ar_ref[...], dtbtr_ref[...], atr_ref[...])
    carry_f = {'s': st_f[...]}
    carry_r = {'s': st_r[...]}

    def chunk(refs, out_ref, rows, carry, par, rev):
        def store(y):
            out_ref[rows, :] = y
        return _ssd_chunk(lambda: tuple(r[rows, :] for r in refs), store, carry, par, e, bm, rev)

    gens = []
    for rows_f, rows_r in _chunk_rows(cps):
        gens.append(chunk((xf_ref, bcf_ref, smf_ref), yf_ref, rows_f, carry_f, par_f, False))
        gens.append(chunk((xr_ref, bcr_ref, smr_ref), yr_ref, rows_r, carry_r, par_r, True))
    _run_interleaved(gens)
    st_f[...] = carry_f['s']
    st_r[...] = carry_r['s']

    @pl.when(j == nblk - 1)
    def _():
        hff_ref[...] = carry_f['s']
        hfr_ref[...] = carry_r['s']


def _scan_specs(t, nb):
    per_row = t // nb
    cps = min(CHUNKS_PER_STEP, per_row // CHUNK)
    nblk = per_row // (cps * CHUNK)
    fmap = lambda b, j: (b * nblk + j, 0)
    rmap = lambda b, j: (b * nblk + nblk - 1 - j, 0)
    return cps, nblk, fmap, rmap


def _ssd_scan(xs, bc, sm, h0_f, h0_r, par_f, par_r, e_mat, bmask, *, nb):
    t = xs.shape[0]
    cps, nblk, fmap, rmap = _scan_specs(t, nb)
    blk = cps * CHUNK
    const = lambda a: pl.BlockSpec(a.shape, lambda b, j: (0,) * a.ndim)
    st = pl.BlockSpec((None, LANES, SSD_WIDTH), lambda b, j: (b, 0, 0))
    widths = (SSD_WIDTH, 2 * LANES, SMALL_COLS)
    tok_in = [pl.BlockSpec((blk, w), m) for m in (fmap, rmap) for w in widths]
    consts = (*par_f, *par_r, e_mat, bmask)
    return pl.pallas_call(
        functools.partial(_ssd_kernel, cps=cps, nblk=nblk),
        out_shape=(jax.ShapeDtypeStruct((t, SSD_WIDTH), bf16), jax.ShapeDtypeStruct((t, SSD_WIDTH), bf16),
                   jax.ShapeDtypeStruct((nb, LANES, SSD_WIDTH), f32),
                   jax.ShapeDtypeStruct((nb, LANES, SSD_WIDTH), f32)),
        grid=(nb, nblk),
        in_specs=tok_in + [st, st] + [const(a) for a in consts],
        out_specs=(pl.BlockSpec((blk, SSD_WIDTH), fmap), pl.BlockSpec((blk, SSD_WIDTH), rmap), st, st),
        scratch_shapes=[pltpu.VMEM((LANES, SSD_WIDTH), f32), pltpu.VMEM((LANES, SSD_WIDTH), f32)],
        compiler_params=_cparams(("arbitrary", "arbitrary")),
        name="ssd",
    )(xs, bc, sm, xs, bc, sm, h0_f, h0_r, *consts)


def _gla_chunk(load, store, carry, par, bm, rev):
    qb, kb, vb, sm = load()
    gup, gbias = par
    q = qb.shape[0]
    gp = _dot(sm.astype(bf16), gup) + gbias
    yield
    g = jax.nn.log_sigmoid(gp) * (1.0 / GLA_GATE_NORM)
    yield
    lower, upper = _tri_masks(q)
    if not rev:
        b = _dot_split(jnp.where(lower, 1.0, 0.0).astype(bf16), g, GLA_SPLIT, split_lhs=False)
        mask, mid = lower, q // 2 - 1
        b_tot = b[q - 1:q, :]
    else:
        b = _dot_split(jnp.where(upper, 1.0, 0.0).astype(bf16), g, GLA_SPLIT, split_lhs=False)
        mask, mid = upper, q // 2
        b_tot = b[0:1, :]
    b_mid = b[mid:mid + 1, :]
    yield

    qf = qb.astype(f32) * (GLA_KEY_DIM ** -0.5)
    kf = kb.astype(f32)
    qd = (qf * jnp.exp(b - b_mid)).astype(bf16)
    yield
    ki = (kf * jnp.exp(b_mid - b)).astype(bf16)
    yield
    q_st = (qf * jnp.exp(b)).astype(bf16)
    yield
    k_end = (kf * jnp.exp(b_tot - b)).astype(bf16)
    dec = jnp.exp(b_tot)
    yield
    v_t = vb.astype(f32).T.astype(bf16)
    yield
    s_upd = _dot(v_t, k_end)
    yield

    lane = lax.broadcasted_iota(i32, (q, LANES), 1)
    zero_b = jnp.zeros((q, LANES), bf16)
    outs = []
    for h in range(GLA_HEADS):
        p, hh = divmod(h, 2)
        qp = qd[:, p * LANES:(p + 1) * LANES]
        kp = ki[:, p * LANES:(p + 1) * LANES]
        in_h = (lane >= GLA_KEY_DIM * hh) & (lane < GLA_KEY_DIM * (hh + 1))
        s = lax.dot_general(jnp.where(in_h, qp, zero_b), kp, _NT, preferred_element_type=f32)
        yield
        attn = jnp.where(mask, s, 0.0).astype(bf16)
        outs.append(_dot(attn, vb[:, h * GLA_VAL_DIM:(h + 1) * GLA_VAL_DIM]))
        yield
    o_intra = jnp.concatenate(outs, axis=1)
    yield

    s_old = carry['s']
    carry['s'] = (s_old * dec + s_upd) * bm
    o_inter = lax.dot_general(q_st, s_old.astype(bf16), _NT, preferred_element_type=f32)
    store((o_intra + o_inter).astype(bf16))


def _gla_kernel(qf_ref, kf_ref, vf_ref, smf_ref, qr_ref, kr_ref, vr_ref, smr_ref, s0f_ref, s0r_ref,
                gupf_ref, gbf_ref, gupr_ref, gbr_ref, bm_ref,
                of_ref, or_ref, sff_ref, sfr_ref, st_f, st_r, *, cps, nblk):
    j = pl.program_id(1)

    @pl.when(j == 0)
    def _():
        st_f[...] = s0f_ref[...]
        st_r[...] = s0r_ref[...]

    bm = bm_ref[...]
    par_f = (gupf_ref[...], gbf_ref[...])
    par_r = (gupr_ref[...], gbr_ref[...])
    carry_f = {'s': st_f[...]}
    carry_r = {'s': st_r[...]}

    def chunk(refs, out_ref, rows, carry, par, rev):
        def store(o):
            out_ref[rows, :] = o
        return _gla_chunk(lambda: tuple(r[rows, :] for r in refs), store, carry, par, bm, rev)

    gens = []
    for rows_f, rows_r in _chunk_rows(cps):
        gens.append(chunk((qf_ref, kf_ref, vf_ref, smf_ref), of_ref, rows_f, carry_f, par_f, False))
        gens.append(chunk((qr_ref, kr_ref, vr_ref, smr_ref), or_ref, rows_r, carry_r, par_r, True))
    _run_interleaved(gens)
    st_f[...] = carry_f['s']
    st_r[...] = carry_r['s']

    @pl.when(j == nblk - 1)
    def _():
        sff_ref[...] = carry_f['s']
        sfr_ref[...] = carry_r['s']


def _gla_scan(qa, ka, va, sm, s0_f, s0_r, par_f, par_r, bmask, *, nb):
    t = qa.shape[0]
    cps, nblk, fmap, rmap = _scan_specs(t, nb)
    blk = cps * CHUNK
    const = lambda a: pl.BlockSpec(a.shape, lambda b, j: (0,) * a.ndim)
    st = pl.BlockSpec((None, GLA_WIDTH, GLA_QK), lambda b, j: (b, 0, 0))
    widths = (GLA_QK, GLA_QK, GLA_WIDTH, SMALL_COLS)
    tok_in = [pl.BlockSpec((blk, w), m) for m in (fmap, rmap) for w in widths]
    consts = (*par_f, *par_r, bmask)
    return pl.pallas_call(
        functools.partial(_gla_kernel, cps=cps, nblk=nblk),
        out_shape=(jax.ShapeDtypeStruct((t, GLA_WIDTH), bf16), jax.ShapeDtypeStruct((t, GLA_WIDTH), bf16),
                   jax.ShapeDtypeStruct((nb, GLA_WIDTH, GLA_QK), f32),
                   jax.ShapeDtypeStruct((nb, GLA_WIDTH, GLA_QK), f32)),
        grid=(nb, nblk),
        in_specs=tok_in + [st, st] + [const(a) for a in consts],
        out_specs=(pl.BlockSpec((blk, GLA_WIDTH), fmap), pl.BlockSpec((blk, GLA_WIDTH), rmap), st, st),
        scratch_shapes=[pltpu.VMEM((GLA_WIDTH, GLA_QK), f32), pltpu.VMEM((GLA_WIDTH, GLA_QK), f32)],
        compiler_params=_cparams(("arbitrary", "arbitrary")),
        name="gla",
    )(qa, ka, va, sm, qa, ka, va, sm, s0_f, s0_r, *consts)


def _outproj_kernel(yf_ref, yb_ref, xs_ref, z_ref, of_ref, ob_ref, go_ref, xres_ref, mod_ref,
                    dsk_ref, sg_ref, gg_ref, wout_ref, n2g_ref, wr_ref,
                    xnew_ref, h2_ref, lg_ref):
    up = lambda r: r[...].astype(f32)
    y = up(yf_ref) + up(yb_ref) + dsk_ref[...] * up(xs_ref)
    y = y * _silu(up(z_ref))
    y = y * lax.rsqrt(jnp.mean(y * y, axis=-1, keepdims=True) + NORM_EPS) * sg_ref[...]
    o = up(of_ref) + up(ob_ref)
    parts = []
    for h in range(GLA_HEADS):
        oh = o[:, h * GLA_VAL_DIM:(h + 1) * GLA_VAL_DIM]
        parts.append(oh * lax.rsqrt(jnp.mean(oh * oh, axis=-1, keepdims=True) + NORM_EPS))
    o = jnp.concatenate(parts, axis=1) * gg_ref[...] * _silu(up(go_ref))
    cat = jnp.concatenate([y, o], axis=1).astype(bf16)
    mix = _dot(cat, wout_ref[...])
    m = mod_ref[...]
    xn = xres_ref[...] + m[2:3] * mix
    xnew_ref[...] = xn
    inv = lax.rsqrt(jnp.mean(xn * xn, axis=-1, keepdims=True) + NORM_EPS)
    h2 = (xn * inv) * (n2g_ref[...] * (1.0 + m[4:5])) + m[3:4]
    h2_ref[...] = h2
    lg_ref[...] = _dot(h2, wr_ref[...], _HI)


def _outproj(yf, yb, xs, z, of, ob, go, xres, mod_l, dsk, sg, gg, wout, n2g, wr,
             *, tm, tiles_per_row, fixed_row):
    t = xres.shape[0]
    if fixed_row is None:
        mod_map = lambda i: (i // tiles_per_row, 0, 0)
    else:
        mod_map = lambda i: (fixed_row, 0, 0)
    tok = lambda w: pl.BlockSpec((tm, w), lambda i: (i, 0))
    const = lambda a: pl.BlockSpec(a.shape, lambda i: (0,) * a.ndim)
    return pl.pallas_call(
        _outproj_kernel,
        out_shape=(jax.ShapeDtypeStruct((t, D_MODEL), f32), jax.ShapeDtypeStruct((t, D_MODEL), f32),
                   jax.ShapeDtypeStruct((t, LANES), f32)),
        grid=(t // tm,),
        in_specs=[tok(512)] * 7 + [tok(D_MODEL), pl.BlockSpec((None, 6, D_MODEL), mod_map),
                                   const(dsk), const(sg), const(gg), const(wout), const(n2g), const(wr)],
        out_specs=(tok(D_MODEL), tok(D_MODEL), tok(LANES)),
        compiler_params=_cparams(("arbitrary",)),
        name="outproj",
    )(yf, yb, xs, z, of, ob, go, xres, mod_l, dsk, sg, gg, wout, n2g, wr)


def _route_kernel(lg_ref, bias_ref, su_ref, ids_ref, rank_ref, gcol_ref, cnt_ref, carry):
    i = pl.program_id(0)
    tm = lg_ref.shape[0]

    @pl.when(i == 0)
    def _():
        carry[...] = jnp.zeros_like(carry)

    s = jax.nn.sigmoid(lg_ref[...].T[0:N_EXPERTS, :])
    sel = s + bias_ref[...]
    a = [sel[8 * m:8 * (m + 1)] for m in range(EXPERTS_PER_GROUP)]
    sv = [s[8 * m:8 * (m + 1)] for m in range(EXPERTS_PER_GROUP)]
    hi01, lo01 = jnp.maximum(a[0], a[1]), jnp.minimum(a[0], a[1])
    hi23, lo23 = jnp.maximum(a[2], a[3]), jnp.minimum(a[2], a[3])
    gscore = jnp.maximum(hi01, hi23) + jnp.maximum(jnp.minimum(hi01, hi23), jnp.maximum(lo01, lo23))
    giota = lax.broadcasted_iota(i32, gscore.shape, 0)
    gmax = jnp.max(gscore, axis=0, keepdims=True)
    gidx = jnp.min(jnp.where(gscore == gmax, giota, N_EXPERT_GROUPS), axis=0, keepdims=True)
    pick = giota == gidx
    v = [jnp.sum(jnp.where(pick, a[m], 0.0), axis=0, keepdims=True) for m in range(4)]
    w = [jnp.sum(jnp.where(pick, sv[m], 0.0), axis=0, keepdims=True) for m in range(4)]

    def first_max(vals, excluded):
        best = vals[0]
        for m in range(1, 4):
            best = jnp.maximum(best, vals[m])
        idx = jnp.full(best.shape, 3, i32)
        for m in (2, 1, 0):
            hit = vals[m] == best
            if excluded is not None:
                hit = hit & (excluded != m)
            idx = jnp.where(hit, m, idx)
        return idx

    i1 = first_max(v, None)
    v_rest = [jnp.where(i1 == m, -jnp.inf, v[m]) for m in range(4)]
    i2 = first_max(v_rest, i1)

    def take(vals, idx):
        out = vals[3]
        for m in (2, 1, 0):
            out = jnp.where(idx == m, vals[m], out)
        return out

    w1, w2 = take(w, i1), take(w, i2)
    denom = w1 + w2
    id1 = gidx * EXPERTS_PER_GROUP + i1
    id2 = gidx * EXPERTS_PER_GROUP + i2
    row2 = lax.broadcasted_iota(i32, (TOP_K, tm), 0)
    ids_ref[...] = jnp.where(row2 == 0, id1, id2)

    eiota = lax.broadcasted_iota(i32, (N_EXPERTS, tm), 0)
    hit1, hit2 = eiota == id1, eiota == id2
    onehot = jnp.where(hit1, 1.0, 0.0) + jnp.where(hit2, 1.0, 0.0)
    before = _dot(onehot.astype(bf16), su_ref[...]) + carry[...]
    r1 = jnp.sum(jnp.where(hit1, before, 0.0), axis=0, keepdims=True)
    r2 = jnp.sum(jnp.where(hit2, before, 0.0), axis=0, keepdims=True)
    rank_ref[...] = jnp.where(row2 == 0, r1, r2).astype(i32)
    new_carry = carry[...] + jnp.sum(onehot, axis=1, keepdims=True)
    carry[...] = new_carry
    cnt_ref[...] = jnp.broadcast_to(new_carry, cnt_ref.shape)

    rows = lax.broadcasted_iota(i32, (LANES, tm), 0)
    gates = jnp.where(rows == 0, w1 / denom, jnp.where(rows == 1, w2 / denom, 0.0))
    gcol_ref[...] = gates.T


def _route(logits, bias_col, su):
    t = logits.shape[0]
    tm = ROUTE_TILE
    return pl.pallas_call(
        _route_kernel,
        out_shape=(jax.ShapeDtypeStruct((2, t), i32), jax.ShapeDtypeStruct((2, t), i32),
                   jax.ShapeDtypeStruct((t, LANES), f32), jax.ShapeDtypeStruct((N_EXPERTS, LANES), f32)),
        grid=(t // tm,),
        in_specs=[pl.BlockSpec((tm, LANES), lambda i: (i, 0)),
                  pl.BlockSpec((N_EXPERTS, 1), lambda i: (0, 0)),
                  pl.BlockSpec((tm, tm), lambda i: (0, 0))],
        out_specs=(pl.BlockSpec((2, tm), lambda i: (0, i)), pl.BlockSpec((2, tm), lambda i: (0, i)),
                   pl.BlockSpec((tm, LANES), lambda i: (i, 0)),
                   pl.BlockSpec((N_EXPERTS, LANES), lambda i: (0, 0))),
        scratch_shapes=[pltpu.VMEM((N_EXPERTS, 1), f32)],
        compiler_params=_cparams(("arbitrary",)),
        name="route",
    )(logits, bias_col, su)


def _row_copy(src_ref, src_row, dst_ref, dst_row, sem):
    return pltpu.make_async_copy(src_ref.at[pl.ds(src_row, 1)], dst_ref.at[pl.ds(dst_row, 1)], sem)


def _dest_kernel(ids_ref, rank_ref, pstart_ref, dest_ref):
    ids = ids_ref[...]
    tm = ids.shape[1]
    eiota = lax.broadcasted_iota(i32, (N_EXPERTS, tm), 0)
    ps = pstart_ref[...]
    rows = [jnp.sum(jnp.where(eiota == ids[k:k + 1, :], ps, 0.0), axis=0, keepdims=True) for k in range(TOP_K)]
    row2 = lax.broadcasted_iota(i32, (TOP_K, tm), 0)
    dest_ref[...] = jnp.where(row2 == 0, rows[0], rows[1]).astype(i32) + rank_ref[...]


def _dest(ids, rank, pstart_col):
    t = ids.shape[1]
    tm = MOE_TILE
    return pl.pallas_call(
        _dest_kernel,
        out_shape=jax.ShapeDtypeStruct((t // tm, TOP_K, tm), i32),
        grid=(t // tm,),
        in_specs=[pl.BlockSpec((TOP_K, tm), lambda i: (0, i)), pl.BlockSpec((TOP_K, tm), lambda i: (0, i)),
                  pl.BlockSpec((N_EXPERTS, 1), lambda i: (0, 0))],
        out_specs=pl.BlockSpec((None, TOP_K, tm), lambda i: (i, 0, 0)),
        compiler_params=_cparams(("arbitrary",)),
        name="dest",
    )(ids, rank, pstart_col)


def _dispatch_kernel(dest_ref, h_ref, buf_in_ref, buf_ref, sem):
    del buf_in_ref
    tm = dest_ref.shape[1]

    def issue(j, carry):
        _row_copy(h_ref, j, buf_ref, dest_ref[0, j], sem).start()
        _row_copy(h_ref, j, buf_ref, dest_ref[1, j], sem).start()
        return carry

    lax.fori_loop(0, tm, issue, 0, unroll=ROW_DMA_UNROLL)

    def drain(j, carry):
        _row_copy(h_ref, 0, buf_ref, 0, sem).wait()
        _row_copy(h_ref, 0, buf_ref, 0, sem).wait()
        return carry

    lax.fori_loop(0, tm, drain, 0, unroll=ROW_DMA_UNROLL)


def _dispatch(dest_tiles, tile_off, h2, buf):
    tm = dest_tiles.shape[2]
    return pl.pallas_call(
        _dispatch_kernel,
        out_shape=jax.ShapeDtypeStruct(buf.shape, buf.dtype),
        grid=(h2.shape[0] // tm,),
        in_specs=[pl.BlockSpec((None, TOP_K, tm), lambda i: (i + tile_off, 0, 0), memory_space=pltpu.SMEM),
                  pl.BlockSpec((tm, D_MODEL), lambda i: (i, 0)), pl.BlockSpec(memory_space=pl.ANY)],
        out_specs=pl.BlockSpec(memory_space=pl.ANY),
        scratch_shapes=[pltpu.SemaphoreType.DMA],
        input_output_aliases={2: 0},
        compiler_params=_cparams(("arbitrary",)),
        name="dispatch",
    )(dest_tiles, h2, buf)


def _expert_kernel(be_ref, nu_ref, x_ref, wg_ref, wu_ref, wd_ref, y_ref, wg_b, wu_b, wd_b):
    i = pl.program_id(0)
    fresh = jnp.logical_or(i == 0, be_ref[i] != be_ref[jnp.maximum(i - 1, 0)])

    @pl.when(jnp.logical_and(fresh, i < nu_ref[0]))
    def _():
        wg_b[...] = wg_ref[...].astype(bf16)
        wu_b[...] = wu_ref[...].astype(bf16)
        wd_b[...] = wd_ref[...].astype(bf16)

    @pl.when(i < nu_ref[0])
    def _():
        xb = x_ref[...].astype(bf16)
        g = _dot(xb, wg_b[...])
        u = _dot(xb, wu_b[...])
        y_ref[...] = _dot((_silu(g) * u).astype(bf16), wd_b[...])

    @pl.when(i >= nu_ref[0])
    def _():
        y_ref[...] = jnp.zeros_like(y_ref)


def _experts(block_e, n_used, buf, w_gate, w_up, w_down, layer):
    p, d = buf.shape
    nb = p // EXPERT_BLOCK
    ff = w_gate.shape[-1]
    return pl.pallas_call(
        _expert_kernel,
        out_shape=jax.ShapeDtypeStruct((p, d), f32),
        grid_spec=pltpu.PrefetchScalarGridSpec(
            num_scalar_prefetch=2,
            grid=(nb,),
            in_specs=[pl.BlockSpec((EXPERT_BLOCK, d), lambda i, be, nu: (i, 0)),
                      pl.BlockSpec((None, None, d, ff), lambda i, be, nu: (layer, be[i], 0, 0)),
                      pl.BlockSpec((None, None, d, ff), lambda i, be, nu: (layer, be[i], 0, 0)),
                      pl.BlockSpec((None, None, ff, d), lambda i, be, nu: (layer, be[i], 0, 0))],
            out_specs=pl.BlockSpec((EXPERT_BLOCK, d), lambda i, be, nu: (i, 0)),
            scratch_shapes=[pltpu.VMEM((d, ff), bf16), pltpu.VMEM((d, ff), bf16), pltpu.VMEM((ff, d), bf16)],
        ),
        compiler_params=_cparams(("arbitrary",)),
        name="experts",
    )(block_e, n_used, buf, w_gate, w_up, w_down)


def _combine_kernel(dest_ref, y_ref, gcol_ref, x_ref, mod_ref, fg_ref, o_ref, ybuf, sem, *, final):
    tm = dest_ref.shape[1]

    def issue(j, carry):
        _row_copy(y_ref, dest_ref[0, j], ybuf.at[0], j, sem).start()
        _row_copy(y_ref, dest_ref[1, j], ybuf.at[1], j, sem).start()
        return carry

    lax.fori_loop(0, tm, issue, 0, unroll=ROW_DMA_UNROLL)

    def drain(j, carry):
        _row_copy(y_ref, 0, ybuf.at[0], 0, sem).wait()
        _row_copy(y_ref, 0, ybuf.at[1], 0, sem).wait()
        return carry

    lax.fori_loop(0, tm, drain, 0, unroll=ROW_DMA_UNROLL)

    gc = gcol_ref[...]
    ffn = gc[:, 0:1] * ybuf[0] + gc[:, 1:2] * ybuf[1]
    x = x_ref[...] + mod_ref[...][5:6] * ffn
    if final:
        x = x * lax.rsqrt(jnp.mean(x * x, axis=-1, keepdims=True) + NORM_EPS) * fg_ref[...]
    o_ref[...] = x


def _combine(dest_tiles, tile_off, y, gcol, xnew, mod_l, final_g, *, tiles_per_row, fixed_row, final):
    tm = dest_tiles.shape[2]
    t = xnew.shape[0]
    if fixed_row is None:
        mod_map = lambda i: (i // tiles_per_row, 0, 0)
    else:
        mod_map = lambda i: (fixed_row, 0, 0)
    return pl.pallas_call(
        functools.partial(_combine_kernel, final=final),
        out_shape=jax.ShapeDtypeStruct((t, D_MODEL), f32),
        grid=(t // tm,),
        in_specs=[pl.BlockSpec((None, TOP_K, tm), lambda i: (i + tile_off, 0, 0), memory_space=pltpu.SMEM),
                  pl.BlockSpec(memory_space=pl.ANY),
                  pl.BlockSpec((tm, LANES), lambda i: (i + tile_off, 0)),
                  pl.BlockSpec((tm, D_MODEL), lambda i: (i, 0)),
                  pl.BlockSpec((None, 6, D_MODEL), mod_map),
                  pl.BlockSpec((1, D_MODEL), lambda i: (0, 0))],
        out_specs=pl.BlockSpec((tm, D_MODEL), lambda i: (i, 0)),
        scratch_shapes=[pltpu.VMEM((2, tm, D_MODEL), f32), pltpu.SemaphoreType.DMA],
        compiler_params=_cparams(("arbitrary",)),
        name="combine",
    )(dest_tiles, y, gcol, xnew, mod_l, final_g)


def _moe(h2_parts, logits, router_bias_col, su, w_gate, w_up, w_down, layer, buf):
    t = logits.shape[0]
    ids, rank, gcol, cnt = _route(logits, router_bias_col, su)
    counts = cnt[:, 0].astype(i32)
    padded = (counts + EXPERT_BLOCK - 1) // EXPERT_BLOCK * EXPERT_BLOCK
    pend = jnp.cumsum(padded)
    pstart = pend - padded
    nb_needed = (t * TOP_K) // EXPERT_BLOCK + N_EXPERTS
    if buf is None:
        buf = jnp.zeros((nb_needed * EXPERT_BLOCK, D_MODEL), f32)
    nb = buf.shape[0] // EXPERT_BLOCK
    assert nb >= nb_needed
    block_pos = jnp.arange(nb, dtype=i32) * EXPERT_BLOCK
    block_e = jnp.minimum(jnp.sum((pend[None, :] <= block_pos[:, None]).astype(i32), axis=1), N_EXPERTS - 1)
    n_used = (pend[-1:] // EXPERT_BLOCK).astype(i32)
    dest_tiles = _dest(ids, rank, pstart.astype(f32)[:, None])
    tile_off = 0
    for part in h2_parts:
        buf = _dispatch(dest_tiles, tile_off, part, buf)
        tile_off += part.shape[0] // MOE_TILE
    y = _experts(block_e, n_used, buf, w_gate, w_up, w_down, layer)
    return dest_tiles, gcol, y, buf


def _pack_layer(l, w_in, conv_w, conv_b, dt_bias_f, dt_bias_b, a_log_f, a_log_b, d_skip, ssd_norm_g,
                gk_up_f, gk_bias_f, gk_up_b, gk_bias_b, gla_norm_g, w_out, norm1_g, norm2_g):
    w = w_in[l]
    z, xbc, dt, q, k, v, go, gk = jnp.split(w, [512, 1280, 1288, 1544, 1800, 2312, 2824], axis=1)
    small = jnp.concatenate([dt, gk, jnp.zeros((D_MODEL, SMALL_COLS - 24), f32)], axis=1)
    w_packed = jnp.concatenate([z, xbc, q, k, v, go, small], axis=1).astype(bf16)

    def lane_row(vec):
        return jnp.zeros((1, LANES), f32).at[0, :SSD_HEADS].set(vec)

    def sub_col(vec):
        return jnp.broadcast_to(vec[:, None], (SSD_HEADS, CHUNK)).astype(f32)

    def gup(m):
        return jnp.zeros((SMALL_COLS, GLA_QK), f32).at[SSD_HEADS:SSD_HEADS + GLA_GATE_RANK].set(m).astype(bf16)

    a_f = -jnp.exp(a_log_f[l])
    a_b = -jnp.exp(a_log_b[l])
    return dict(
        w_packed=w_packed,
        conv_w8=jnp.zeros((8, SSD_XBC), f32).at[:5].set(conv_w[l]),
        conv_b=conv_b[l][None, :],
        ssd_f=(lane_row(dt_bias_f[l]), lane_row(a_f), sub_col(dt_bias_f[l]), sub_col(a_f)),
        ssd_b=(lane_row(dt_bias_b[l]), lane_row(a_b), sub_col(dt_bias_b[l]), sub_col(a_b)),
        gla_f=(gup(gk_up_f[l]), gk_bias_f[l][None, :]),
        gla_b=(gup(gk_up_b[l]), gk_bias_b[l][None, :]),
        dsk=jnp.repeat(d_skip[l], SSD_HEAD_DIM)[None, :],
        sg=ssd_norm_g[l][None, :],
        gg=jnp.tile(gla_norm_g[l], GLA_HEADS)[None, :],
        wout=w_out[l].astype(bf16),
        n1g=norm1_g[l][None, :],
        n2g=norm2_g[l][None, :],
    )


def _constants():
    r = jnp.arange(LANES)[:, None]
    c = jnp.arange(SSD_WIDTH)[None, :]
    e_mat = ((c // SSD_HEAD_DIM) == r).astype(bf16)
    ssd_mask = ((r // SSD_STATE) == (c // (SSD_WIDTH // SSD_GROUPS))).astype(f32)
    rr = jnp.arange(GLA_WIDTH)[:, None]
    cc = jnp.arange(GLA_QK)[None, :]
    gla_mask = ((rr // GLA_VAL_DIM) == (cc // GLA_KEY_DIM)).astype(f32)
    k = jnp.arange(ROUTE_TILE)
    su = (k[:, None] < k[None, :]).astype(bf16)
    return e_mat, ssd_mask, gla_mask, su


def _mixers(streams, pk, consts, nb):
    e_mat, ssd_mask, gla_mask, _ = consts
    out = {}
    h0_f = jnp.zeros((nb, LANES, SSD_WIDTH), f32)
    h0_b = h0_f
    s0_f = jnp.zeros((nb, GLA_WIDTH, GLA_QK), f32)
    s0_b = s0_f
    for name in ('ctx', 'lat'):
        z, xs, bc, q, k, v, go, sm = streams[name]
        yf, yb, h0_f, h0_b = _ssd_scan(xs, bc, sm, h0_f, h0_b, pk['ssd_f'], pk['ssd_b'], e_mat, ssd_mask, nb=nb)
        of, ob, s0_f, s0_b = _gla_scan(q, k, v, sm, s0_f, s0_b, pk['gla_f'], pk['gla_b'], gla_mask, nb=nb)
        out[name] = (yf, yb, of, ob)
    return out


def kernel(x, c, ctx, c_ctx, w_mod, b_mod, norm1_g, norm2_g, w_in, conv_w, conv_b, dt_bias_f, dt_bias_b, a_log_f, a_log_b, d_skip, ssd_norm_g, gk_up_f, gk_bias_f, gk_up_b, gk_bias_b, gla_norm_g, w_out, w_router, router_bias, w_gate, w_up, w_down, final_norm_g):
    nb, seq, d = x.shape
    ctx_len = ctx.shape[1]
    depth = w_mod.shape[0]
    consts = _constants()
    su = consts[3]

    cvec = jnp.zeros((16, d), f32).at[:nb].set(c).at[nb].set(c_ctx)
    mod = _modulation(cvec, w_mod, b_mod).reshape(depth, 16, 6, d)
    ctx_row = nb

    perm = jnp.array([g * EXPERTS_PER_GROUP + m for m in range(EXPERTS_PER_GROUP)
                      for g in range(N_EXPERT_GROUPS)], dtype=i32)
    wr = jnp.zeros((d, LANES), f32).at[:, :N_EXPERTS].set(w_router[:, perm])
    bias_col = router_bias[perm][:, None]

    x2 = x.reshape(nb * seq, d)
    c2 = ctx.reshape(nb * ctx_len, d)
    lat_tiles = seq // TM_LAT
    moe_buf = None

    for l in range(depth):
        last = l == depth - 1
        pk = _pack_layer(l, w_in, conv_w, conv_b, dt_bias_f, dt_bias_b, a_log_f, a_log_b, d_skip, ssd_norm_g,
                         gk_up_f, gk_bias_f, gk_up_b, gk_bias_b, gla_norm_g, w_out, norm1_g, norm2_g)
        mod_l = mod[l]
        streams = {
            'ctx': _inproj(c2, mod_l, pk['n1g'], pk['w_packed'], pk['conv_w8'], pk['conv_b'],
                           tm=ctx_len, rowlen=ctx_len, tiles_per_row=1, fixed_row=ctx_row),
            'lat': _inproj(x2, mod_l, pk['n1g'], pk['w_packed'], pk['conv_w8'], pk['conv_b'],
                           tm=TM_LAT, rowlen=GRID_W, tiles_per_row=lat_tiles, fixed_row=None),
        }
        mix = _mixers(streams, pk, consts, nb)

        def merge(name, xres, tm, tiles_per_row, fixed_row):
            z, xs, bc, q, k, v, go, sm = streams[name]
            yf, yb, of, ob = mix[name]
            return _outproj(yf, yb, xs, z, of, ob, go, xres, mod_l, pk['dsk'], pk['sg'], pk['gg'],
                            pk['wout'], pk['n2g'], wr, tm=tm, tiles_per_row=tiles_per_row, fixed_row=fixed_row)

        xn_lat, h2_lat, lg_lat = merge('lat', x2, TM_LAT, lat_tiles, None)
        if last:
            parts, logits = [h2_lat], lg_lat
        else:
            xn_ctx, h2_ctx, lg_ctx = merge('ctx', c2, ctx_len, 1, ctx_row)
            parts, logits = [h2_ctx, h2_lat], jnp.concatenate([lg_ctx, lg_lat], axis=0)
        dest_tiles, gcol, y, moe_buf = _moe(parts, logits, bias_col, su, w_gate, w_up, w_down, l, moe_buf)
        lat_off = 0
        if not last:
            c2 = _combine(dest_tiles, 0, y, gcol, xn_ctx, mod_l, final_norm_g[None, :],
                          tiles_per_row=1, fixed_row=ctx_row, final=False)
            lat_off = xn_ctx.shape[0] // MOE_TILE
        x2 = _combine(dest_tiles, lat_off, y, gcol, xn_lat, mod_l, final_norm_g[None, :],
                      tiles_per_row=seq // MOE_TILE, fixed_row=None, final=last)
    return x2.reshape(nb, seq, d)
```

```python
import functools

import jax
import jax.numpy as jnp
from jax import lax
from jax.experimental import pallas as pl
from jax.experimental.pallas import tpu as pltpu

f32 = jnp.float32
bf16 = jnp.bfloat16
i32 = jnp.int32
u32 = jnp.uint32

D_MODEL = 1024
SSD_HEADS = 8
SSD_HEAD_DIM = 64
SSD_WIDTH = 512
SSD_GROUPS = 2
SSD_STATE = 64
SSD_XBC = 768
GLA_HEADS = 4
GLA_KEY_DIM = 64
GLA_VAL_DIM = 128
GLA_QK = 256
GLA_WIDTH = 512
GLA_GATE_RANK = 16
GLA_GATE_NORM = 16.0
GRID_W = 64
N_EXPERTS = 32
N_EXPERT_GROUPS = 8
EXPERTS_PER_GROUP = 4
TOP_K = 2
EXPERT_FF = 512
NORM_EPS = 1e-6

LANES = 128
CHUNK = 128
CHUNKS_PER_STEP = 4
INTERLEAVE_WAVE = 8
ROW_SPLIT = 2
SSD_SPLIT = 3
GLA_SPLIT = 2
MAIN_COLS = 2816
SMALL_COLS = LANES
TM_LAT = 512
ROUTE_TILE = 512
MOE_TILE = 512
ROW_DMA_UNROLL = 8
PACKED_COLS = D_MODEL // 2
EXPERT_BLOCK = 512
VMEM_LIMIT = 48 * 1024 * 1024

_HI = lax.Precision.HIGHEST
_NT = (((1,), (1,)), ((), ()))


def _dot(a, b, precision=None):
    return jnp.dot(a, b, preferred_element_type=f32, precision=precision)


def _silu(x):
    return x * jax.nn.sigmoid(x)


def _row_spec(rows, block_index):
    return pl.BlockSpec((rows, PACKED_COLS), lambda *idx: (block_index(*idx), 0))


def _pack_rows(value):
    hi = pltpu.bitcast(value[:, :PACKED_COLS].astype(bf16).astype(f32), u32)
    lo = pltpu.bitcast(value[:, PACKED_COLS:].astype(bf16).astype(f32), u32)
    return hi | (lo >> 16)


def _unpack_rows(packed):
    hi = pltpu.bitcast(packed & jnp.uint32(0xFFFF0000), f32)
    lo = pltpu.bitcast(packed << 16, f32)
    return jnp.concatenate([hi, lo], axis=1)


def _cparams(sem):
    return pltpu.CompilerParams(dimension_semantics=sem, vmem_limit_bytes=VMEM_LIMIT)


def _mod_kernel(c_ref, w_ref, b_ref, o_ref):
    sc = _silu(c_ref[...]).astype(bf16)
    o_ref[...] = _dot(sc, w_ref[...].astype(bf16)) + b_ref[...]


def _modulation(cvec, w_mod, b_mod):
    depth, d, n = w_mod.shape
    tn = 1536
    return pl.pallas_call(
        _mod_kernel,
        out_shape=jax.ShapeDtypeStruct((depth, 16, n), f32),
        grid=(depth, n // tn),
        in_specs=[pl.BlockSpec((16, d), lambda l, j: (0, 0)),
                  pl.BlockSpec((None, d, tn), lambda l, j: (l, 0, j)),
                  pl.BlockSpec((None, 1, tn), lambda l, j: (l, 0, j))],
        out_specs=pl.BlockSpec((None, 16, tn), lambda l, j: (l, 0, j)),
        compiler_params=_cparams(("arbitrary", "arbitrary")),
        name="modulation",
    )(cvec, w_mod, b_mod.reshape(depth, 1, n))


def _inproj_kernel(x_ref, mod_ref, g_ref, w_ref, cw_ref, cb_ref,
                   z_ref, xs_ref, bc_ref, q_ref, k_ref, v_ref, go_ref, sm_ref, *, rowlen):
    tm = x_ref.shape[0]
    split = ROW_SPLIT if (tm // ROW_SPLIT) % rowlen == 0 else 1
    sub = tm // split
    m = mod_ref[...]
    geff = g_ref[...] * (1.0 + m[1:2])
    cw = cw_ref[...]

    def rows_gen(rows):
        x = x_ref[rows, :]
        inv = lax.rsqrt(jnp.mean(x * x, axis=-1, keepdims=True) + NORM_EPS)
        h = ((x * inv) * geff + m[0:1]).astype(bf16)
        yield

        def proj(lo, hi):
            return _dot(h, w_ref[:, lo:hi])

        z_ref[rows, :] = proj(0, 512).astype(bf16)
        yield

        xbc = proj(512, 1280)
        yield
        pos = lax.broadcasted_iota(i32, xbc.shape, 0) & (rowlen - 1)
        acc = xbc * cw[2:3]
        for d in (-2, -1, 1, 2):
            shifted = pltpu.roll(xbc, (-d) % sub, 0)
            valid = (pos >= -d) if d < 0 else (pos <= rowlen - 1 - d)
            acc = acc + jnp.where(valid, shifted, 0.0) * cw[2 + d:3 + d]
        yield
        act = _silu(acc + cb_ref[...])
        xs_ref[rows, :] = act[:, :SSD_WIDTH].astype(bf16)
        bc_ref[rows, :] = act[:, SSD_WIDTH:].astype(bf16)
        yield

        qk = proj(1280, 1792)
        q_ref[rows, :] = qk[:, :GLA_QK].astype(bf16)
        k_ref[rows, :] = qk[:, GLA_QK:].astype(bf16)
        yield
        v_ref[rows, :] = proj(1792, 2304).astype(bf16)
        yield
        go_ref[rows, :] = proj(2304, 2816).astype(bf16)
        yield
        sm_ref[rows, :] = proj(2816, 2944)

    _run_interleaved(rows_gen(pl.ds(r * sub, sub)) for r in range(split))


def _inproj(x2d, mod_l, norm_g, w_packed, conv_w8, conv_b, *, tm, rowlen, tiles_per_row, fixed_row):
    t = x2d.shape[0]
    nt = t // tm
    if fixed_row is None:
        mod_map = lambda i: (i // tiles_per_row, 0, 0)
    else:
        mod_map = lambda i: (fixed_row, 0, 0)
    tok = lambda w: pl.BlockSpec((tm, w), lambda i: (i, 0))
    const = lambda a: pl.BlockSpec(a.shape, lambda i: (0,) * a.ndim)
    widths = (512, 512, 256, 256, 256, 512, 512)
    return pl.pallas_call(
        functools.partial(_inproj_kernel, rowlen=rowlen),
        out_shape=tuple(jax.ShapeDtypeStruct((t, w), bf16) for w in widths)
        + (jax.ShapeDtypeStruct((t, SMALL_COLS), f32),),
        grid=(nt,),
        in_specs=[tok(D_MODEL), pl.BlockSpec((None, 6, D_MODEL), mod_map), const(norm_g),
                  const(w_packed), const(conv_w8), const(conv_b)],
        out_specs=tuple(tok(w) for w in widths) + (tok(SMALL_COLS),),
        compiler_params=_cparams(("arbitrary",)),
        name="inproj",
    )(x2d, mod_l, norm_g, w_packed, conv_w8, conv_b)


def _dot_split(a, b, parts, *, split_lhs):
    rest = a if split_lhs else b
    acc = None
    for _ in range(parts):
        piece = rest.astype(bf16)
        rest = rest - piece.astype(f32)
        term = _dot(piece, b) if split_lhs else _dot(a, piece)
        acc = term if acc is None else acc + term
    return acc


def _tri_masks(q):
    r = lax.broadcasted_iota(i32, (q, q), 0)
    c = lax.broadcasted_iota(i32, (q, q), 1)
    return r >= c, r <= c


def _ssd_chunk(load, store, carry, par, e, bm, rev):
    xb, bcv, sm = load()
    dtb, a_lane, dtbt, a_sub = par
    q = xb.shape[0]
    dt_col = jax.nn.softplus(sm + dtb)
    a_col = dt_col * a_lane
    dt_row = jax.nn.softplus(sm.T[0:SSD_HEADS, :] + dtbt)
    a_row = dt_row * a_sub
    yield
    lower, upper = _tri_masks(q)
    lo_b = jnp.where(lower, 1.0, 0.0).astype(bf16)
    up_b = jnp.where(upper, 1.0, 0.0).astype(bf16)
    if not rev:
        cs_col = _dot_split(lo_b, a_col, SSD_SPLIT, split_lhs=False)
        cs_row = _dot_split(a_row, up_b, SSD_SPLIT, split_lhs=True)
        mask = lower
        a_tot = cs_col[q - 1:q, :]
    else:
        cs_col = _dot_split(up_b, a_col, SSD_SPLIT, split_lhs=False)
        cs_row = _dot_split(a_row, lo_b, SSD_SPLIT, split_lhs=True)
        mask = upper
        a_tot = cs_col[0:1, :]
    yield

    w_exp = _dot((jnp.exp(a_tot - cs_col) * dt_col).astype(bf16), e)
    ecs_exp = _dot(jnp.exp(cs_col).astype(bf16), e)
    dec_exp = _dot_split(jnp.broadcast_to(jnp.exp(a_tot), (8, LANES)), e, SSD_SPLIT, split_lhs=True)[0:1]
    yield

    xw = (xb.astype(f32) * w_exp).astype(bf16)
    b_all = bcv[:, 0:LANES]
    c_all = bcv[:, LANES:2 * LANES]
    b_t = b_all.astype(f32).T.astype(bf16)
    yield
    s_upd = _dot(b_t, xw)
    yield

    lane = lax.broadcasted_iota(i32, (q, LANES), 1)
    zero_b = jnp.zeros((q, LANES), bf16)
    ys = []
    for g in range(SSD_GROUPS):
        in_g = (lane >= SSD_STATE * g) & (lane < SSD_STATE * (g + 1))
        cb = _dot(jnp.where(in_g, c_all, zero_b), b_t)
        yield
        for pp in range(2):
            h0 = 4 * g + 2 * pp
            ms = []
            for h in (h0, h0 + 1):
                seg = cs_col[:, h:h + 1] - cs_row[h:h + 1, :]
                dec = jnp.exp(jnp.where(mask, seg, -1e30))
                ms.append((cb * dec * dt_row[h:h + 1, :]).astype(bf16))
            xp = xb[:, h0 * SSD_HEAD_DIM:h0 * SSD_HEAD_DIM + LANES]
            rhs = jnp.concatenate([jnp.where(lane < SSD_HEAD_DIM, xp, zero_b),
                                   jnp.where(lane >= SSD_HEAD_DIM, xp, zero_b)], axis=0)
            yield
            ys.append(_dot(jnp.concatenate(ms, axis=1), rhs))
            yield
    y_intra = jnp.concatenate(ys, axis=1)
    yield

    s_old = carry['s']
    carry['s'] = (s_old * dec_exp + s_upd) * bm
    store((y_intra + _dot(c_all, s_old.astype(bf16)) * ecs_exp).astype(bf16))


def _run_interleaved(gens):
    gens = list(gens)
    done = object()
    for w in range(0, len(gens), INTERLEAVE_WAVE):
        live = gens[w:w + INTERLEAVE_WAVE]
        while live:
            live = [g for g in live if next(g, done) is not done]


def _chunk_rows(cps):
    return [(pl.ds(ci * CHUNK, CHUNK), pl.ds((cps - 1 - ci) * CHUNK, CHUNK)) for ci in range(cps)]


def _ssd_kernel(xf_ref, bcf_ref, smf_ref, xr_ref, bcr_ref, smr_ref, h0f_ref, h0r_ref,
                dtbf_ref, af_ref, dtbtf_ref, atf_ref, dtbr_ref, ar_ref, dtbtr_ref, atr_ref, e_ref, bm_ref,
                yf_ref, yr_ref, hff_ref, hfr_ref, st_f, st_r, *, cps, nblk):
    j = pl.program_id(1)

    @pl.when(j == 0)
    def _():
        st_f[...] = h0f_ref[...]
        st_r[...] = h0r_ref[...]

    e = e_ref[...]
    bm = bm_ref[...]
    par_f = (dtbf_ref[...], af_ref[...], dtbtf_ref[...], atf_ref[...])
    par_r = (dtbr_ref[...], ar_ref[...], dtbtr_ref[...], atr_ref[...])
    carry_f = {'s': st_f[...]}
    carry_r = {'s': st_r[...]}

    def chunk(refs, out_ref, rows, carry, par, rev):
        def store(y):
            out_ref[rows, :] = y
        return _ssd_chunk(lambda: tuple(r[rows, :] for r in refs), store, carry, par, e, bm, rev)

    gens = []
    for rows_f, rows_r in _chunk_rows(cps):
        gens.append(chunk((xf_ref, bcf_ref, smf_ref), yf_ref, rows_f, carry_f, par_f, False))
        gens.append(chunk((xr_ref, bcr_ref, smr_ref), yr_ref, rows_r, carry_r, par_r, True))
    _run_interleaved(gens)
    st_f[...] = carry_f['s']
    st_r[...] = carry_r['s']

    @pl.when(j == nblk - 1)
    def _():
        hff_ref[...] = carry_f['s']
        hfr_ref[...] = carry_r['s']


def _scan_specs(t, nb):
    per_row = t // nb
    cps = min(CHUNKS_PER_STEP, per_row // CHUNK)
    nblk = per_row // (cps * CHUNK)
    fmap = lambda b, j: (b * nblk + j, 0)
    rmap = lambda b, j: (b * nblk + nblk - 1 - j, 0)
    return cps, nblk, fmap, rmap


def _ssd_scan(xs, bc, sm, h0_f, h0_r, par_f, par_r, e_mat, bmask, *, nb):
    t = xs.shape[0]
    cps, nblk, fmap, rmap = _scan_specs(t, nb)
    blk = cps * CHUNK
    const = lambda a: pl.BlockSpec(a.shape, lambda b, j: (0,) * a.ndim)
    st = pl.BlockSpec((None, LANES, SSD_WIDTH), lambda b, j: (b, 0, 0))
    widths = (SSD_WIDTH, 2 * LANES, SMALL_COLS)
    tok_in = [pl.BlockSpec((blk, w), m) for m in (fmap, rmap) for w in widths]
    consts = (*par_f, *par_r, e_mat, bmask)
    return pl.pallas_call(
        functools.partial(_ssd_kernel, cps=cps, nblk=nblk),
        out_shape=(jax.ShapeDtypeStruct((t, SSD_WIDTH), bf16), jax.ShapeDtypeStruct((t, SSD_WIDTH), bf16),
                   jax.ShapeDtypeStruct((nb, LANES, SSD_WIDTH), f32),
                   jax.ShapeDtypeStruct((nb, LANES, SSD_WIDTH), f32)),
        grid=(nb, nblk),
        in_specs=tok_in + [st, st] + [const(a) for a in consts],
        out_specs=(pl.BlockSpec((blk, SSD_WIDTH), fmap), pl.BlockSpec((blk, SSD_WIDTH), rmap), st, st),
        scratch_shapes=[pltpu.VMEM((LANES, SSD_WIDTH), f32), pltpu.VMEM((LANES, SSD_WIDTH), f32)],
        compiler_params=_cparams(("arbitrary", "arbitrary")),
        name="ssd",
    )(xs, bc, sm, xs, bc, sm, h0_f, h0_r, *consts)


def _gla_chunk(load, store, carry, par, bm, rev):
    qb, kb, vb, sm = load()
    gup, gbias = par
    q = qb.shape[0]
    gp = _dot(sm.astype(bf16), gup) + gbias
    yield
    g = jax.nn.log_sigmoid(gp) * (1.0 / GLA_GATE_NORM)
    yield
    lower, upper = _tri_masks(q)
    if not rev:
        b = _dot_split(jnp.where(lower, 1.0, 0.0).astype(bf16), g, GLA_SPLIT, split_lhs=False)
        mask, mid = lower, q // 2 - 1
        b_tot = b[q - 1:q, :]
    else:
        b = _dot_split(jnp.where(upper, 1.0, 0.0).astype(bf16), g, GLA_SPLIT, split_lhs=False)
        mask, mid = upper, q // 2
        b_tot = b[0:1, :]
    b_mid = b[mid:mid + 1, :]
    yield

    qf = qb.astype(f32) * (GLA_KEY_DIM ** -0.5)
    kf = kb.astype(f32)
    qd = (qf * jnp.exp(b - b_mid)).astype(bf16)
    yield
    ki = (kf * jnp.exp(b_mid - b)).astype(bf16)
    yield
    q_st = (qf * jnp.exp(b)).astype(bf16)
    yield
    k_end = (kf * jnp.exp(b_tot - b)).astype(bf16)
    dec = jnp.exp(b_tot)
    yield
    v_t = vb.astype(f32).T.astype(bf16)
    yield
    s_upd = _dot(v_t, k_end)
    yield

    lane = lax.broadcasted_iota(i32, (q, LANES), 1)
    zero_b = jnp.zeros((q, LANES), bf16)
    outs = []
    for h in range(GLA_HEADS):
        p, hh = divmod(h, 2)
        qp = qd[:, p * LANES:(p + 1) * LANES]
        kp = ki[:, p * LANES:(p + 1) * LANES]
        in_h = (lane >= GLA_KEY_DIM * hh) & (lane < GLA_KEY_DIM * (hh + 1))
        s = lax.dot_general(jnp.where(in_h, qp, zero_b), kp, _NT, preferred_element_type=f32)
        yield
        attn = jnp.where(mask, s, 0.0).astype(bf16)
        outs.append(_dot(attn, vb[:, h * GLA_VAL_DIM:(h + 1) * GLA_VAL_DIM]))
        yield
    o_intra = jnp.concatenate(outs, axis=1)
    yield

    s_old = carry['s']
    carry['s'] = (s_old * dec + s_upd) * bm
    o_inter = lax.dot_general(q_st, s_old.astype(bf16), _NT, preferred_element_type=f32)
    store((o_intra + o_inter).astype(bf16))


def _gla_kernel(qf_ref, kf_ref, vf_ref, smf_ref, qr_ref, kr_ref, vr_ref, smr_ref, s0f_ref, s0r_ref,
                gupf_ref, gbf_ref, gupr_ref, gbr_ref, bm_ref,
                of_ref, or_ref, sff_ref, sfr_ref, st_f, st_r, *, cps, nblk):
    j = pl.program_id(1)

    @pl.when(j == 0)
    def _():
        st_f[...] = s0f_ref[...]
        st_r[...] = s0r_ref[...]

    bm = bm_ref[...]
    par_f = (gupf_ref[...], gbf_ref[...])
    par_r = (gupr_ref[...], gbr_ref[...])
    carry_f = {'s': st_f[...]}
    carry_r = {'s': st_r[...]}

    def chunk(refs, out_ref, rows, carry, par, rev):
        def store(o):
            out_ref[rows, :] = o
        return _gla_chunk(lambda: tuple(r[rows, :] for r in refs), store, carry, par, bm, rev)

    gens = []
    for rows_f, rows_r in _chunk_rows(cps):
        gens.append(chunk((qf_ref, kf_ref, vf_ref, smf_ref), of_ref, rows_f, carry_f, par_f, False))
        gens.append(chunk((qr_ref, kr_ref, vr_ref, smr_ref), or_ref, rows_r, carry_r, par_r, True))
    _run_interleaved(gens)
    st_f[...] = carry_f['s']
    st_r[...] = carry_r['s']

    @pl.when(j == nblk - 1)
    def _():
        sff_ref[...] = carry_f['s']
        sfr_ref[...] = carry_r['s']


def _gla_scan(qa, ka, va, sm, s0_f, s0_r, par_f, par_r, bmask, *, nb):
    t = qa.shape[0]
    cps, nblk, fmap, rmap = _scan_specs(t, nb)
    blk = cps * CHUNK
    const = lambda a: pl.BlockSpec(a.shape, lambda b, j: (0,) * a.ndim)
    st = pl.BlockSpec((None, GLA_WIDTH, GLA_QK), lambda b, j: (b, 0, 0))
    widths = (GLA_QK, GLA_QK, GLA_WIDTH, SMALL_COLS)
    tok_in = [pl.BlockSpec((blk, w), m) for m in (fmap, rmap) for w in widths]
    consts = (*par_f, *par_r, bmask)
    return pl.pallas_call(
        functools.partial(_gla_kernel, cps=cps, nblk=nblk),
        out_shape=(jax.ShapeDtypeStruct((t, GLA_WIDTH), bf16), jax.ShapeDtypeStruct((t, GLA_WIDTH), bf16),
                   jax.ShapeDtypeStruct((nb, GLA_WIDTH, GLA_QK), f32),
                   jax.ShapeDtypeStruct((nb, GLA_WIDTH, GLA_QK), f32)),
        grid=(nb, nblk),
        in_specs=tok_in + [st, st] + [const(a) for a in consts],
        out_specs=(pl.BlockSpec((blk, GLA_WIDTH), fmap), pl.BlockSpec((blk, GLA_WIDTH), rmap), st, st),
        scratch_shapes=[pltpu.VMEM((GLA_WIDTH, GLA_QK), f32), pltpu.VMEM((GLA_WIDTH, GLA_QK), f32)],
        compiler_params=_cparams(("arbitrary", "arbitrary")),
        name="gla",
    )(qa, ka, va, sm, qa, ka, va, sm, s0_f, s0_r, *consts)


def _outproj_kernel(yf_ref, yb_ref, xs_ref, z_ref, of_ref, ob_ref, go_ref, xres_ref, mod_ref,
                    dsk_ref, sg_ref, gg_ref, wout_ref, n2g_ref, wr_ref,
                    xnew_ref, h2_ref, lg_ref):
    tm = xres_ref.shape[0]
    sub = tm // ROW_SPLIT
    m = mod_ref[...]
    h2_gain = n2g_ref[...] * (1.0 + m[4:5])

    def rows_gen(rows):
        up = lambda r: r[rows, :].astype(f32)
        y = up(yf_ref) + up(yb_ref) + dsk_ref[...] * up(xs_ref)
        yield
        y = y * _silu(up(z_ref))
        yield
        y = y * lax.rsqrt(jnp.mean(y * y, axis=-1, keepdims=True) + NORM_EPS) * sg_ref[...]
        yield
        o = up(of_ref) + up(ob_ref)
        parts = []
        for h in range(GLA_HEADS):
            oh = o[:, h * GLA_VAL_DIM:(h + 1) * GLA_VAL_DIM]
            parts.append(oh * lax.rsqrt(jnp.mean(oh * oh, axis=-1, keepdims=True) + NORM_EPS))
        yield
        o = jnp.concatenate(parts, axis=1) * gg_ref[...] * _silu(up(go_ref))
        yield
        cat = jnp.concatenate([y, o], axis=1).astype(bf16)
        mix = _dot(cat, wout_ref[...])
        yield
        xn = xres_ref[rows, :] + m[2:3] * mix
        xnew_ref[rows, :] = xn
        yield
        inv = lax.rsqrt(jnp.mean(xn * xn, axis=-1, keepdims=True) + NORM_EPS)
        h2 = (xn * inv) * h2_gain + m[3:4]
        h2_ref[rows, :] = _pack_rows(h2)
        yield
        h2_hi = h2.astype(bf16)
        h2_lo = (h2 - h2_hi.astype(f32)).astype(bf16)
        wr_hi = wr_ref[0]
        lg_ref[rows, :] = _dot(h2_hi, wr_hi) + _dot(h2_hi, wr_ref[1]) + _dot(h2_lo, wr_hi)

    _run_interleaved(rows_gen(pl.ds(r * sub, sub)) for r in range(ROW_SPLIT))


def _outproj(yf, yb, xs, z, of, ob, go, xres, mod_l, dsk, sg, gg, wout, n2g, wr,
             *, tm, tiles_per_row, fixed_row):
    t = xres.shape[0]
    if fixed_row is None:
        mod_map = lambda i: (i // tiles_per_row, 0, 0)
    else:
        mod_map = lambda i: (fixed_row, 0, 0)
    tok = lambda w: pl.BlockSpec((tm, w), lambda i: (i, 0))
    const = lambda a: pl.BlockSpec(a.shape, lambda i: (0,) * a.ndim)
    return pl.pallas_call(
        _outproj_kernel,
        out_shape=(jax.ShapeDtypeStruct((t, D_MODEL), f32), jax.ShapeDtypeStruct((t, PACKED_COLS), u32),
                   jax.ShapeDtypeStruct((t, LANES), f32)),
        grid=(t // tm,),
        in_specs=[tok(512)] * 7 + [tok(D_MODEL), pl.BlockSpec((None, 6, D_MODEL), mod_map),
                                   const(dsk), const(sg), const(gg), const(wout), const(n2g), const(wr)],
        out_specs=(tok(D_MODEL), _row_spec(tm, lambda i: i), tok(LANES)),
        compiler_params=_cparams(("arbitrary",)),
        name="outproj",
    )(yf, yb, xs, z, of, ob, go, xres, mod_l, dsk, sg, gg, wout, n2g, wr)


def _route_kernel(lg_ref, bias_ref, su_ref, ids_ref, rank_ref, gcol_ref, cnt_ref, carry):
    i = pl.program_id(0)
    tm = lg_ref.shape[0]

    @pl.when(i == 0)
    def _():
        carry[...] = jnp.zeros_like(carry)

    s = jax.nn.sigmoid(lg_ref[...].T[0:N_EXPERTS, :])
    sel = s + bias_ref[...]
    a = [sel[8 * m:8 * (m + 1)] for m in range(EXPERTS_PER_GROUP)]
    sv = [s[8 * m:8 * (m + 1)] for m in range(EXPERTS_PER_GROUP)]
    hi01, lo01 = jnp.maximum(a[0], a[1]), jnp.minimum(a[0], a[1])
    hi23, lo23 = jnp.maximum(a[2], a[3]), jnp.minimum(a[2], a[3])
    gscore = jnp.maximum(hi01, hi23) + jnp.maximum(jnp.minimum(hi01, hi23), jnp.maximum(lo01, lo23))
    giota = lax.broadcasted_iota(i32, gscore.shape, 0)
    gmax = jnp.max(gscore, axis=0, keepdims=True)
    gidx = jnp.min(jnp.where(gscore == gmax, giota, N_EXPERT_GROUPS), axis=0, keepdims=True)
    pick = giota == gidx
    v = [jnp.sum(jnp.where(pick, a[m], 0.0), axis=0, keepdims=True) for m in range(4)]
    w = [jnp.sum(jnp.where(pick, sv[m], 0.0), axis=0, keepdims=True) for m in range(4)]

    def first_max(vals, excluded):
        best = vals[0]
        for m in range(1, 4):
            best = jnp.maximum(best, vals[m])
        idx = jnp.full(best.shape, 3, i32)
        for m in (2, 1, 0):
            hit = vals[m] == best
            if excluded is not None:
                hit = hit & (excluded != m)
            idx = jnp.where(hit, m, idx)
        return idx

    i1 = first_max(v, None)
    v_rest = [jnp.where(i1 == m, -jnp.inf, v[m]) for m in range(4)]
    i2 = first_max(v_rest, i1)

    def take(vals, idx):
        out = vals[3]
        for m in (2, 1, 0):
            out = jnp.where(idx == m, vals[m], out)
        return out

    w1, w2 = take(w, i1), take(w, i2)
    denom = w1 + w2
    id1 = gidx * EXPERTS_PER_GROUP + i1
    id2 = gidx * EXPERTS_PER_GROUP + i2
    row2 = lax.broadcasted_iota(i32, (TOP_K, tm), 0)
    ids_ref[...] = jnp.where(row2 == 0, id1, id2)

    eiota = lax.broadcasted_iota(i32, (N_EXPERTS, tm), 0)
    hit1, hit2 = eiota == id1, eiota == id2
    onehot = jnp.where(hit1, 1.0, 0.0) + jnp.where(hit2, 1.0, 0.0)
    before = _dot(onehot.astype(bf16), su_ref[...]) + carry[...]
    r1 = jnp.sum(jnp.where(hit1, before, 0.0), axis=0, keepdims=True)
    r2 = jnp.sum(jnp.where(hit2, before, 0.0), axis=0, keepdims=True)
    rank_ref[...] = jnp.where(row2 == 0, r1, r2).astype(i32)
    new_carry = carry[...] + jnp.sum(onehot, axis=1, keepdims=True)
    carry[...] = new_carry
    cnt_ref[...] = jnp.broadcast_to(new_carry, cnt_ref.shape)

    rows = lax.broadcasted_iota(i32, (LANES, tm), 0)
    gates = jnp.where(rows == 0, w1 / denom, jnp.where(rows == 1, w2 / denom, 0.0))
    gcol_ref[...] = gates.T


def _route(logits, bias_col, su):
    t = logits.shape[0]
    tm = ROUTE_TILE
    return pl.pallas_call(
        _route_kernel,
        out_shape=(jax.ShapeDtypeStruct((2, t), i32), jax.ShapeDtypeStruct((2, t), i32),
                   jax.ShapeDtypeStruct((t, LANES), f32), jax.ShapeDtypeStruct((N_EXPERTS, LANES), f32)),
        grid=(t // tm,),
        in_specs=[pl.BlockSpec((tm, LANES), lambda i: (i, 0)),
                  pl.BlockSpec((N_EXPERTS, 1), lambda i: (0, 0)),
                  pl.BlockSpec((tm, tm), lambda i: (0, 0))],
        out_specs=(pl.BlockSpec((2, tm), lambda i: (0, i)), pl.BlockSpec((2, tm), lambda i: (0, i)),
                   pl.BlockSpec((tm, LANES), lambda i: (i, 0)),
                   pl.BlockSpec((N_EXPERTS, LANES), lambda i: (0, 0))),
        scratch_shapes=[pltpu.VMEM((N_EXPERTS, 1), f32)],
        compiler_params=_cparams(("arbitrary",)),
        name="route",
    )(logits, bias_col, su)


def _row_copy(src_ref, src_row, dst_ref, dst_row, sem):
    return pltpu.make_async_copy(src_ref.at[pl.ds(src_row, 1)], dst_ref.at[pl.ds(dst_row, 1)], sem)


def _dest_kernel(ids_ref, rank_ref, pstart_ref, dest_ref):
    ids = ids_ref[...]
    tm = ids.shape[1]
    eiota = lax.broadcasted_iota(i32, (N_EXPERTS, tm), 0)
    ps = pstart_ref[...]
    rows = [jnp.sum(jnp.where(eiota == ids[k:k + 1, :], ps, 0.0), axis=0, keepdims=True) for k in range(TOP_K)]
    row2 = lax.broadcasted_iota(i32, (TOP_K, tm), 0)
    dest_ref[...] = jnp.where(row2 == 0, rows[0], rows[1]).astype(i32) + rank_ref[...]


def _dest(ids, rank, pstart_col):
    t = ids.shape[1]
    tm = MOE_TILE
    return pl.pallas_call(
        _dest_kernel,
        out_shape=jax.ShapeDtypeStruct((t // tm, TOP_K, tm), i32),
        grid=(t // tm,),
        in_specs=[pl.BlockSpec((TOP_K, tm), lambda i: (0, i)), pl.BlockSpec((TOP_K, tm), lambda i: (0, i)),
                  pl.BlockSpec((N_EXPERTS, 1), lambda i: (0, 0))],
        out_specs=pl.BlockSpec((None, TOP_K, tm), lambda i: (i, 0, 0)),
        compiler_params=_cparams(("arbitrary",)),
        name="dest",
    )(ids, rank, pstart_col)


def _dispatch_kernel(dest_ref, h_ref, buf_in_ref, buf_ref, sem):
    del buf_in_ref
    tm = dest_ref.shape[1]

    def issue(j, carry):
        _row_copy(h_ref, j, buf_ref, dest_ref[0, j], sem).start()
        _row_copy(h_ref, j, buf_ref, dest_ref[1, j], sem).start()
        return carry

    lax.fori_loop(0, tm, issue, 0, unroll=ROW_DMA_UNROLL)

    def drain(j, carry):
        _row_copy(h_ref, 0, buf_ref, 0, sem).wait()
        _row_copy(h_ref, 0, buf_ref, 0, sem).wait()
        return carry

    lax.fori_loop(0, tm, drain, 0, unroll=ROW_DMA_UNROLL)


def _dispatch(dest_tiles, tile_off, h2, buf):
    tm = dest_tiles.shape[2]
    return pl.pallas_call(
        _dispatch_kernel,
        out_shape=jax.ShapeDtypeStruct(buf.shape, buf.dtype),
        grid=(h2.shape[0] // tm,),
        in_specs=[pl.BlockSpec((None, TOP_K, tm), lambda i: (i + tile_off, 0, 0), memory_space=pltpu.SMEM),
                  _row_spec(tm, lambda i: i), pl.BlockSpec(memory_space=pl.ANY)],
        out_specs=pl.BlockSpec(memory_space=pl.ANY),
        scratch_shapes=[pltpu.SemaphoreType.DMA],
        input_output_aliases={2: 0},
        compiler_params=_cparams(("arbitrary",)),
        name="dispatch",
    )(dest_tiles, h2, buf)


def _expert_kernel(be_ref, nu_ref, x_ref, wg_ref, wu_ref, wd_ref, y_ref, wg_b, wu_b, wd_b):
    i = pl.program_id(0)
    fresh = jnp.logical_or(i == 0, be_ref[i] != be_ref[jnp.maximum(i - 1, 0)])

    @pl.when(jnp.logical_and(fresh, i < nu_ref[0]))
    def _():
        wg_b[...] = wg_ref[...].astype(bf16)
        wu_b[...] = wu_ref[...].astype(bf16)
        wd_b[...] = wd_ref[...].astype(bf16)

    @pl.when(i < nu_ref[0])
    def _():
        sub = x_ref.shape[0] // ROW_SPLIT

        def rows_gen(rows):
            xb = _unpack_rows(x_ref[rows, :]).astype(bf16)
            yield
            g = _dot(xb, wg_b[...])
            yield
            u = _dot(xb, wu_b[...])
            yield
            hmid = (_silu(g) * u).astype(bf16)
            yield
            y_ref[rows, :] = _pack_rows(_dot(hmid, wd_b[...]))

        _run_interleaved(rows_gen(pl.ds(r * sub, sub)) for r in range(ROW_SPLIT))

    @pl.when(i >= nu_ref[0])
    def _():
        y_ref[...] = jnp.zeros_like(y_ref)


def _experts(block_e, n_used, buf, w_gate, w_up, w_down, layer):
    p = buf.shape[0]
    nb = p // EXPERT_BLOCK
    d, ff = w_gate.shape[-2:]
    row_blocks = _row_spec(EXPERT_BLOCK, lambda i, be, nu: i)
    return pl.pallas_call(
        _expert_kernel,
        out_shape=jax.ShapeDtypeStruct(buf.shape, u32),
        grid_spec=pltpu.PrefetchScalarGridSpec(
            num_scalar_prefetch=2,
            grid=(nb,),
            in_specs=[row_blocks,
                      pl.BlockSpec((None, None, d, ff), lambda i, be, nu: (layer, be[i], 0, 0)),
                      pl.BlockSpec((None, None, d, ff), lambda i, be, nu: (layer, be[i], 0, 0)),
                      pl.BlockSpec((None, None, ff, d), lambda i, be, nu: (layer, be[i], 0, 0))],
            out_specs=row_blocks,
            scratch_shapes=[pltpu.VMEM((d, ff), bf16), pltpu.VMEM((d, ff), bf16), pltpu.VMEM((ff, d), bf16)],
        ),
        compiler_params=_cparams(("arbitrary",)),
        name="experts",
    )(block_e, n_used, buf, w_gate, w_up, w_down)


def _combine_kernel(dest_ref, y_ref, gcol_ref, x_ref, mod_ref, fg_ref, o_ref, ybuf, sem, *, final):
    tm = dest_ref.shape[1]

    def issue(j, carry):
        _row_copy(y_ref, dest_ref[0, j], ybuf.at[0], j, sem).start()
        _row_copy(y_ref, dest_ref[1, j], ybuf.at[1], j, sem).start()
        return carry

    lax.fori_loop(0, tm, issue, 0, unroll=ROW_DMA_UNROLL)

    def drain(j, carry):
        _row_copy(y_ref, 0, ybuf.at[0], 0, sem).wait()
        _row_copy(y_ref, 0, ybuf.at[1], 0, sem).wait()
        return carry

    lax.fori_loop(0, tm, drain, 0, unroll=ROW_DMA_UNROLL)

    gc = gcol_ref[...]
    ffn = gc[:, 0:1] * _unpack_rows(ybuf[0]) + gc[:, 1:2] * _unpack_rows(ybuf[1])
    x = x_ref[...] + mod_ref[...][5:6] * ffn
    if final:
        x = x * lax.rsqrt(jnp.mean(x * x, axis=-1, keepdims=True) + NORM_EPS) * fg_ref[...]
    o_ref[...] = x


def _combine(dest_tiles, tile_off, y, gcol, xnew, mod_l, final_g, *, tiles_per_row, fixed_row, final):
    tm = dest_tiles.shape[2]
    t = xnew.shape[0]
    if fixed_row is None:
        mod_map = lambda i: (i // tiles_per_row, 0, 0)
    else:
        mod_map = lambda i: (fixed_row, 0, 0)
    return pl.pallas_call(
        functools.partial(_combine_kernel, final=final),
        out_shape=jax.ShapeDtypeStruct((t, D_MODEL), f32),
        grid=(t // tm,),
        in_specs=[pl.BlockSpec((None, TOP_K, tm), lambda i: (i + tile_off, 0, 0), memory_space=pltpu.SMEM),
                  pl.BlockSpec(memory_space=pl.ANY),
                  pl.BlockSpec((tm, LANES), lambda i: (i + tile_off, 0)),
                  pl.BlockSpec((tm, D_MODEL), lambda i: (i, 0)),
                  pl.BlockSpec((None, 6, D_MODEL), mod_map),
                  pl.BlockSpec((1, D_MODEL), lambda i: (0, 0))],
        out_specs=pl.BlockSpec((tm, D_MODEL), lambda i: (i, 0)),
        scratch_shapes=[pltpu.VMEM((TOP_K, tm, PACKED_COLS), u32), pltpu.SemaphoreType.DMA],
        compiler_params=_cparams(("arbitrary",)),
        name="combine",
    )(dest_tiles, y, gcol, xnew, mod_l, final_g)


def _moe(h2_parts, logits, router_bias_col, su, w_gate, w_up, w_down, layer, buf):
    t = logits.shape[0]
    ids, rank, gcol, cnt = _route(logits, router_bias_col, su)
    counts = cnt[:, 0].astype(i32)
    padded = (counts + EXPERT_BLOCK - 1) // EXPERT_BLOCK * EXPERT_BLOCK
    pend = jnp.cumsum(padded)
    pstart = pend - padded
    nb_needed = (t * TOP_K) // EXPERT_BLOCK + N_EXPERTS
    if buf is None:
        buf = jnp.zeros((nb_needed * EXPERT_BLOCK, PACKED_COLS), u32)
    nb = buf.shape[0] // EXPERT_BLOCK
    assert nb >= nb_needed
    block_pos = jnp.arange(nb, dtype=i32) * EXPERT_BLOCK
    block_e = jnp.minimum(jnp.sum((pend[None, :] <= block_pos[:, None]).astype(i32), axis=1), N_EXPERTS - 1)
    n_used = (pend[-1:] // EXPERT_BLOCK).astype(i32)
    dest_tiles = _dest(ids, rank, pstart.astype(f32)[:, None])
    tile_off = 0
    for part in h2_parts:
        buf = _dispatch(dest_tiles, tile_off, part, buf)
        tile_off += part.shape[0] // MOE_TILE
    y = _experts(block_e, n_used, buf, w_gate, w_up, w_down, layer)
    return dest_tiles, gcol, y, buf


def _pack_layer(l, w_in, conv_w, conv_b, dt_bias_f, dt_bias_b, a_log_f, a_log_b, d_skip, ssd_norm_g,
                gk_up_f, gk_bias_f, gk_up_b, gk_bias_b, gla_norm_g, w_out, norm1_g, norm2_g):
    w = w_in[l]
    z, xbc, dt, q, k, v, go, gk = jnp.split(w, [512, 1280, 1288, 1544, 1800, 2312, 2824], axis=1)
    small = jnp.concatenate([dt, gk, jnp.zeros((D_MODEL, SMALL_COLS - 24), f32)], axis=1)
    w_packed = jnp.concatenate([z, xbc, q, k, v, go, small], axis=1).astype(bf16)

    def lane_row(vec):
        return jnp.zeros((1, LANES), f32).at[0, :SSD_HEADS].set(vec)

    def sub_col(vec):
        return jnp.broadcast_to(vec[:, None], (SSD_HEADS, CHUNK)).astype(f32)

    def gup(m):
        return jnp.zeros((SMALL_COLS, GLA_QK), f32).at[SSD_HEADS:SSD_HEADS + GLA_GATE_RANK].set(m).astype(bf16)

    a_f = -jnp.exp(a_log_f[l])
    a_b = -jnp.exp(a_log_b[l])
    return dict(
        w_packed=w_packed,
        conv_w8=jnp.zeros((8, SSD_XBC), f32).at[:5].set(conv_w[l]),
        conv_b=conv_b[l][None, :],
        ssd_f=(lane_row(dt_bias_f[l]), lane_row(a_f), sub_col(dt_bias_f[l]), sub_col(a_f)),
        ssd_b=(lane_row(dt_bias_b[l]), lane_row(a_b), sub_col(dt_bias_b[l]), sub_col(a_b)),
        gla_f=(gup(gk_up_f[l]), gk_bias_f[l][None, :]),
        gla_b=(gup(gk_up_b[l]), gk_bias_b[l][None, :]),
        dsk=jnp.repeat(d_skip[l], SSD_HEAD_DIM)[None, :],
        sg=ssd_norm_g[l][None, :],
        gg=jnp.tile(gla_norm_g[l], GLA_HEADS)[None, :],
        wout=w_out[l].astype(bf16),
        n1g=norm1_g[l][None, :],
        n2g=norm2_g[l][None, :],
    )


def _constants():
    r = jnp.arange(LANES)[:, None]
    c = jnp.arange(SSD_WIDTH)[None, :]
    e_mat = ((c // SSD_HEAD_DIM) == r).astype(bf16)
    ssd_mask = ((r // SSD_STATE) == (c // (SSD_WIDTH // SSD_GROUPS))).astype(f32)
    rr = jnp.arange(GLA_WIDTH)[:, None]
    cc = jnp.arange(GLA_QK)[None, :]
    gla_mask = ((rr // GLA_VAL_DIM) == (cc // GLA_KEY_DIM)).astype(f32)
    k = jnp.arange(ROUTE_TILE)
    su = (k[:, None] < k[None, :]).astype(bf16)
    return e_mat, ssd_mask, gla_mask, su


def _mixers(streams, pk, consts, nb):
    e_mat, ssd_mask, gla_mask, _ = consts
    out = {}
    h0_f = jnp.zeros((nb, LANES, SSD_WIDTH), f32)
    h0_b = h0_f
    s0_f = jnp.zeros((nb, GLA_WIDTH, GLA_QK), f32)
    s0_b = s0_f
    for name in ('ctx', 'lat'):
        z, xs, bc, q, k, v, go, sm = streams[name]
        yf, yb, h0_f, h0_b = _ssd_scan(xs, bc, sm, h0_f, h0_b, pk['ssd_f'], pk['ssd_b'], e_mat, ssd_mask, nb=nb)
        of, ob, s0_f, s0_b = _gla_scan(q, k, v, sm, s0_f, s0_b, pk['gla_f'], pk['gla_b'], gla_mask, nb=nb)
        out[name] = (yf, yb, of, ob)
    return out


def kernel(x, c, ctx, c_ctx, w_mod, b_mod, norm1_g, norm2_g, w_in, conv_w, conv_b, dt_bias_f, dt_bias_b, a_log_f, a_log_b, d_skip, ssd_norm_g, gk_up_f, gk_bias_f, gk_up_b, gk_bias_b, gla_norm_g, w_out, w_router, router_bias, w_gate, w_up, w_down, final_norm_g):
    nb, seq, d = x.shape
    ctx_len = ctx.shape[1]
    depth = w_mod.shape[0]
    consts = _constants()
    su = consts[3]

    cvec = jnp.zeros((16, d), f32).at[:nb].set(c).at[nb].set(c_ctx)
    mod = _modulation(cvec, w_mod, b_mod).reshape(depth, 16, 6, d)
    ctx_row = nb

    perm = jnp.array([g * EXPERTS_PER_GROUP + m for m in range(EXPERTS_PER_GROUP)
                      for g in range(N_EXPERT_GROUPS)], dtype=i32)
    wr32 = jnp.zeros((d, LANES), f32).at[:, :N_EXPERTS].set(w_router[:, perm])
    wr_hi = wr32.astype(bf16)
    wr = jnp.stack([wr_hi, (wr32 - wr_hi.astype(f32)).astype(bf16)])
    bias_col = router_bias[perm][:, None]

    x2 = x.reshape(nb * seq, d)
    c2 = ctx.reshape(nb * ctx_len, d)
    lat_tiles = seq // TM_LAT
    moe_buf = None

    for l in range(depth):
        last = l == depth - 1
        pk = _pack_layer(l, w_in, conv_w, conv_b, dt_bias_f, dt_bias_b, a_log_f, a_log_b, d_skip, ssd_norm_g,
                         gk_up_f, gk_bias_f, gk_up_b, gk_bias_b, gla_norm_g, w_out, norm1_g, norm2_g)
        mod_l = mod[l]
        streams = {
            'ctx': _inproj(c2, mod_l, pk['n1g'], pk['w_packed'], pk['conv_w8'], pk['conv_b'],
                           tm=ctx_len, rowlen=ctx_len, tiles_per_row=1, fixed_row=ctx_row),
            'lat': _inproj(x2, mod_l, pk['n1g'], pk['w_packed'], pk['conv_w8'], pk['conv_b'],
                           tm=TM_LAT, rowlen=GRID_W, tiles_per_row=lat_tiles, fixed_row=None),
        }
        mix = _mixers(streams, pk, consts, nb)

        def merge(name, xres, tm, tiles_per_row, fixed_row):
            z, xs, bc, q, k, v, go, sm = streams[name]
            yf, yb, of, ob = mix[name]
            return _outproj(yf, yb, xs, z, of, ob, go, xres, mod_l, pk['dsk'], pk['sg'], pk['gg'],
                            pk['wout'], pk['n2g'], wr, tm=tm, tiles_per_row=tiles_per_row, fixed_row=fixed_row)

        xn_lat, h2_lat, lg_lat = merge('lat', x2, TM_LAT, lat_tiles, None)
        if last:
            parts, logits = [h2_lat], lg_lat
        else:
            xn_ctx, h2_ctx, lg_ctx = merge('ctx', c2, ctx_len, 1, ctx_row)
            parts, logits = [h2_ctx, h2_lat], jnp.concatenate([lg_ctx, lg_lat], axis=0)
        dest_tiles, gcol, y, moe_buf = _moe(parts, logits, bias_col, su, w_gate, w_up, w_down, l, moe_buf)
        lat_off = 0
        if not last:
            c2 = _combine(dest_tiles, 0, y, gcol, xn_ctx, mod_l, final_norm_g[None, :],
                          tiles_per_row=1, fixed_row=ctx_row, final=False)
            lat_off = xn_ctx.shape[0] // MOE_TILE
        x2 = _combine(dest_tiles, lat_off, y, gcol, xn_lat, mod_l, final_norm_g[None, :],
                      tiles_per_row=seq // MOE_TILE, fixed_row=None, final=last)
    return x2.reshape(nb, seq, d)
```

```python
import functools

import jax
import jax.numpy as jnp
from jax import lax
from jax.experimental import pallas as pl
from jax.experimental.pallas import tpu as pltpu

f32 = jnp.float32
bf16 = jnp.bfloat16
i32 = jnp.int32
u32 = jnp.uint32

D_MODEL = 1024
SSD_HEADS = 8
SSD_HEAD_DIM = 64
SSD_WIDTH = 512
SSD_GROUPS = 2
SSD_STATE = 64
SSD_XBC = 768
GLA_HEADS = 4
GLA_KEY_DIM = 64
GLA_VAL_DIM = 128
GLA_QK = 256
GLA_WIDTH = 512
GLA_GATE_RANK = 16
GLA_GATE_NORM = 16.0
GRID_W = 64
N_EXPERTS = 32
N_EXPERT_GROUPS = 8
EXPERTS_PER_GROUP = 4
TOP_K = 2
EXPERT_FF = 512
NORM_EPS = 1e-6

LANES = 128
CHUNK = 128
CHUNKS_PER_STEP = 4
INTERLEAVE_WAVE = 8
ROW_SPLIT = 2
SSD_SPLIT = 3
GLA_SPLIT = 2
MAIN_COLS = 2816
SMALL_COLS = LANES
TM_LAT = 512
ROUTE_TILE = 512
MOE_TILE = 512
ROW_DMA_UNROLL = 8
PACKED_COLS = D_MODEL // 2
EXPERT_BLOCK = 512
VMEM_LIMIT = 48 * 1024 * 1024

_HI = lax.Precision.HIGHEST
_NT = (((1,), (1,)), ((), ()))


def _dot(a, b, precision=None):
    return jnp.dot(a, b, preferred_element_type=f32, precision=precision)


def _silu(x):
    return x * jax.nn.sigmoid(x)


def _row_spec(rows, block_index):
    return pl.BlockSpec((rows, PACKED_COLS), lambda *idx: (block_index(*idx), 0))


def _pack_rows(value):
    hi = pltpu.bitcast(value[:, :PACKED_COLS].astype(bf16).astype(f32), u32)
    lo = pltpu.bitcast(value[:, PACKED_COLS:].astype(bf16).astype(f32), u32)
    return hi | (lo >> 16)


def _unpack_rows(packed):
    hi = pltpu.bitcast(packed & jnp.uint32(0xFFFF0000), f32)
    lo = pltpu.bitcast(packed << 16, f32)
    return jnp.concatenate([hi, lo], axis=1)


def _cparams(sem):
    return pltpu.CompilerParams(dimension_semantics=sem, vmem_limit_bytes=VMEM_LIMIT)


def _mod_kernel(c_ref, w_ref, b_ref, o_ref):
    sc = _silu(c_ref[...]).astype(bf16)
    o_ref[...] = _dot(sc, w_ref[...].astype(bf16)) + b_ref[...]


def _modulation(cvec, w_mod, b_mod):
    depth, d, n = w_mod.shape
    tn = 1536
    return pl.pallas_call(
        _mod_kernel,
        out_shape=jax.ShapeDtypeStruct((depth, 16, n), f32),
        grid=(depth, n // tn),
        in_specs=[pl.BlockSpec((16, d), lambda l, j: (0, 0)),
                  pl.BlockSpec((None, d, tn), lambda l, j: (l, 0, j)),
                  pl.BlockSpec((None, 1, tn), lambda l, j: (l, 0, j))],
        out_specs=pl.BlockSpec((None, 16, tn), lambda l, j: (l, 0, j)),
        compiler_params=_cparams(("arbitrary", "arbitrary")),
        name="modulation",
    )(cvec, w_mod, b_mod.reshape(depth, 1, n))


def _inproj_kernel(x_ref, mod_ref, g_ref, w_ref, cw_ref, cb_ref,
                   z_ref, xs_ref, bc_ref, q_ref, k_ref, v_ref, go_ref, sm_ref, *, rowlen):
    tm = x_ref.shape[0]
    split = ROW_SPLIT if (tm // ROW_SPLIT) % rowlen == 0 else 1
    sub = tm // split
    m = mod_ref[...]
    geff = g_ref[...] * (1.0 + m[1:2])
    cw = cw_ref[...]

    def rows_gen(rows):
        x = x_ref[rows, :]
        inv = lax.rsqrt(jnp.mean(x * x, axis=-1, keepdims=True) + NORM_EPS)
        h = ((x * inv) * geff + m[0:1]).astype(bf16)
        yield

        def proj(lo, hi):
            return _dot(h, w_ref[:, lo:hi])

        z_ref[rows, :] = proj(0, 512).astype(bf16)
        yield

        xbc = proj(512, 1280)
        yield
        pos = lax.broadcasted_iota(i32, xbc.shape, 0) & (rowlen - 1)
        acc = xbc * cw[2:3]
        for d in (-2, -1, 1, 2):
            shifted = pltpu.roll(xbc, (-d) % sub, 0)
            valid = (pos >= -d) if d < 0 else (pos <= rowlen - 1 - d)
            acc = acc + jnp.where(valid, shifted, 0.0) * cw[2 + d:3 + d]
        yield
        act = _silu(acc + cb_ref[...])
        xs_ref[rows, :] = act[:, :SSD_WIDTH].astype(bf16)
        bc_ref[rows, :] = act[:, SSD_WIDTH:].astype(bf16)
        yield

        qk = proj(1280, 1792)
        q_ref[rows, :] = qk[:, :GLA_QK].astype(bf16)
        k_ref[rows, :] = qk[:, GLA_QK:].astype(bf16)
        yield
        v_ref[rows, :] = proj(1792, 2304).astype(bf16)
        yield
        go_ref[rows, :] = proj(2304, 2816).astype(bf16)
        yield
        sm_ref[rows, :] = proj(2816, 2944)

    _run_interleaved(rows_gen(pl.ds(r * sub, sub)) for r in range(split))


def _inproj(x2d, mod_l, norm_g, w_packed, conv_w8, conv_b, *, tm, rowlen, tiles_per_row, fixed_row):
    t = x2d.shape[0]
    nt = t // tm
    if fixed_row is None:
        mod_map = lambda i: (i // tiles_per_row, 0, 0)
    else:
        mod_map = lambda i: (fixed_row, 0, 0)
    tok = lambda w: pl.BlockSpec((tm, w), lambda i: (i, 0))
    const = lambda a: pl.BlockSpec(a.shape, lambda i: (0,) * a.ndim)
    widths = (512, 512, 256, 256, 256, 512, 512)
    return pl.pallas_call(
        functools.partial(_inproj_kernel, rowlen=rowlen),
        out_shape=tuple(jax.ShapeDtypeStruct((t, w), bf16) for w in widths)
        + (jax.ShapeDtypeStruct((t, SMALL_COLS), f32),),
        grid=(nt,),
        in_specs=[tok(D_MODEL), pl.BlockSpec((None, 6, D_MODEL), mod_map), const(norm_g),
                  const(w_packed), const(conv_w8), const(conv_b)],
        out_specs=tuple(tok(w) for w in widths) + (tok(SMALL_COLS),),
        compiler_params=_cparams(("arbitrary",)),
        name="inproj",
    )(x2d, mod_l, norm_g, w_packed, conv_w8, conv_b)


def _dot_split(a, b, parts, *, split_lhs):
    rest = a if split_lhs else b
    acc = None
    for _ in range(parts):
        piece = rest.astype(bf16)
        rest = rest - piece.astype(f32)
        term = _dot(piece, b) if split_lhs else _dot(a, piece)
        acc = term if acc is None else acc + term
    return acc


def _tri_masks(q):
    r = lax.broadcasted_iota(i32, (q, q), 0)
    c = lax.broadcasted_iota(i32, (q, q), 1)
    return r >= c, r <= c


def _ssd_chunk(load, store, carry, par, e, bm, rev):
    xb, bcv, sm = load()
    dtb, a_lane, dtbt, a_sub = par
    q = xb.shape[0]
    dt_col = jax.nn.softplus(sm + dtb)
    a_col = dt_col * a_lane
    dt_row = jax.nn.softplus(sm.T[0:SSD_HEADS, :] + dtbt)
    a_row = dt_row * a_sub
    yield
    lower, upper = _tri_masks(q)
    lo_b = jnp.where(lower, 1.0, 0.0).astype(bf16)
    up_b = jnp.where(upper, 1.0, 0.0).astype(bf16)
    if not rev:
        cs_col = _dot_split(lo_b, a_col, SSD_SPLIT, split_lhs=False)
        cs_row = _dot_split(a_row, up_b, SSD_SPLIT, split_lhs=True)
        mask = lower
        a_tot = cs_col[q - 1:q, :]
    else:
        cs_col = _dot_split(up_b, a_col, SSD_SPLIT, split_lhs=False)
        cs_row = _dot_split(a_row, lo_b, SSD_SPLIT, split_lhs=True)
        mask = upper
        a_tot = cs_col[0:1, :]
    yield

    w_exp = _dot((jnp.exp(a_tot - cs_col) * dt_col).astype(bf16), e)
    ecs_exp = _dot(jnp.exp(cs_col).astype(bf16), e)
    dec_exp = _dot_split(jnp.broadcast_to(jnp.exp(a_tot), (8, LANES)), e, SSD_SPLIT, split_lhs=True)[0:1]
    yield

    xw = (xb.astype(f32) * w_exp).astype(bf16)
    b_all = bcv[:, 0:LANES]
    c_all = bcv[:, LANES:2 * LANES]
    b_t = b_all.astype(f32).T.astype(bf16)
    yield
    s_upd = _dot(b_t, xw)
    yield

    lane = lax.broadcasted_iota(i32, (q, LANES), 1)
    zero_b = jnp.zeros((q, LANES), bf16)
    ys = []
    for g in range(SSD_GROUPS):
        in_g = (lane >= SSD_STATE * g) & (lane < SSD_STATE * (g + 1))
        cb = _dot(jnp.where(in_g, c_all, zero_b), b_t)
        yield
        for pp in range(2):
            h0 = 4 * g + 2 * pp
            ms = []
            for h in (h0, h0 + 1):
                seg = cs_col[:, h:h + 1] - cs_row[h:h + 1, :]
                dec = jnp.exp(jnp.where(mask, seg, -1e30))
                ms.append((cb * dec * dt_row[h:h + 1, :]).astype(bf16))
            xp = xb[:, h0 * SSD_HEAD_DIM:h0 * SSD_HEAD_DIM + LANES]
            rhs = jnp.concatenate([jnp.where(lane < SSD_HEAD_DIM, xp, zero_b),
                                   jnp.where(lane >= SSD_HEAD_DIM, xp, zero_b)], axis=0)
            yield
            ys.append(_dot(jnp.concatenate(ms, axis=1), rhs))
            yield
    y_intra = jnp.concatenate(ys, axis=1)
    yield

    s_old = carry['s']
    carry['s'] = (s_old * dec_exp + s_upd) * bm
    store((y_intra + _dot(c_all, s_old.astype(bf16)) * ecs_exp).astype(bf16))


def _run_interleaved(gens):
    gens = list(gens)
    done = object()
    for w in range(0, len(gens), INTERLEAVE_WAVE):
        live = gens[w:w + INTERLEAVE_WAVE]
        while live:
            live = [g for g in live if next(g, done) is not done]


def _chunk_rows(cps):
    return [(pl.ds(ci * CHUNK, CHUNK), pl.ds((cps - 1 - ci) * CHUNK, CHUNK)) for ci in range(cps)]


def _ssd_kernel(xf_ref, bcf_ref, smf_ref, xr_ref, bcr_ref, smr_ref, h0f_ref, h0r_ref,
                dtbf_ref, af_ref, dtbtf_ref, atf_ref, dtbr_ref, ar_ref, dtbtr_ref, atr_ref, e_ref, bm_ref,
                yf_ref, yr_ref, hff_ref, hfr_ref, st_f, st_r, *, cps, nblk):
    j = pl.program_id(1)

    @pl.when(j == 0)
    def _():
        st_f[...] = h0f_ref[...]
        st_r[...] = h0r_ref[...]

    e = e_ref[...]
    bm = bm_ref[...]
    par_f = (dtbf_ref[...], af_ref[...], dtbtf_ref[...], atf_ref[...])
    par_r = (dtbr_ref[...], ar_ref[...], dtbtr_ref[...], atr_ref[...])
    carry_f = {'s': st_f[...]}
    carry_r = {'s': st_r[...]}

    def chunk(refs, out_ref, rows, carry, par, rev):
        def store(y):
            out_ref[rows, :] = y
        return _ssd_chunk(lambda: tuple(r[rows, :] for r in refs), store, carry, par, e, bm, rev)

    gens = []
    for rows_f, rows_r in _chunk_rows(cps):
        gens.append(chunk((xf_ref, bcf_ref, smf_ref), yf_ref, rows_f, carry_f, par_f, False))
        gens.append(chunk((xr_ref, bcr_ref, smr_ref), yr_ref, rows_r, carry_r, par_r, True))
    _run_interleaved(gens)
    st_f[...] = carry_f['s']
    st_r[...] = carry_r['s']

    @pl.when(j == nblk - 1)
    def _():
        hff_ref[...] = carry_f['s']
        hfr_ref[...] = carry_r['s']


def _scan_specs(t, nb):
    per_row = t // nb
    cps = min(CHUNKS_PER_STEP, per_row // CHUNK)
    nblk = per_row // (cps * CHUNK)
    fmap = lambda b, j: (b * nblk + j, 0)
    rmap = lambda b, j: (b * nblk + nblk - 1 - j, 0)
    return cps, nblk, fmap, rmap


def _ssd_scan(xs, bc, sm, h0_f, h0_r, par_f, par_r, e_mat, bmask, *, nb):
    t = xs.shape[0]
    cps, nblk, fmap, rmap = _scan_specs(t, nb)
    blk = cps * CHUNK
    const = lambda a: pl.BlockSpec(a.shape, lambda b, j: (0,) * a.ndim)
    st = pl.BlockSpec((None, LANES, SSD_WIDTH), lambda b, j: (b, 0, 0))
    widths = (SSD_WIDTH, 2 * LANES, SMALL_COLS)
    tok_in = [pl.BlockSpec((blk, w), m) for m in (fmap, rmap) for w in widths]
    consts = (*par_f, *par_r, e_mat, bmask)
    return pl.pallas_call(
        functools.partial(_ssd_kernel, cps=cps, nblk=nblk),
        out_shape=(jax.ShapeDtypeStruct((t, SSD_WIDTH), bf16), jax.ShapeDtypeStruct((t, SSD_WIDTH), bf16),
                   jax.ShapeDtypeStruct((nb, LANES, SSD_WIDTH), f32),
                   jax.ShapeDtypeStruct((nb, LANES, SSD_WIDTH), f32)),
        grid=(nb, nblk),
        in_specs=tok_in + [st, st] + [const(a) for a in consts],
        out_specs=(pl.BlockSpec((blk, SSD_WIDTH), fmap), pl.BlockSpec((blk, SSD_WIDTH), rmap), st, st),
        scratch_shapes=[pltpu.VMEM((LANES, SSD_WIDTH), f32), pltpu.VMEM((LANES, SSD_WIDTH), f32)],
        compiler_params=_cparams(("arbitrary", "arbitrary")),
        name="ssd",
    )(xs, bc, sm, xs, bc, sm, h0_f, h0_r, *consts)


def _gla_chunk(load, store, carry, par, bm, rev):
    qb, kb, vb, sm = load()
    gup, gbias = par
    q = qb.shape[0]
    gp = _dot(sm.astype(bf16), gup) + gbias
    yield
    g = jax.nn.log_sigmoid(gp) * (1.0 / GLA_GATE_NORM)
    yield
    lower, upper = _tri_masks(q)
    if not rev:
        b = _dot_split(jnp.where(lower, 1.0, 0.0).astype(bf16), g, GLA_SPLIT, split_lhs=False)
        mask, mid = lower, q // 2 - 1
        b_tot = b[q - 1:q, :]
    else:
        b = _dot_split(jnp.where(upper, 1.0, 0.0).astype(bf16), g, GLA_SPLIT, split_lhs=False)
        mask, mid = upper, q // 2
        b_tot = b[0:1, :]
    b_mid = b[mid:mid + 1, :]
    yield

    qf = qb.astype(f32) * (GLA_KEY_DIM ** -0.5)
    kf = kb.astype(f32)
    qd = (qf * jnp.exp(b - b_mid)).astype(bf16)
    yield
    ki = (kf * jnp.exp(b_mid - b)).astype(bf16)
    yield
    q_st = (qf * jnp.exp(b)).astype(bf16)
    yield
    k_end = (kf * jnp.exp(b_tot - b)).astype(bf16)
    dec = jnp.exp(b_tot)
    yield
    v_t = vb.astype(f32).T.astype(bf16)
    yield
    s_upd = _dot(v_t, k_end)
    yield

    lane = lax.broadcasted_iota(i32, (q, LANES), 1)
    zero_b = jnp.zeros((q, LANES), bf16)
    outs = []
    for h in range(GLA_HEADS):
        p, hh = divmod(h, 2)
        qp = qd[:, p * LANES:(p + 1) * LANES]
        kp = ki[:, p * LANES:(p + 1) * LANES]
        in_h = (lane >= GLA_KEY_DIM * hh) & (lane < GLA_KEY_DIM * (hh + 1))
        s = lax.dot_general(jnp.where(in_h, qp, zero_b), kp, _NT, preferred_element_type=f32)
        yield
        attn = jnp.where(mask, s, 0.0).astype(bf16)
        outs.append(_dot(attn, vb[:, h * GLA_VAL_DIM:(h + 1) * GLA_VAL_DIM]))
        yield
    o_intra = jnp.concatenate(outs, axis=1)
    yield

    s_old = carry['s']
    carry['s'] = (s_old * dec + s_upd) * bm
    o_inter = lax.dot_general(q_st, s_old.astype(bf16), _NT, preferred_element_type=f32)
    store((o_intra + o_inter).astype(bf16))


def _gla_kernel(qf_ref, kf_ref, vf_ref, smf_ref, qr_ref, kr_ref, vr_ref, smr_ref, s0f_ref, s0r_ref,
                gupf_ref, gbf_ref, gupr_ref, gbr_ref, bm_ref,
                of_ref, or_ref, sff_ref, sfr_ref, st_f, st_r, *, cps, nblk):
    j = pl.program_id(1)

    @pl.when(j == 0)
    def _():
        st_f[...] = s0f_ref[...]
        st_r[...] = s0r_ref[...]

    bm = bm_ref[...]
    par_f = (gupf_ref[...], gbf_ref[...])
    par_r = (gupr_ref[...], gbr_ref[...])
    carry_f = {'s': st_f[...]}
    carry_r = {'s': st_r[...]}

    def chunk(refs, out_ref, rows, carry, par, rev):
        def store(o):
            out_ref[rows, :] = o
        return _gla_chunk(lambda: tuple(r[rows, :] for r in refs), store, carry, par, bm, rev)

    gens = []
    for rows_f, rows_r in _chunk_rows(cps):
        gens.append(chunk((qf_ref, kf_ref, vf_ref, smf_ref), of_ref, rows_f, carry_f, par_f, False))
        gens.append(chunk((qr_ref, kr_ref, vr_ref, smr_ref), or_ref, rows_r, carry_r, par_r, True))
    _run_interleaved(gens)
    st_f[...] = carry_f['s']
    st_r[...] = carry_r['s']

    @pl.when(j == nblk - 1)
    def _():
        sff_ref[...] = carry_f['s']
        sfr_ref[...] = carry_r['s']


def _gla_scan(qa, ka, va, sm, s0_f, s0_r, par_f, par_r, bmask, *, nb):
    t = qa.shape[0]
    cps, nblk, fmap, rmap = _scan_specs(t, nb)
    blk = cps * CHUNK
    const = lambda a: pl.BlockSpec(a.shape, lambda b, j: (0,) * a.ndim)
    st = pl.BlockSpec((None, GLA_WIDTH, GLA_QK), lambda b, j: (b, 0, 0))
    widths = (GLA_QK, GLA_QK, GLA_WIDTH, SMALL_COLS)
    tok_in = [pl.BlockSpec((blk, w), m) for m in (fmap, rmap) for w in widths]
    consts = (*par_f, *par_r, bmask)
    return pl.pallas_call(
        functools.partial(_gla_kernel, cps=cps, nblk=nblk),
        out_shape=(jax.ShapeDtypeStruct((t, GLA_WIDTH), bf16), jax.ShapeDtypeStruct((t, GLA_WIDTH), bf16),
                   jax.ShapeDtypeStruct((nb, GLA_WIDTH, GLA_QK), f32),
                   jax.ShapeDtypeStruct((nb, GLA_WIDTH, GLA_QK), f32)),
        grid=(nb, nblk),
        in_specs=tok_in + [st, st] + [const(a) for a in consts],
        out_specs=(pl.BlockSpec((blk, GLA_WIDTH), fmap), pl.BlockSpec((blk, GLA_WIDTH), rmap), st, st),
        scratch_shapes=[pltpu.VMEM((GLA_WIDTH, GLA_QK), f32), pltpu.VMEM((GLA_WIDTH, GLA_QK), f32)],
        compiler_params=_cparams(("arbitrary", "arbitrary")),
        name="gla",
    )(qa, ka, va, sm, qa, ka, va, sm, s0_f, s0_r, *consts)


def _outproj_kernel(yf_ref, yb_ref, xs_ref, z_ref, of_ref, ob_ref, go_ref, xres_ref, mod_ref,
                    dsk_ref, sg_ref, gg_ref, wout_ref, n2g_ref, wr_ref,
                    xnew_ref, h2_ref, lg_ref):
    tm = xres_ref.shape[0]
    sub = tm // ROW_SPLIT
    m = mod_ref[...]
    h2_gain = n2g_ref[...] * (1.0 + m[4:5])

    def rows_gen(rows):
        up = lambda r: r[rows, :].astype(f32)
        y = up(yf_ref) + up(yb_ref) + dsk_ref[...] * up(xs_ref)
        yield
        y = y * _silu(up(z_ref))
        yield
        y = y * lax.rsqrt(jnp.mean(y * y, axis=-1, keepdims=True) + NORM_EPS) * sg_ref[...]
        yield
        o = up(of_ref) + up(ob_ref)
        parts = []
        for h in range(GLA_HEADS):
            oh = o[:, h * GLA_VAL_DIM:(h + 1) * GLA_VAL_DIM]
            parts.append(oh * lax.rsqrt(jnp.mean(oh * oh, axis=-1, keepdims=True) + NORM_EPS))
        yield
        o = jnp.concatenate(parts, axis=1) * gg_ref[...] * _silu(up(go_ref))
        yield
        cat = jnp.concatenate([y, o], axis=1).astype(bf16)
        mix = _dot(cat, wout_ref[...])
        yield
        xn = xres_ref[rows, :] + m[2:3] * mix
        xnew_ref[rows, :] = xn
        yield
        inv = lax.rsqrt(jnp.mean(xn * xn, axis=-1, keepdims=True) + NORM_EPS)
        h2 = (xn * inv) * h2_gain + m[3:4]
        h2_ref[rows, :] = _pack_rows(h2)
        yield
        h2_hi = h2.astype(bf16)
        h2_lo = (h2 - h2_hi.astype(f32)).astype(bf16)
        wr_hi = wr_ref[0]
        lg_ref[rows, :] = _dot(h2_hi, wr_hi) + _dot(h2_hi, wr_ref[1]) + _dot(h2_lo, wr_hi)

    _run_interleaved(rows_gen(pl.ds(r * sub, sub)) for r in range(ROW_SPLIT))


def _outproj(yf, yb, xs, z, of, ob, go, xres, mod_l, dsk, sg, gg, wout, n2g, wr,
             *, tm, tiles_per_row, fixed_row):
    t = xres.shape[0]
    if fixed_row is None:
        mod_map = lambda i: (i // tiles_per_row, 0, 0)
    else:
        mod_map = lambda i: (fixed_row, 0, 0)
    tok = lambda w: pl.BlockSpec((tm, w), lambda i: (i, 0))
    const = lambda a: pl.BlockSpec(a.shape, lambda i: (0,) * a.ndim)
    return pl.pallas_call(
        _outproj_kernel,
        out_shape=(jax.ShapeDtypeStruct((t, D_MODEL), f32), jax.ShapeDtypeStruct((t, PACKED_COLS), u32),
                   jax.ShapeDtypeStruct((t, LANES), f32)),
        grid=(t // tm,),
        in_specs=[tok(512)] * 7 + [tok(D_MODEL), pl.BlockSpec((None, 6, D_MODEL), mod_map),
                                   const(dsk), const(sg), const(gg), const(wout), const(n2g), const(wr)],
        out_specs=(tok(D_MODEL), _row_spec(tm, lambda i: i), tok(LANES)),
        compiler_params=_cparams(("arbitrary",)),
        name="outproj",
    )(yf, yb, xs, z, of, ob, go, xres, mod_l, dsk, sg, gg, wout, n2g, wr)


def _route_kernel(lg_ref, bias_ref, su_ref, ids_ref, rank_ref, gcol_ref, cnt_ref, carry):
    i = pl.program_id(0)
    tm = lg_ref.shape[0]

    @pl.when(i == 0)
    def _():
        carry[...] = jnp.zeros_like(carry)

    s = jax.nn.sigmoid(lg_ref[...].T[0:N_EXPERTS, :])
    sel = s + bias_ref[...]
    a = [sel[8 * m:8 * (m + 1)] for m in range(EXPERTS_PER_GROUP)]
    sv = [s[8 * m:8 * (m + 1)] for m in range(EXPERTS_PER_GROUP)]
    hi01, lo01 = jnp.maximum(a[0], a[1]), jnp.minimum(a[0], a[1])
    hi23, lo23 = jnp.maximum(a[2], a[3]), jnp.minimum(a[2], a[3])
    gscore = jnp.maximum(hi01, hi23) + jnp.maximum(jnp.minimum(hi01, hi23), jnp.maximum(lo01, lo23))
    giota = lax.broadcasted_iota(i32, gscore.shape, 0)
    gmax = jnp.max(gscore, axis=0, keepdims=True)
    gidx = jnp.min(jnp.where(gscore == gmax, giota, N_EXPERT_GROUPS), axis=0, keepdims=True)
    pick = giota == gidx
    v = [jnp.sum(jnp.where(pick, a[m], 0.0), axis=0, keepdims=True) for m in range(4)]
    w = [jnp.sum(jnp.where(pick, sv[m], 0.0), axis=0, keepdims=True) for m in range(4)]

    def first_max(vals, excluded):
        best = vals[0]
        for m in range(1, 4):
            best = jnp.maximum(best, vals[m])
        idx = jnp.full(best.shape, 3, i32)
        for m in (2, 1, 0):
            hit = vals[m] == best
            if excluded is not None:
                hit = hit & (excluded != m)
            idx = jnp.where(hit, m, idx)
        return idx

    i1 = first_max(v, None)
    v_rest = [jnp.where(i1 == m, -jnp.inf, v[m]) for m in range(4)]
    i2 = first_max(v_rest, i1)

    def take(vals, idx):
        out = vals[3]
        for m in (2, 1, 0):
            out = jnp.where(idx == m, vals[m], out)
        return out

    w1, w2 = take(w, i1), take(w, i2)
    denom = w1 + w2
    id1 = gidx * EXPERTS_PER_GROUP + i1
    id2 = gidx * EXPERTS_PER_GROUP + i2
    row2 = lax.broadcasted_iota(i32, (TOP_K, tm), 0)
    ids_ref[...] = jnp.where(row2 == 0, id1, id2)

    eiota = lax.broadcasted_iota(i32, (N_EXPERTS, tm), 0)
    hit1, hit2 = eiota == id1, eiota == id2
    onehot = jnp.where(hit1, 1.0, 0.0) + jnp.where(hit2, 1.0, 0.0)
    before = _dot(onehot.astype(bf16), su_ref[...]) + carry[...]
    r1 = jnp.sum(jnp.where(hit1, before, 0.0), axis=0, keepdims=True)
    r2 = jnp.sum(jnp.where(hit2, before, 0.0), axis=0, keepdims=True)
    rank_ref[...] = jnp.where(row2 == 0, r1, r2).astype(i32)
    new_carry = carry[...] + jnp.sum(onehot, axis=1, keepdims=True)
    carry[...] = new_carry
    cnt_ref[...] = jnp.broadcast_to(new_carry, cnt_ref.shape)

    rows = lax.broadcasted_iota(i32, (LANES, tm), 0)
    gates = jnp.where(rows == 0, w1 / denom, jnp.where(rows == 1, w2 / denom, 0.0))
    gcol_ref[...] = gates.T


def _route(logits, bias_col, su):
    t = logits.shape[0]
    tm = ROUTE_TILE
    return pl.pallas_call(
        _route_kernel,
        out_shape=(jax.ShapeDtypeStruct((2, t), i32), jax.ShapeDtypeStruct((2, t), i32),
                   jax.ShapeDtypeStruct((t, LANES), f32), jax.ShapeDtypeStruct((N_EXPERTS, LANES), f32)),
        grid=(t // tm,),
        in_specs=[pl.BlockSpec((tm, LANES), lambda i: (i, 0)),
                  pl.BlockSpec((N_EXPERTS, 1), lambda i: (0, 0)),
                  pl.BlockSpec((tm, tm), lambda i: (0, 0))],
        out_specs=(pl.BlockSpec((2, tm), lambda i: (0, i)), pl.BlockSpec((2, tm), lambda i: (0, i)),
                   pl.BlockSpec((tm, LANES), lambda i: (i, 0)),
                   pl.BlockSpec((N_EXPERTS, LANES), lambda i: (0, 0))),
        scratch_shapes=[pltpu.VMEM((N_EXPERTS, 1), f32)],
        compiler_params=_cparams(("arbitrary",)),
        name="route",
    )(logits, bias_col, su)


def _row_copy(src_ref, src_row, dst_ref, dst_row, sem):
    return pltpu.make_async_copy(src_ref.at[pl.ds(src_row, 1)], dst_ref.at[pl.ds(dst_row, 1)], sem)


def _dest_kernel(ids_ref, rank_ref, pstart_ref, dest_ref):
    ids = ids_ref[...]
    tm = ids.shape[1]
    eiota = lax.broadcasted_iota(i32, (N_EXPERTS, tm), 0)
    ps = pstart_ref[...]
    rows = [jnp.sum(jnp.where(eiota == ids[k:k + 1, :], ps, 0.0), axis=0, keepdims=True) for k in range(TOP_K)]
    row2 = lax.broadcasted_iota(i32, (TOP_K, tm), 0)
    dest_ref[...] = jnp.where(row2 == 0, rows[0], rows[1]).astype(i32) + rank_ref[...]


def _dest(ids, rank, pstart_col):
    t = ids.shape[1]
    tm = MOE_TILE
    return pl.pallas_call(
        _dest_kernel,
        out_shape=jax.ShapeDtypeStruct((t // tm, TOP_K, tm), i32),
        grid=(t // tm,),
        in_specs=[pl.BlockSpec((TOP_K, tm), lambda i: (0, i)), pl.BlockSpec((TOP_K, tm), lambda i: (0, i)),
                  pl.BlockSpec((N_EXPERTS, 1), lambda i: (0, 0))],
        out_specs=pl.BlockSpec((None, TOP_K, tm), lambda i: (i, 0, 0)),
        compiler_params=_cparams(("arbitrary",)),
        name="dest",
    )(ids, rank, pstart_col)


def _dispatch_kernel(dest_ref, h_ref, buf_in_ref, buf_ref, sem):
    del buf_in_ref
    tm = dest_ref.shape[1]

    def issue(j, carry):
        _row_copy(h_ref, j, buf_ref, dest_ref[0, j], sem).start(priority=0)
        _row_copy(h_ref, j, buf_ref, dest_ref[1, j], sem).start(priority=1)
        return carry

    lax.fori_loop(0, tm, issue, 0, unroll=ROW_DMA_UNROLL)

    def drain(j, carry):
        _row_copy(h_ref, 0, buf_ref, 0, sem).wait()
        _row_copy(h_ref, 0, buf_ref, 0, sem).wait()
        return carry

    lax.fori_loop(0, tm, drain, 0, unroll=ROW_DMA_UNROLL)


def _dispatch(dest_tiles, tile_off, h2, buf):
    tm = dest_tiles.shape[2]
    return pl.pallas_call(
        _dispatch_kernel,
        out_shape=jax.ShapeDtypeStruct(buf.shape, buf.dtype),
        grid=(h2.shape[0] // tm,),
        in_specs=[pl.BlockSpec((None, TOP_K, tm), lambda i: (i + tile_off, 0, 0), memory_space=pltpu.SMEM),
                  _row_spec(tm, lambda i: i), pl.BlockSpec(memory_space=pl.ANY)],
        out_specs=pl.BlockSpec(memory_space=pl.ANY),
        scratch_shapes=[pltpu.SemaphoreType.DMA],
        input_output_aliases={2: 0},
        compiler_params=_cparams(("arbitrary",)),
        name="dispatch",
    )(dest_tiles, h2, buf)


def _expert_kernel(be_ref, nu_ref, x_ref, wg_ref, wu_ref, wd_ref, y_ref, wg_b, wu_b, wd_b):
    i = pl.program_id(0)
    fresh = jnp.logical_or(i == 0, be_ref[i] != be_ref[jnp.maximum(i - 1, 0)])

    @pl.when(jnp.logical_and(fresh, i < nu_ref[0]))
    def _():
        wg_b[...] = wg_ref[...].astype(bf16)
        wu_b[...] = wu_ref[...].astype(bf16)
        wd_b[...] = wd_ref[...].astype(bf16)

    @pl.when(i < nu_ref[0])
    def _():
        sub = x_ref.shape[0] // ROW_SPLIT

        def rows_gen(rows):
            xb = _unpack_rows(x_ref[rows, :]).astype(bf16)
            yield
            g = _dot(xb, wg_b[...])
            yield
            u = _dot(xb, wu_b[...])
            yield
            hmid = (_silu(g) * u).astype(bf16)
            yield
            y_ref[rows, :] = _pack_rows(_dot(hmid, wd_b[...]))

        _run_interleaved(rows_gen(pl.ds(r * sub, sub)) for r in range(ROW_SPLIT))

    @pl.when(i >= nu_ref[0])
    def _():
        y_ref[...] = jnp.zeros_like(y_ref)


def _experts(block_e, n_used, buf, w_gate, w_up, w_down, layer):
    p = buf.shape[0]
    nb = p // EXPERT_BLOCK
    d, ff = w_gate.shape[-2:]
    row_blocks = _row_spec(EXPERT_BLOCK, lambda i, be, nu: i)
    return pl.pallas_call(
        _expert_kernel,
        out_shape=jax.ShapeDtypeStruct(buf.shape, u32),
        grid_spec=pltpu.PrefetchScalarGridSpec(
            num_scalar_prefetch=2,
            grid=(nb,),
            in_specs=[row_blocks,
                      pl.BlockSpec((None, None, d, ff), lambda i, be, nu: (layer, be[i], 0, 0)),
                      pl.BlockSpec((None, None, d, ff), lambda i, be, nu: (layer, be[i], 0, 0)),
                      pl.BlockSpec((None, None, ff, d), lambda i, be, nu: (layer, be[i], 0, 0))],
            out_specs=row_blocks,
            scratch_shapes=[pltpu.VMEM((d, ff), bf16), pltpu.VMEM((d, ff), bf16), pltpu.VMEM((ff, d), bf16)],
        ),
        compiler_params=_cparams(("arbitrary",)),
        name="experts",
    )(block_e, n_used, buf, w_gate, w_up, w_down)


def _combine_kernel(dest_ref, y_ref, gcol_ref, x_ref, mod_ref, fg_ref, o_ref, ybuf, sem, *, final):
    tm = dest_ref.shape[1]

    def issue(j, carry):
        _row_copy(y_ref, dest_ref[0, j], ybuf.at[0], j, sem).start(priority=0)
        _row_copy(y_ref, dest_ref[1, j], ybuf.at[1], j, sem).start(priority=1)
        return carry

    lax.fori_loop(0, tm, issue, 0, unroll=ROW_DMA_UNROLL)

    def drain(j, carry):
        _row_copy(y_ref, 0, ybuf.at[0], 0, sem).wait()
        _row_copy(y_ref, 0, ybuf.at[1], 0, sem).wait()
        return carry

    lax.fori_loop(0, tm, drain, 0, unroll=ROW_DMA_UNROLL)

    gc = gcol_ref[...]
    ffn = gc[:, 0:1] * _unpack_rows(ybuf[0]) + gc[:, 1:2] * _unpack_rows(ybuf[1])
    x = x_ref[...] + mod_ref[...][5:6] * ffn
    if final:
        x = x * lax.rsqrt(jnp.mean(x * x, axis=-1, keepdims=True) + NORM_EPS) * fg_ref[...]
    o_ref[...] = x


def _combine(dest_tiles, tile_off, y, gcol, xnew, mod_l, final_g, *, tiles_per_row, fixed_row, final):
    tm = dest_tiles.shape[2]
    t = xnew.shape[0]
    if fixed_row is None:
        mod_map = lambda i: (i // tiles_per_row, 0, 0)
    else:
        mod_map = lambda i: (fixed_row, 0, 0)
    return pl.pallas_call(
        functools.partial(_combine_kernel, final=final),
        out_shape=jax.ShapeDtypeStruct((t, D_MODEL), f32),
        grid=(t // tm,),
        in_specs=[pl.BlockSpec((None, TOP_K, tm), lambda i: (i + tile_off, 0, 0), memory_space=pltpu.SMEM),
                  pl.BlockSpec(memory_space=pl.ANY),
                  pl.BlockSpec((tm, LANES), lambda i: (i + tile_off, 0)),
                  pl.BlockSpec((tm, D_MODEL), lambda i: (i, 0)),
                  pl.BlockSpec((None, 6, D_MODEL), mod_map),
                  pl.BlockSpec((1, D_MODEL), lambda i: (0, 0))],
        out_specs=pl.BlockSpec((tm, D_MODEL), lambda i: (i, 0)),
        scratch_shapes=[pltpu.VMEM((TOP_K, tm, PACKED_COLS), u32), pltpu.SemaphoreType.DMA],
        compiler_params=_cparams(("arbitrary",)),
        name="combine",
    )(dest_tiles, y, gcol, xnew, mod_l, final_g)


def _moe(h2_parts, logits, router_bias_col, su, w_gate, w_up, w_down, layer, buf):
    t = logits.shape[0]
    ids, rank, gcol, cnt = _route(logits, router_bias_col, su)
    counts = cnt[:, 0].astype(i32)
    padded = (counts + EXPERT_BLOCK - 1) // EXPERT_BLOCK * EXPERT_BLOCK
    pend = jnp.cumsum(padded)
    pstart = pend - padded
    nb_needed = (t * TOP_K) // EXPERT_BLOCK + N_EXPERTS
    if buf is None:
        buf = jnp.zeros((nb_needed * EXPERT_BLOCK, PACKED_COLS), u32)
    nb = buf.shape[0] // EXPERT_BLOCK
    assert nb >= nb_needed
    block_pos = jnp.arange(nb, dtype=i32) * EXPERT_BLOCK
    block_e = jnp.minimum(jnp.sum((pend[None, :] <= block_pos[:, None]).astype(i32), axis=1), N_EXPERTS - 1)
    n_used = (pend[-1:] // EXPERT_BLOCK).astype(i32)
    dest_tiles = _dest(ids, rank, pstart.astype(f32)[:, None])
    tile_off = 0
    for part in h2_parts:
        buf = _dispatch(dest_tiles, tile_off, part, buf)
        tile_off += part.shape[0] // MOE_TILE
    y = _experts(block_e, n_used, buf, w_gate, w_up, w_down, layer)
    return dest_tiles, gcol, y, buf


def _pack_layer(l, w_in, conv_w, conv_b, dt_bias_f, dt_bias_b, a_log_f, a_log_b, d_skip, ssd_norm_g,
                gk_up_f, gk_bias_f, gk_up_b, gk_bias_b, gla_norm_g, w_out, norm1_g, norm2_g):
    w = w_in[l]
    z, xbc, dt, q, k, v, go, gk = jnp.split(w, [512, 1280, 1288, 1544, 1800, 2312, 2824], axis=1)
    small = jnp.concatenate([dt, gk, jnp.zeros((D_MODEL, SMALL_COLS - 24), f32)], axis=1)
    w_packed = jnp.concatenate([z, xbc, q, k, v, go, small], axis=1).astype(bf16)

    def lane_row(vec):
        return jnp.zeros((1, LANES), f32).at[0, :SSD_HEADS].set(vec)

    def sub_col(vec):
        return jnp.broadcast_to(vec[:, None], (SSD_HEADS, CHUNK)).astype(f32)

    def gup(m):
        return jnp.zeros((SMALL_COLS, GLA_QK), f32).at[SSD_HEADS:SSD_HEADS + GLA_GATE_RANK].set(m).astype(bf16)

    a_f = -jnp.exp(a_log_f[l])
    a_b = -jnp.exp(a_log_b[l])
    return dict(
        w_packed=w_packed,
        conv_w8=jnp.zeros((8, SSD_XBC), f32).at[:5].set(conv_w[l]),
        conv_b=conv_b[l][None, :],
        ssd_f=(lane_row(dt_bias_f[l]), lane_row(a_f), sub_col(dt_bias_f[l]), sub_col(a_f)),
        ssd_b=(lane_row(dt_bias_b[l]), lane_row(a_b), sub_col(dt_bias_b[l]), sub_col(a_b)),
        gla_f=(gup(gk_up_f[l]), gk_bias_f[l][None, :]),
        gla_b=(gup(gk_up_b[l]), gk_bias_b[l][None, :]),
        dsk=jnp.repeat(d_skip[l], SSD_HEAD_DIM)[None, :],
        sg=ssd_norm_g[l][None, :],
        gg=jnp.tile(gla_norm_g[l], GLA_HEADS)[None, :],
        wout=w_out[l].astype(bf16),
        n1g=norm1_g[l][None, :],
        n2g=norm2_g[l][None, :],
    )


def _constants():
    r = jnp.arange(LANES)[:, None]
    c = jnp.arange(SSD_WIDTH)[None, :]
    e_mat = ((c // SSD_HEAD_DIM) == r).astype(bf16)
    ssd_mask = ((r // SSD_STATE) == (c // (SSD_WIDTH // SSD_GROUPS))).astype(f32)
    rr = jnp.arange(GLA_WIDTH)[:, None]
    cc = jnp.arange(GLA_QK)[None, :]
    gla_mask = ((rr // GLA_VAL_DIM) == (cc // GLA_KEY_DIM)).astype(f32)
    k = jnp.arange(ROUTE_TILE)
    su = (k[:, None] < k[None, :]).astype(bf16)
    return e_mat, ssd_mask, gla_mask, su


def _mixers(streams, pk, consts, nb):
    e_mat, ssd_mask, gla_mask, _ = consts
    out = {}
    h0_f = jnp.zeros((nb, LANES, SSD_WIDTH), f32)
    h0_b = h0_f
    s0_f = jnp.zeros((nb, GLA_WIDTH, GLA_QK), f32)
    s0_b = s0_f
    for name in ('ctx', 'lat'):
        z, xs, bc, q, k, v, go, sm = streams[name]
        yf, yb, h0_f, h0_b = _ssd_scan(xs, bc, sm, h0_f, h0_b, pk['ssd_f'], pk['ssd_b'], e_mat, ssd_mask, nb=nb)
        of, ob, s0_f, s0_b = _gla_scan(q, k, v, sm, s0_f, s0_b, pk['gla_f'], pk['gla_b'], gla_mask, nb=nb)
        out[name] = (yf, yb, of, ob)
    return out


def kernel(x, c, ctx, c_ctx, w_mod, b_mod, norm1_g, norm2_g, w_in, conv_w, conv_b, dt_bias_f, dt_bias_b, a_log_f, a_log_b, d_skip, ssd_norm_g, gk_up_f, gk_bias_f, gk_up_b, gk_bias_b, gla_norm_g, w_out, w_router, router_bias, w_gate, w_up, w_down, final_norm_g):
    nb, seq, d = x.shape
    ctx_len = ctx.shape[1]
    depth = w_mod.shape[0]
    consts = _constants()
    su = consts[3]

    cvec = jnp.zeros((16, d), f32).at[:nb].set(c).at[nb].set(c_ctx)
    mod = _modulation(cvec, w_mod, b_mod).reshape(depth, 16, 6, d)
    ctx_row = nb

    perm = jnp.array([g * EXPERTS_PER_GROUP + m for m in range(EXPERTS_PER_GROUP)
                      for g in range(N_EXPERT_GROUPS)], dtype=i32)
    wr32 = jnp.zeros((d, LANES), f32).at[:, :N_EXPERTS].set(w_router[:, perm])
    wr_hi = wr32.astype(bf16)
    wr = jnp.stack([wr_hi, (wr32 - wr_hi.astype(f32)).astype(bf16)])
    bias_col = router_bias[perm][:, None]

    x2 = x.reshape(nb * seq, d)
    c2 = ctx.reshape(nb * ctx_len, d)
    lat_tiles = seq // TM_LAT
    moe_buf = None

    for l in range(depth):
        last = l == depth - 1
        pk = _pack_layer(l, w_in, conv_w, conv_b, dt_bias_f, dt_bias_b, a_log_f, a_log_b, d_skip, ssd_norm_g,
                         gk_up_f, gk_bias_f, gk_up_b, gk_bias_b, gla_norm_g, w_out, norm1_g, norm2_g)
        mod_l = mod[l]
        streams = {
            'ctx': _inproj(c2, mod_l, pk['n1g'], pk['w_packed'], pk['conv_w8'], pk['conv_b'],
                           tm=ctx_len, rowlen=ctx_len, tiles_per_row=1, fixed_row=ctx_row),
            'lat': _inproj(x2, mod_l, pk['n1g'], pk['w_packed'], pk['conv_w8'], pk['conv_b'],
                           tm=TM_LAT, rowlen=GRID_W, tiles_per_row=lat_tiles, fixed_row=None),
        }
        mix = _mixers(streams, pk, consts, nb)

        def merge(name, xres, tm, tiles_per_row, fixed_row):
            z, xs, bc, q, k, v, go, sm = streams[name]
            yf, yb, of, ob = mix[name]
            return _outproj(yf, yb, xs, z, of, ob, go, xres, mod_l, pk['dsk'], pk['sg'], pk['gg'],
                            pk['wout'], pk['n2g'], wr, tm=tm, tiles_per_row=tiles_per_row, fixed_row=fixed_row)

        xn_lat, h2_lat, lg_lat = merge('lat', x2, TM_LAT, lat_tiles, None)
        if last:
            parts, logits = [h2_lat], lg_lat
        else:
            xn_ctx, h2_ctx, lg_ctx = merge('ctx', c2, ctx_len, 1, ctx_row)
            parts, logits = [h2_ctx, h2_lat], jnp.concatenate([lg_ctx, lg_lat], axis=0)
        dest_tiles, gcol, y, moe_buf = _moe(parts, logits, bias_col, su, w_gate, w_up, w_down, l, moe_buf)
        lat_off = 0
        if not last:
            c2 = _combine(dest_tiles, 0, y, gcol, xn_ctx, mod_l, final_norm_g[None, :],
                          tiles_per_row=1, fixed_row=ctx_row, final=False)
            lat_off = xn_ctx.shape[0] // MOE_TILE
        x2 = _combine(dest_tiles, lat_off, y, gcol, xn_lat, mod_l, final_norm_g[None, :],
                      tiles_per_row=seq // MOE_TILE, fixed_row=None, final=last)
    return x2.reshape(nb, seq, d)
```

```python
import functools

import jax
import jax.numpy as jnp
from jax import lax
from jax.experimental import pallas as pl
from jax.experimental.pallas import tpu as pltpu
from jax.experimental.pallas import tpu_sc as plsc

f32 = jnp.float32
bf16 = jnp.bfloat16
i32 = jnp.int32
u32 = jnp.uint32

D_MODEL = 1024
SSD_HEADS = 8
SSD_HEAD_DIM = 64
SSD_WIDTH = 512
SSD_GROUPS = 2
SSD_STATE = 64
SSD_XBC = 768
GLA_HEADS = 4
GLA_KEY_DIM = 64
GLA_VAL_DIM = 128
GLA_QK = 256
GLA_WIDTH = 512
GLA_GATE_RANK = 16
GLA_GATE_NORM = 16.0
GRID_W = 64
N_EXPERTS = 32
N_EXPERT_GROUPS = 8
EXPERTS_PER_GROUP = 4
TOP_K = 2
EXPERT_FF = 512
NORM_EPS = 1e-6

LANES = 128
CHUNK = 128
CHUNKS_PER_STEP = 4
INTERLEAVE_WAVE = 8
ROW_SPLIT = 2
SSD_SPLIT = 3
GLA_SPLIT = 2
MAIN_COLS = 2816
SMALL_COLS = LANES
TM_LAT = 512
ROUTE_TILE = 512
MOE_TILE = 512
ROW_DMA_UNROLL = 8
SC_WINDOW = 128
SC_ROW_PARTS = 2
PACKED_COLS = D_MODEL // 2
EXPERT_BLOCK = 512
VMEM_LIMIT = 48 * 1024 * 1024

_HI = lax.Precision.HIGHEST
_NT = (((1,), (1,)), ((), ()))


def _dot(a, b, precision=None):
    return jnp.dot(a, b, preferred_element_type=f32, precision=precision)


def _silu(x):
    return x * jax.nn.sigmoid(x)


def _row_spec(rows, block_index):
    return pl.BlockSpec((rows, PACKED_COLS), lambda *idx: (block_index(*idx), 0))


def _pack_rows(value):
    hi = pltpu.bitcast(value[:, :PACKED_COLS].astype(bf16).astype(f32), u32)
    lo = pltpu.bitcast(value[:, PACKED_COLS:].astype(bf16).astype(f32), u32)
    return hi | (lo >> 16)


def _unpack_rows(packed):
    hi = pltpu.bitcast(packed & jnp.uint32(0xFFFF0000), f32)
    lo = pltpu.bitcast(packed << 16, f32)
    return jnp.concatenate([hi, lo], axis=1)


def _cparams(sem):
    return pltpu.CompilerParams(dimension_semantics=sem, vmem_limit_bytes=VMEM_LIMIT)


def _mod_kernel(c_ref, w_ref, b_ref, o_ref):
    sc = _silu(c_ref[...]).astype(bf16)
    o_ref[...] = _dot(sc, w_ref[...].astype(bf16)) + b_ref[...]


def _modulation(cvec, w_mod, b_mod):
    depth, d, n = w_mod.shape
    tn = 1536
    return pl.pallas_call(
        _mod_kernel,
        out_shape=jax.ShapeDtypeStruct((depth, 16, n), f32),
        grid=(depth, n // tn),
        in_specs=[pl.BlockSpec((16, d), lambda l, j: (0, 0)),
                  pl.BlockSpec((None, d, tn), lambda l, j: (l, 0, j)),
                  pl.BlockSpec((None, 1, tn), lambda l, j: (l, 0, j))],
        out_specs=pl.BlockSpec((None, 16, tn), lambda l, j: (l, 0, j)),
        compiler_params=_cparams(("arbitrary", "arbitrary")),
        name="modulation",
    )(cvec, w_mod, b_mod.reshape(depth, 1, n))


def _inproj_kernel(x_ref, mod_ref, g_ref, w_ref, cw_ref, cb_ref,
                   z_ref, xs_ref, bc_ref, q_ref, k_ref, v_ref, go_ref, sm_ref, *, rowlen):
    tm = x_ref.shape[0]
    split = ROW_SPLIT if (tm // ROW_SPLIT) % rowlen == 0 else 1
    sub = tm // split
    m = mod_ref[...]
    geff = g_ref[...] * (1.0 + m[1:2])
    cw = cw_ref[...]

    def rows_gen(rows):
        x = x_ref[rows, :]
        inv = lax.rsqrt(jnp.mean(x * x, axis=-1, keepdims=True) + NORM_EPS)
        h = ((x * inv) * geff + m[0:1]).astype(bf16)
        yield

        def proj(lo, hi):
            return _dot(h, w_ref[:, lo:hi])

        z_ref[rows, :] = proj(0, 512).astype(bf16)
        yield

        xbc = proj(512, 1280)
        yield
        pos = lax.broadcasted_iota(i32, xbc.shape, 0) & (rowlen - 1)
        acc = xbc * cw[2:3]
        for d in (-2, -1, 1, 2):
            shifted = pltpu.roll(xbc, (-d) % sub, 0)
            valid = (pos >= -d) if d < 0 else (pos <= rowlen - 1 - d)
            acc = acc + jnp.where(valid, shifted, 0.0) * cw[2 + d:3 + d]
        yield
        act = _silu(acc + cb_ref[...])
        xs_ref[rows, :] = act[:, :SSD_WIDTH].astype(bf16)
        bc_ref[rows, :] = act[:, SSD_WIDTH:].astype(bf16)
        yield

        qk = proj(1280, 1792)
        q_ref[rows, :] = qk[:, :GLA_QK].astype(bf16)
        k_ref[rows, :] = qk[:, GLA_QK:].astype(bf16)
        yield
        v_ref[rows, :] = proj(1792, 2304).astype(bf16)
        yield
        go_ref[rows, :] = proj(2304, 2816).astype(bf16)
        yield
        sm_ref[rows, :] = proj(2816, 2944)

    _run_interleaved(rows_gen(pl.ds(r * sub, sub)) for r in range(split))


def _inproj(x2d, mod_l, norm_g, w_packed, conv_w8, conv_b, *, tm, rowlen, tiles_per_row, fixed_row):
    t = x2d.shape[0]
    nt = t // tm
    if fixed_row is None:
        mod_map = lambda i: (i // tiles_per_row, 0, 0)
    else:
        mod_map = lambda i: (fixed_row, 0, 0)
    tok = lambda w: pl.BlockSpec((tm, w), lambda i: (i, 0))
    const = lambda a: pl.BlockSpec(a.shape, lambda i: (0,) * a.ndim)
    widths = (512, 512, 256, 256, 256, 512, 512)
    return pl.pallas_call(
        functools.partial(_inproj_kernel, rowlen=rowlen),
        out_shape=tuple(jax.ShapeDtypeStruct((t, w), bf16) for w in widths)
        + (jax.ShapeDtypeStruct((t, SMALL_COLS), f32),),
        grid=(nt,),
        in_specs=[tok(D_MODEL), pl.BlockSpec((None, 6, D_MODEL), mod_map), const(norm_g),
                  const(w_packed), const(conv_w8), const(conv_b)],
        out_specs=tuple(tok(w) for w in widths) + (tok(SMALL_COLS),),
        compiler_params=_cparams(("arbitrary",)),
        name="inproj",
    )(x2d, mod_l, norm_g, w_packed, conv_w8, conv_b)


def _dot_split(a, b, parts, *, split_lhs):
    rest = a if split_lhs else b
    acc = None
    for _ in range(parts):
        piece = rest.astype(bf16)
        rest = rest - piece.astype(f32)
        term = _dot(piece, b) if split_lhs else _dot(a, piece)
        acc = term if acc is None else acc + term
    return acc


def _tri_masks(q):
    r = lax.broadcasted_iota(i32, (q, q), 0)
    c = lax.broadcasted_iota(i32, (q, q), 1)
    return r >= c, r <= c


def _ssd_chunk(load, store, carry, par, e, bm, rev):
    xb, bcv, sm = load()
    dtb, a_lane, dtbt, a_sub = par
    q = xb.shape[0]
    dt_col = jax.nn.softplus(sm + dtb)
    a_col = dt_col * a_lane
    dt_row = jax.nn.softplus(sm.T[0:SSD_HEADS, :] + dtbt)
    a_row = dt_row * a_sub
    yield
    lower, upper = _tri_masks(q)
    lo_b = jnp.where(lower, 1.0, 0.0).astype(bf16)
    up_b = jnp.where(upper, 1.0, 0.0).astype(bf16)
    if not rev:
        cs_col = _dot_split(lo_b, a_col, SSD_SPLIT, split_lhs=False)
        cs_row = _dot_split(a_row, up_b, SSD_SPLIT, split_lhs=True)
        mask = lower
        a_tot = cs_col[q - 1:q, :]
    else:
        cs_col = _dot_split(up_b, a_col, SSD_SPLIT, split_lhs=False)
        cs_row = _dot_split(a_row, lo_b, SSD_SPLIT, split_lhs=True)
        mask = upper
        a_tot = cs_col[0:1, :]
    yield

    w_exp = _dot((jnp.exp(a_tot - cs_col) * dt_col).astype(bf16), e)
    ecs_exp = _dot(jnp.exp(cs_col).astype(bf16), e)
    dec_exp = _dot_split(jnp.broadcast_to(jnp.exp(a_tot), (8, LANES)), e, SSD_SPLIT, split_lhs=True)[0:1]
    yield

    xw = (xb.astype(f32) * w_exp).astype(bf16)
    b_all = bcv[:, 0:LANES]
    c_all = bcv[:, LANES:2 * LANES]
    b_t = b_all.astype(f32).T.astype(bf16)
    yield
    s_upd = _dot(b_t, xw)
    yield

    lane = lax.broadcasted_iota(i32, (q, LANES), 1)
    zero_b = jnp.zeros((q, LANES), bf16)
    ys = []
    for g in range(SSD_GROUPS):
        in_g = (lane >= SSD_STATE * g) & (lane < SSD_STATE * (g + 1))
        cb = _dot(jnp.where(in_g, c_all, zero_b), b_t)
        yield
        for pp in range(2):
            h0 = 4 * g + 2 * pp
            ms = []
            for h in (h0, h0 + 1):
                seg = cs_col[:, h:h + 1] - cs_row[h:h + 1, :]
                dec = jnp.exp(jnp.where(mask, seg, -1e30))
                ms.append((cb * dec * dt_row[h:h + 1, :]).astype(bf16))
            xp = xb[:, h0 * SSD_HEAD_DIM:h0 * SSD_HEAD_DIM + LANES]
            rhs = jnp.concatenate([jnp.where(lane < SSD_HEAD_DIM, xp, zero_b),
                                   jnp.where(lane >= SSD_HEAD_DIM, xp, zero_b)], axis=0)
            yield
            ys.append(_dot(jnp.concatenate(ms, axis=1), rhs))
            yield
    y_intra = jnp.concatenate(ys, axis=1)
    yield

    s_old = carry['s']
    carry['s'] = (s_old * dec_exp + s_upd) * bm
    store((y_intra + _dot(c_all, s_old.astype(bf16)) * ecs_exp).astype(bf16))


def _run_interleaved(gens):
    gens = list(gens)
    done = object()
    for w in range(0, len(gens), INTERLEAVE_WAVE):
        live = gens[w:w + INTERLEAVE_WAVE]
        while live:
            live = [g for g in live if next(g, done) is not done]


def _chunk_rows(cps):
    return [(pl.ds(ci * CHUNK, CHUNK), pl.ds((cps - 1 - ci) * CHUNK, CHUNK)) for ci in range(cps)]


def _ssd_kernel(xf_ref, bcf_ref, smf_ref, xr_ref, bcr_ref, smr_ref, h0f_ref, h0r_ref,
                dtbf_ref, af_ref, dtbtf_ref, atf_ref, dtbr_ref, ar_ref, dtbtr_ref, atr_ref, e_ref, bm_ref,
                yf_ref, yr_ref, hff_ref, hfr_ref, st_f, st_r, *, cps, nblk):
    j = pl.program_id(1)

    @pl.when(j == 0)
    def _():
        st_f[...] = h0f_ref[...]
        st_r[...] = h0r_ref[...]

    e = e_ref[...]
    bm = bm_ref[...]
    par_f = (dtbf_ref[...], af_ref[...], dtbtf_ref[...], atf_ref[...])
    par_r = (dtbr_ref[...], ar_ref[...], dtbtr_ref[...], atr_ref[...])
    carry_f = {'s': st_f[...]}
    carry_r = {'s': st_r[...]}

    def chunk(refs, out_ref, rows, carry, par, rev):
        def store(y):
            out_ref[rows, :] = y
        return _ssd_chunk(lambda: tuple(r[rows, :] for r in refs), store, carry, par, e, bm, rev)

    gens = []
    for rows_f, rows_r in _chunk_rows(cps):
        gens.append(chunk((xf_ref, bcf_ref, smf_ref), yf_ref, rows_f, carry_f, par_f, False))
        gens.append(chunk((xr_ref, bcr_ref, smr_ref), yr_ref, rows_r, carry_r, par_r, True))
    _run_interleaved(gens)
    st_f[...] = carry_f['s']
    st_r[...] = carry_r['s']

    @pl.when(j == nblk - 1)
    def _():
        hff_ref[...] = carry_f['s']
        hfr_ref[...] = carry_r['s']


def _scan_specs(t, nb):
    per_row = t // nb
    cps = min(CHUNKS_PER_STEP, per_row // CHUNK)
    nblk = per_row // (cps * CHUNK)
    fmap = lambda b, j: (b * nblk + j, 0)
    rmap = lambda b, j: (b * nblk + nblk - 1 - j, 0)
    return cps, nblk, fmap, rmap


def _ssd_scan(xs, bc, sm, h0_f, h0_r, par_f, par_r, e_mat, bmask, *, nb):
    t = xs.shape[0]
    cps, nblk, fmap, rmap = _scan_specs(t, nb)
    blk = cps * CHUNK
    const = lambda a: pl.BlockSpec(a.shape, lambda b, j: (0,) * a.ndim)
    st = pl.BlockSpec((None, LANES, SSD_WIDTH), lambda b, j: (b, 0, 0))
    widths = (SSD_WIDTH, 2 * LANES, SMALL_COLS)
    tok_in = [pl.BlockSpec((blk, w), m) for m in (fmap, rmap) for w in widths]
    consts = (*par_f, *par_r, e_mat, bmask)
    return pl.pallas_call(
        functools.partial(_ssd_kernel, cps=cps, nblk=nblk),
        out_shape=(jax.ShapeDtypeStruct((t, SSD_WIDTH), bf16), jax.ShapeDtypeStruct((t, SSD_WIDTH), bf16),
                   jax.ShapeDtypeStruct((nb, LANES, SSD_WIDTH), f32),
                   jax.ShapeDtypeStruct((nb, LANES, SSD_WIDTH), f32)),
        grid=(nb, nblk),
        in_specs=tok_in + [st, st] + [const(a) for a in consts],
        out_specs=(pl.BlockSpec((blk, SSD_WIDTH), fmap), pl.BlockSpec((blk, SSD_WIDTH), rmap), st, st),
        scratch_shapes=[pltpu.VMEM((LANES, SSD_WIDTH), f32), pltpu.VMEM((LANES, SSD_WIDTH), f32)],
        compiler_params=_cparams(("arbitrary", "arbitrary")),
        name="ssd",
    )(xs, bc, sm, xs, bc, sm, h0_f, h0_r, *consts)


def _gla_chunk(load, store, carry, par, bm, rev):
    qb, kb, vb, sm = load()
    gup, gbias = par
    q = qb.shape[0]
    gp = _dot(sm.astype(bf16), gup) + gbias
    yield
    g = jax.nn.log_sigmoid(gp) * (1.0 / GLA_GATE_NORM)
    yield
    lower, upper = _tri_masks(q)
    if not rev:
        b = _dot_split(jnp.where(lower, 1.0, 0.0).astype(bf16), g, GLA_SPLIT, split_lhs=False)
        mask, mid = lower, q // 2 - 1
        b_tot = b[q - 1:q, :]
    else:
        b = _dot_split(jnp.where(upper, 1.0, 0.0).astype(bf16), g, GLA_SPLIT, split_lhs=False)
        mask, mid = upper, q // 2
        b_tot = b[0:1, :]
    b_mid = b[mid:mid + 1, :]
    yield

    qf = qb.astype(f32) * (GLA_KEY_DIM ** -0.5)
    kf = kb.astype(f32)
    qd = (qf * jnp.exp(b - b_mid)).astype(bf16)
    yield
    ki = (kf * jnp.exp(b_mid - b)).astype(bf16)
    yield
    q_st = (qf * jnp.exp(b)).astype(bf16)
    yield
    k_end = (kf * jnp.exp(b_tot - b)).astype(bf16)
    dec = jnp.exp(b_tot)
    yield
    v_t = vb.astype(f32).T.astype(bf16)
    yield
    s_upd = _dot(v_t, k_end)
    yield

    lane = lax.broadcasted_iota(i32, (q, LANES), 1)
    zero_b = jnp.zeros((q, LANES), bf16)
    outs = []
    for h in range(GLA_HEADS):
        p, hh = divmod(h, 2)
        qp = qd[:, p * LANES:(p + 1) * LANES]
        kp = ki[:, p * LANES:(p + 1) * LANES]
        in_h = (lane >= GLA_KEY_DIM * hh) & (lane < GLA_KEY_DIM * (hh + 1))
        s = lax.dot_general(jnp.where(in_h, qp, zero_b), kp, _NT, preferred_element_type=f32)
        yield
        attn = jnp.where(mask, s, 0.0).astype(bf16)
        outs.append(_dot(attn, vb[:, h * GLA_VAL_DIM:(h + 1) * GLA_VAL_DIM]))
        yield
    o_intra = jnp.concatenate(outs, axis=1)
    yield

    s_old = carry['s']
    carry['s'] = (s_old * dec + s_upd) * bm
    o_inter = lax.dot_general(q_st, s_old.astype(bf16), _NT, preferred_element_type=f32)
    store((o_intra + o_inter).astype(bf16))


def _gla_kernel(qf_ref, kf_ref, vf_ref, smf_ref, qr_ref, kr_ref, vr_ref, smr_ref, s0f_ref, s0r_ref,
                gupf_ref, gbf_ref, gupr_ref, gbr_ref, bm_ref,
                of_ref, or_ref, sff_ref, sfr_ref, st_f, st_r, *, cps, nblk):
    j = pl.program_id(1)

    @pl.when(j == 0)
    def _():
        st_f[...] = s0f_ref[...]
        st_r[...] = s0r_ref[...]

    bm = bm_ref[...]
    par_f = (gupf_ref[...], gbf_ref[...])
    par_r = (gupr_ref[...], gbr_ref[...])
    carry_f = {'s': st_f[...]}
    carry_r = {'s': st_r[...]}

    def chunk(refs, out_ref, rows, carry, par, rev):
        def store(o):
            out_ref[rows, :] = o
        return _gla_chunk(lambda: tuple(r[rows, :] for r in refs), store, carry, par, bm, rev)

    gens = []
    for rows_f, rows_r in _chunk_rows(cps):
        gens.append(chunk((qf_ref, kf_ref, vf_ref, smf_ref), of_ref, rows_f, carry_f, par_f, False))
        gens.append(chunk((qr_ref, kr_ref, vr_ref, smr_ref), or_ref, rows_r, carry_r, par_r, True))
    _run_interleaved(gens)
    st_f[...] = carry_f['s']
    st_r[...] = carry_r['s']

    @pl.when(j == nblk - 1)
    def _():
        sff_ref[...] = carry_f['s']
        sfr_ref[...] = carry_r['s']


def _gla_scan(qa, ka, va, sm, s0_f, s0_r, par_f, par_r, bmask, *, nb):
    t = qa.shape[0]
    cps, nblk, fmap, rmap = _scan_specs(t, nb)
    blk = cps * CHUNK
    const = lambda a: pl.BlockSpec(a.shape, lambda b, j: (0,) * a.ndim)
    st = pl.BlockSpec((None, GLA_WIDTH, GLA_QK), lambda b, j: (b, 0, 0))
    widths = (GLA_QK, GLA_QK, GLA_WIDTH, SMALL_COLS)
    tok_in = [pl.BlockSpec((blk, w), m) for m in (fmap, rmap) for w in widths]
    consts = (*par_f, *par_r, bmask)
    return pl.pallas_call(
        functools.partial(_gla_kernel, cps=cps, nblk=nblk),
        out_shape=(jax.ShapeDtypeStruct((t, GLA_WIDTH), bf16), jax.ShapeDtypeStruct((t, GLA_WIDTH), bf16),
                   jax.ShapeDtypeStruct((nb, GLA_WIDTH, GLA_QK), f32),
                   jax.ShapeDtypeStruct((nb, GLA_WIDTH, GLA_QK), f32)),
        grid=(nb, nblk),
        in_specs=tok_in + [st, st] + [const(a) for a in consts],
        out_specs=(pl.BlockSpec((blk, GLA_WIDTH), fmap), pl.BlockSpec((blk, GLA_WIDTH), rmap), st, st),
        scratch_shapes=[pltpu.VMEM((GLA_WIDTH, GLA_QK), f32), pltpu.VMEM((GLA_WIDTH, GLA_QK), f32)],
        compiler_params=_cparams(("arbitrary", "arbitrary")),
        name="gla",
    )(qa, ka, va, sm, qa, ka, va, sm, s0_f, s0_r, *consts)


def _outproj_kernel(yf_ref, yb_ref, xs_ref, z_ref, of_ref, ob_ref, go_ref, xres_ref, mod_ref,
                    dsk_ref, sg_ref, gg_ref, wout_ref, n2g_ref, wr_ref,
                    xnew_ref, h2_ref, lg_ref):
    tm = xres_ref.shape[0]
    sub = tm // ROW_SPLIT
    m = mod_ref[...]
    h2_gain = n2g_ref[...] * (1.0 + m[4:5])

    def rows_gen(rows):
        up = lambda r: r[rows, :].astype(f32)
        y = up(yf_ref) + up(yb_ref) + dsk_ref[...] * up(xs_ref)
        yield
        y = y * _silu(up(z_ref))
        yield
        y = y * lax.rsqrt(jnp.mean(y * y, axis=-1, keepdims=True) + NORM_EPS) * sg_ref[...]
        yield
        o = up(of_ref) + up(ob_ref)
        parts = []
        for h in range(GLA_HEADS):
            oh = o[:, h * GLA_VAL_DIM:(h + 1) * GLA_VAL_DIM]
            parts.append(oh * lax.rsqrt(jnp.mean(oh * oh, axis=-1, keepdims=True) + NORM_EPS))
        yield
        o = jnp.concatenate(parts, axis=1) * gg_ref[...] * _silu(up(go_ref))
        yield
        cat = jnp.concatenate([y, o], axis=1).astype(bf16)
        mix = _dot(cat, wout_ref[...])
        yield
        xn = xres_ref[rows, :] + m[2:3] * mix
        xnew_ref[rows, :] = xn
        yield
        inv = lax.rsqrt(jnp.mean(xn * xn, axis=-1, keepdims=True) + NORM_EPS)
        h2 = (xn * inv) * h2_gain + m[3:4]
        h2_ref[rows, :] = _pack_rows(h2)
        yield
        h2_hi = h2.astype(bf16)
        h2_lo = (h2 - h2_hi.astype(f32)).astype(bf16)
        wr_hi = wr_ref[0]
        lg_ref[rows, :] = _dot(h2_hi, wr_hi) + _dot(h2_hi, wr_ref[1]) + _dot(h2_lo, wr_hi)

    _run_interleaved(rows_gen(pl.ds(r * sub, sub)) for r in range(ROW_SPLIT))


def _outproj(yf, yb, xs, z, of, ob, go, xres, mod_l, dsk, sg, gg, wout, n2g, wr,
             *, tm, tiles_per_row, fixed_row):
    t = xres.shape[0]
    if fixed_row is None:
        mod_map = lambda i: (i // tiles_per_row, 0, 0)
    else:
        mod_map = lambda i: (fixed_row, 0, 0)
    tok = lambda w: pl.BlockSpec((tm, w), lambda i: (i, 0))
    const = lambda a: pl.BlockSpec(a.shape, lambda i: (0,) * a.ndim)
    return pl.pallas_call(
        _outproj_kernel,
        out_shape=(jax.ShapeDtypeStruct((t, D_MODEL), f32), jax.ShapeDtypeStruct((t, PACKED_COLS), u32),
                   jax.ShapeDtypeStruct((t, LANES), f32)),
        grid=(t // tm,),
        in_specs=[tok(512)] * 7 + [tok(D_MODEL), pl.BlockSpec((None, 6, D_MODEL), mod_map),
                                   const(dsk), const(sg), const(gg), const(wout), const(n2g), const(wr)],
        out_specs=(tok(D_MODEL), _row_spec(tm, lambda i: i), tok(LANES)),
        compiler_params=_cparams(("arbitrary",)),
        name="outproj",
    )(yf, yb, xs, z, of, ob, go, xres, mod_l, dsk, sg, gg, wout, n2g, wr)


def _route_kernel(lg_ref, bias_ref, su_ref, ids_ref, rank_ref, gcol_ref, cnt_ref, carry):
    i = pl.program_id(0)
    tm = lg_ref.shape[0]

    @pl.when(i == 0)
    def _():
        carry[...] = jnp.zeros_like(carry)

    s = jax.nn.sigmoid(lg_ref[...].T[0:N_EXPERTS, :])
    sel = s + bias_ref[...]
    a = [sel[8 * m:8 * (m + 1)] for m in range(EXPERTS_PER_GROUP)]
    sv = [s[8 * m:8 * (m + 1)] for m in range(EXPERTS_PER_GROUP)]
    hi01, lo01 = jnp.maximum(a[0], a[1]), jnp.minimum(a[0], a[1])
    hi23, lo23 = jnp.maximum(a[2], a[3]), jnp.minimum(a[2], a[3])
    gscore = jnp.maximum(hi01, hi23) + jnp.maximum(jnp.minimum(hi01, hi23), jnp.maximum(lo01, lo23))
    giota = lax.broadcasted_iota(i32, gscore.shape, 0)
    gmax = jnp.max(gscore, axis=0, keepdims=True)
    gidx = jnp.min(jnp.where(gscore == gmax, giota, N_EXPERT_GROUPS), axis=0, keepdims=True)
    pick = giota == gidx
    v = [jnp.sum(jnp.where(pick, a[m], 0.0), axis=0, keepdims=True) for m in range(4)]
    w = [jnp.sum(jnp.where(pick, sv[m], 0.0), axis=0, keepdims=True) for m in range(4)]

    def first_max(vals, excluded):
        best = vals[0]
        for m in range(1, 4):
            best = jnp.maximum(best, vals[m])
        idx = jnp.full(best.shape, 3, i32)
        for m in (2, 1, 0):
            hit = vals[m] == best
            if excluded is not None:
                hit = hit & (excluded != m)
            idx = jnp.where(hit, m, idx)
        return idx

    i1 = first_max(v, None)
    v_rest = [jnp.where(i1 == m, -jnp.inf, v[m]) for m in range(4)]
    i2 = first_max(v_rest, i1)

    def take(vals, idx):
        out = vals[3]
        for m in (2, 1, 0):
            out = jnp.where(idx == m, vals[m], out)
        return out

    w1, w2 = take(w, i1), take(w, i2)
    denom = w1 + w2
    id1 = gidx * EXPERTS_PER_GROUP + i1
    id2 = gidx * EXPERTS_PER_GROUP + i2
    row2 = lax.broadcasted_iota(i32, (TOP_K, tm), 0)
    ids_ref[...] = jnp.where(row2 == 0, id1, id2)

    eiota = lax.broadcasted_iota(i32, (N_EXPERTS, tm), 0)
    hit1, hit2 = eiota == id1, eiota == id2
    onehot = jnp.where(hit1, 1.0, 0.0) + jnp.where(hit2, 1.0, 0.0)
    before = _dot(onehot.astype(bf16), su_ref[...]) + carry[...]
    r1 = jnp.sum(jnp.where(hit1, before, 0.0), axis=0, keepdims=True)
    r2 = jnp.sum(jnp.where(hit2, before, 0.0), axis=0, keepdims=True)
    rank_ref[...] = jnp.where(row2 == 0, r1, r2).astype(i32)
    new_carry = carry[...] + jnp.sum(onehot, axis=1, keepdims=True)
    carry[...] = new_carry
    cnt_ref[...] = jnp.broadcast_to(new_carry, cnt_ref.shape)

    rows = lax.broadcasted_iota(i32, (LANES, tm), 0)
    gates = jnp.where(rows == 0, w1 / denom, jnp.where(rows == 1, w2 / denom, 0.0))
    gcol_ref[...] = gates.T


def _route(logits, bias_col, su):
    t = logits.shape[0]
    tm = ROUTE_TILE
    return pl.pallas_call(
        _route_kernel,
        out_shape=(jax.ShapeDtypeStruct((2, t), i32), jax.ShapeDtypeStruct((2, t), i32),
                   jax.ShapeDtypeStruct((t, LANES), f32), jax.ShapeDtypeStruct((N_EXPERTS, LANES), f32)),
        grid=(t // tm,),
        in_specs=[pl.BlockSpec((tm, LANES), lambda i: (i, 0)),
                  pl.BlockSpec((N_EXPERTS, 1), lambda i: (0, 0)),
                  pl.BlockSpec((tm, tm), lambda i: (0, 0))],
        out_specs=(pl.BlockSpec((2, tm), lambda i: (0, i)), pl.BlockSpec((2, tm), lambda i: (0, i)),
                   pl.BlockSpec((tm, LANES), lambda i: (i, 0)),
                   pl.BlockSpec((N_EXPERTS, LANES), lambda i: (0, 0))),
        scratch_shapes=[pltpu.VMEM((N_EXPERTS, 1), f32)],
        compiler_params=_cparams(("arbitrary",)),
        name="route",
    )(logits, bias_col, su)


def _row_copy(src_ref, src_row, dst_ref, dst_row, sem):
    return pltpu.make_async_copy(src_ref.at[pl.ds(src_row, 1)], dst_ref.at[pl.ds(dst_row, 1)], sem)


def _dest_kernel(ids_ref, rank_ref, pstart_ref, dest_ref):
    ids = ids_ref[...]
    tm = ids.shape[1]
    eiota = lax.broadcasted_iota(i32, (N_EXPERTS, tm), 0)
    ps = pstart_ref[...]
    rows = [jnp.sum(jnp.where(eiota == ids[k:k + 1, :], ps, 0.0), axis=0, keepdims=True) for k in range(TOP_K)]
    row2 = lax.broadcasted_iota(i32, (TOP_K, tm), 0)
    dest_ref[...] = jnp.where(row2 == 0, rows[0], rows[1]).astype(i32) + rank_ref[...]


def _dest(ids, rank, pstart_col):
    t = ids.shape[1]
    tm = MOE_TILE
    return pl.pallas_call(
        _dest_kernel,
        out_shape=jax.ShapeDtypeStruct((t // tm, TOP_K, tm), i32),
        grid=(t // tm,),
        in_specs=[pl.BlockSpec((TOP_K, tm), lambda i: (0, i)), pl.BlockSpec((TOP_K, tm), lambda i: (0, i)),
                  pl.BlockSpec((N_EXPERTS, 1), lambda i: (0, 0))],
        out_specs=pl.BlockSpec((None, TOP_K, tm), lambda i: (i, 0, 0)),
        compiler_params=_cparams(("arbitrary",)),
        name="dest",
    )(ids, rank, pstart_col)


def _dispatch_kernel(dest_ref, h_ref, buf_in_ref, buf_ref, sem):
    del buf_in_ref
    tm = dest_ref.shape[1]

    def issue(j, carry):
        _row_copy(h_ref, j, buf_ref, dest_ref[0, j], sem).start(priority=0)
        _row_copy(h_ref, j, buf_ref, dest_ref[1, j], sem).start(priority=1)
        return carry

    lax.fori_loop(0, tm, issue, 0, unroll=ROW_DMA_UNROLL)

    def drain(j, carry):
        _row_copy(h_ref, 0, buf_ref, 0, sem).wait()
        _row_copy(h_ref, 0, buf_ref, 0, sem).wait()
        return carry

    lax.fori_loop(0, tm, drain, 0, unroll=ROW_DMA_UNROLL)


def _dispatch(dest_tiles, tile_off, h2, buf):
    tm = dest_tiles.shape[2]
    return pl.pallas_call(
        _dispatch_kernel,
        out_shape=jax.ShapeDtypeStruct(buf.shape, buf.dtype),
        grid=(h2.shape[0] // tm,),
        in_specs=[pl.BlockSpec((None, TOP_K, tm), lambda i: (i + tile_off, 0, 0), memory_space=pltpu.SMEM),
                  _row_spec(tm, lambda i: i), pl.BlockSpec(memory_space=pl.ANY)],
        out_specs=pl.BlockSpec(memory_space=pl.ANY),
        scratch_shapes=[pltpu.SemaphoreType.DMA],
        input_output_aliases={2: 0},
        compiler_params=_cparams(("arbitrary",)),
        name="dispatch",
    )(dest_tiles, h2, buf)


def _expert_kernel(be_ref, nu_ref, x_ref, wg_ref, wu_ref, wd_ref, y_ref, wg_b, wu_b, wd_b):
    i = pl.program_id(0)
    fresh = jnp.logical_or(i == 0, be_ref[i] != be_ref[jnp.maximum(i - 1, 0)])

    @pl.when(jnp.logical_and(fresh, i < nu_ref[0]))
    def _():
        wg_b[...] = wg_ref[...].astype(bf16)
        wu_b[...] = wu_ref[...].astype(bf16)
        wd_b[...] = wd_ref[...].astype(bf16)

    @pl.when(i < nu_ref[0])
    def _():
        sub = x_ref.shape[0] // ROW_SPLIT

        def rows_gen(rows):
            xb = _unpack_rows(x_ref[rows, :]).astype(bf16)
            yield
            g = _dot(xb, wg_b[...])
            yield
            u = _dot(xb, wu_b[...])
            yield
            hmid = (_silu(g) * u).astype(bf16)
            yield
            y_ref[rows, :] = _pack_rows(_dot(hmid, wd_b[...]))

        _run_interleaved(rows_gen(pl.ds(r * sub, sub)) for r in range(ROW_SPLIT))

    @pl.when(i >= nu_ref[0])
    def _():
        y_ref[...] = jnp.zeros_like(y_ref)


def _experts(block_e, n_used, buf, w_gate, w_up, w_down, layer):
    p = buf.shape[0]
    nb = p // EXPERT_BLOCK
    d, ff = w_gate.shape[-2:]
    row_blocks = _row_spec(EXPERT_BLOCK, lambda i, be, nu: i)
    return pl.pallas_call(
        _expert_kernel,
        out_shape=jax.ShapeDtypeStruct(buf.shape, u32),
        grid_spec=pltpu.PrefetchScalarGridSpec(
            num_scalar_prefetch=2,
            grid=(nb,),
            in_specs=[row_blocks,
                      pl.BlockSpec((None, None, d, ff), lambda i, be, nu: (layer, be[i], 0, 0)),
                      pl.BlockSpec((None, None, d, ff), lambda i, be, nu: (layer, be[i], 0, 0)),
                      pl.BlockSpec((None, None, ff, d), lambda i, be, nu: (layer, be[i], 0, 0))],
            out_specs=row_blocks,
            scratch_shapes=[pltpu.VMEM((d, ff), bf16), pltpu.VMEM((d, ff), bf16), pltpu.VMEM((ff, d), bf16)],
        ),
        compiler_params=_cparams(("arbitrary",)),
        name="experts",
    )(block_e, n_used, buf, w_gate, w_up, w_down)


def _combine_kernel(dest_ref, y_ref, gcol_ref, x_ref, mod_ref, fg_ref, o_ref, ybuf, sem, *, final):
    tm = dest_ref.shape[1]

    def issue(j, carry):
        _row_copy(y_ref, dest_ref[0, j], ybuf.at[0], j, sem).start(priority=0)
        _row_copy(y_ref, dest_ref[1, j], ybuf.at[1], j, sem).start(priority=1)
        return carry

    lax.fori_loop(0, tm, issue, 0, unroll=ROW_DMA_UNROLL)

    def drain(j, carry):
        _row_copy(y_ref, 0, ybuf.at[0], 0, sem).wait()
        _row_copy(y_ref, 0, ybuf.at[1], 0, sem).wait()
        return carry

    lax.fori_loop(0, tm, drain, 0, unroll=ROW_DMA_UNROLL)

    gc = gcol_ref[...]
    ffn = gc[:, 0:1] * _unpack_rows(ybuf[0]) + gc[:, 1:2] * _unpack_rows(ybuf[1])
    x = x_ref[...] + mod_ref[...][5:6] * ffn
    if final:
        x = x * lax.rsqrt(jnp.mean(x * x, axis=-1, keepdims=True) + NORM_EPS) * fg_ref[...]
    o_ref[...] = x


def _combine(dest_tiles, tile_off, y, gcol, xnew, mod_l, final_g, *, tiles_per_row, fixed_row, final):
    tm = dest_tiles.shape[2]
    t = xnew.shape[0]
    if fixed_row is None:
        mod_map = lambda i: (i // tiles_per_row, 0, 0)
    else:
        mod_map = lambda i: (fixed_row, 0, 0)
    return pl.pallas_call(
        functools.partial(_combine_kernel, final=final),
        out_shape=jax.ShapeDtypeStruct((t, D_MODEL), f32),
        grid=(t // tm,),
        in_specs=[pl.BlockSpec((None, TOP_K, tm), lambda i: (i + tile_off, 0, 0), memory_space=pltpu.SMEM),
                  pl.BlockSpec(memory_space=pl.ANY),
                  pl.BlockSpec((tm, LANES), lambda i: (i + tile_off, 0)),
                  pl.BlockSpec((tm, D_MODEL), lambda i: (i, 0)),
                  pl.BlockSpec((None, 6, D_MODEL), mod_map),
                  pl.BlockSpec((1, D_MODEL), lambda i: (0, 0))],
        out_specs=pl.BlockSpec((tm, D_MODEL), lambda i: (i, 0)),
        scratch_shapes=[pltpu.VMEM((TOP_K, tm, PACKED_COLS), u32), pltpu.SemaphoreType.DMA],
        compiler_params=_cparams(("arbitrary",)),
        name="combine",
    )(dest_tiles, y, gcol, xnew, mod_l, final_g)


def _sc_gather_rows(table, dest_tiles):
    nt, _, tm = dest_tiles.shape
    t = nt * tm
    dest = dest_tiles.transpose(1, 0, 2).reshape(TOP_K * t)
    idx = (dest[:, None] * SC_ROW_PARTS + jnp.arange(SC_ROW_PARTS, dtype=i32)[None, :]).reshape(1, -1)
    n = idx.shape[1]
    cols = PACKED_COLS // SC_ROW_PARTS
    halves = table.reshape(table.shape[0] * SC_ROW_PARTS, cols)
    mesh = plsc.VectorSubcoreMesh(core_axis_name="core", subcore_axis_name="subcore")

    @pl.kernel(out_type=jax.ShapeDtypeStruct((n, cols), table.dtype), mesh=mesh, scratch_types=[])
    def gather(x_hbm, i_hbm, o_hbm):
        def body(i_vmem, o_vmem):
            pltpu.sync_copy(x_hbm.at[i_vmem.at[0]], o_vmem)

        pltpu.emit_pipeline(
            body,
            grid=(n // SC_WINDOW,),
            in_specs=[pl.BlockSpec((1, SC_WINDOW), lambda i: (0, i))],
            out_specs=[pl.BlockSpec((SC_WINDOW, cols), lambda i: (i, 0))],
            core_axis_name=("core", "subcore"),
            dimension_semantics=(pltpu.PARALLEL,),
        )(i_hbm, o_hbm)

    return gather(halves, idx).reshape(TOP_K, t, PACKED_COLS)


def _combine_rows_kernel(y0_ref, y1_ref, gcol_ref, x_ref, mod_ref, fg_ref, o_ref, *, final):
    gc = gcol_ref[...]
    ffn = gc[:, 0:1] * _unpack_rows(y0_ref[...]) + gc[:, 1:2] * _unpack_rows(y1_ref[...])
    x = x_ref[...] + mod_ref[...][5:6] * ffn
    if final:
        x = x * lax.rsqrt(jnp.mean(x * x, axis=-1, keepdims=True) + NORM_EPS) * fg_ref[...]
    o_ref[...] = x


def _combine_rows(yg, tile_off, gcol, xnew, mod_l, final_g, *, tiles_per_row, fixed_row, final):
    tm = MOE_TILE
    t = xnew.shape[0]
    if fixed_row is None:
        mod_map = lambda i: (i // tiles_per_row, 0, 0)
    else:
        mod_map = lambda i: (fixed_row, 0, 0)
    slot = lambda k: pl.BlockSpec((None, tm, PACKED_COLS), lambda i: (k, i + tile_off, 0))
    return pl.pallas_call(
        functools.partial(_combine_rows_kernel, final=final),
        out_shape=jax.ShapeDtypeStruct((t, D_MODEL), f32),
        grid=(t // tm,),
        in_specs=[slot(0), slot(1),
                  pl.BlockSpec((tm, LANES), lambda i: (i + tile_off, 0)),
                  pl.BlockSpec((tm, D_MODEL), lambda i: (i, 0)),
                  pl.BlockSpec((None, 6, D_MODEL), mod_map),
                  pl.BlockSpec((1, D_MODEL), lambda i: (0, 0))],
        out_specs=pl.BlockSpec((tm, D_MODEL), lambda i: (i, 0)),
        compiler_params=_cparams(("arbitrary",)),
        name="combine_rows",
    )(yg, yg, gcol, xnew, mod_l, final_g)


def _moe(h2_parts, logits, router_bias_col, su, w_gate, w_up, w_down, layer, buf):
    t = logits.shape[0]
    ids, rank, gcol, cnt = _route(logits, router_bias_col, su)
    counts = cnt[:, 0].astype(i32)
    padded = (counts + EXPERT_BLOCK - 1) // EXPERT_BLOCK * EXPERT_BLOCK
    pend = jnp.cumsum(padded)
    pstart = pend - padded
    nb_needed = (t * TOP_K) // EXPERT_BLOCK + N_EXPERTS
    if buf is None:
        buf = jnp.zeros((nb_needed * EXPERT_BLOCK, PACKED_COLS), u32)
    nb = buf.shape[0] // EXPERT_BLOCK
    assert nb >= nb_needed
    block_pos = jnp.arange(nb, dtype=i32) * EXPERT_BLOCK
    block_e = jnp.minimum(jnp.sum((pend[None, :] <= block_pos[:, None]).astype(i32), axis=1), N_EXPERTS - 1)
    n_used = (pend[-1:] // EXPERT_BLOCK).astype(i32)
    dest_tiles = _dest(ids, rank, pstart.astype(f32)[:, None])
    tile_off = 0
    for part in h2_parts:
        buf = _dispatch(dest_tiles, tile_off, part, buf)
        tile_off += part.shape[0] // MOE_TILE
    y = _experts(block_e, n_used, buf, w_gate, w_up, w_down, layer)
    return dest_tiles, gcol, y, buf


def _pack_layer(l, w_in, conv_w, conv_b, dt_bias_f, dt_bias_b, a_log_f, a_log_b, d_skip, ssd_norm_g,
                gk_up_f, gk_bias_f, gk_up_b, gk_bias_b, gla_norm_g, w_out, norm1_g, norm2_g):
    w = w_in[l]
    z, xbc, dt, q, k, v, go, gk = jnp.split(w, [512, 1280, 1288, 1544, 1800, 2312, 2824], axis=1)
    small = jnp.concatenate([dt, gk, jnp.zeros((D_MODEL, SMALL_COLS - 24), f32)], axis=1)
    w_packed = jnp.concatenate([z, xbc, q, k, v, go, small], axis=1).astype(bf16)

    def lane_row(vec):
        return jnp.zeros((1, LANES), f32).at[0, :SSD_HEADS].set(vec)

    def sub_col(vec):
        return jnp.broadcast_to(vec[:, None], (SSD_HEADS, CHUNK)).astype(f32)

    def gup(m):
        return jnp.zeros((SMALL_COLS, GLA_QK), f32).at[SSD_HEADS:SSD_HEADS + GLA_GATE_RANK].set(m).astype(bf16)

    a_f = -jnp.exp(a_log_f[l])
    a_b = -jnp.exp(a_log_b[l])
    return dict(
        w_packed=w_packed,
        conv_w8=jnp.zeros((8, SSD_XBC), f32).at[:5].set(conv_w[l]),
        conv_b=conv_b[l][None, :],
        ssd_f=(lane_row(dt_bias_f[l]), lane_row(a_f), sub_col(dt_bias_f[l]), sub_col(a_f)),
        ssd_b=(lane_row(dt_bias_b[l]), lane_row(a_b), sub_col(dt_bias_b[l]), sub_col(a_b)),
        gla_f=(gup(gk_up_f[l]), gk_bias_f[l][None, :]),
        gla_b=(gup(gk_up_b[l]), gk_bias_b[l][None, :]),
        dsk=jnp.repeat(d_skip[l], SSD_HEAD_DIM)[None, :],
        sg=ssd_norm_g[l][None, :],
        gg=jnp.tile(gla_norm_g[l], GLA_HEADS)[None, :],
        wout=w_out[l].astype(bf16),
        n1g=norm1_g[l][None, :],
        n2g=norm2_g[l][None, :],
    )


def _constants():
    r = jnp.arange(LANES)[:, None]
    c = jnp.arange(SSD_WIDTH)[None, :]
    e_mat = ((c // SSD_HEAD_DIM) == r).astype(bf16)
    ssd_mask = ((r // SSD_STATE) == (c // (SSD_WIDTH // SSD_GROUPS))).astype(f32)
    rr = jnp.arange(GLA_WIDTH)[:, None]
    cc = jnp.arange(GLA_QK)[None, :]
    gla_mask = ((rr // GLA_VAL_DIM) == (cc // GLA_KEY_DIM)).astype(f32)
    k = jnp.arange(ROUTE_TILE)
    su = (k[:, None] < k[None, :]).astype(bf16)
    return e_mat, ssd_mask, gla_mask, su


def _mixers(streams, pk, consts, nb):
    e_mat, ssd_mask, gla_mask, _ = consts
    out = {}
    h0_f = jnp.zeros((nb, LANES, SSD_WIDTH), f32)
    h0_b = h0_f
    s0_f = jnp.zeros((nb, GLA_WIDTH, GLA_QK), f32)
    s0_b = s0_f
    for name in ('ctx', 'lat'):
        z, xs, bc, q, k, v, go, sm = streams[name]
        yf, yb, h0_f, h0_b = _ssd_scan(xs, bc, sm, h0_f, h0_b, pk['ssd_f'], pk['ssd_b'], e_mat, ssd_mask, nb=nb)
        of, ob, s0_f, s0_b = _gla_scan(q, k, v, sm, s0_f, s0_b, pk['gla_f'], pk['gla_b'], gla_mask, nb=nb)
        out[name] = (yf, yb, of, ob)
    return out


def kernel(x, c, ctx, c_ctx, w_mod, b_mod, norm1_g, norm2_g, w_in, conv_w, conv_b, dt_bias_f, dt_bias_b, a_log_f, a_log_b, d_skip, ssd_norm_g, gk_up_f, gk_bias_f, gk_up_b, gk_bias_b, gla_norm_g, w_out, w_router, router_bias, w_gate, w_up, w_down, final_norm_g):
    nb, seq, d = x.shape
    ctx_len = ctx.shape[1]
    depth = w_mod.shape[0]
    consts = _constants()
    su = consts[3]

    cvec = jnp.zeros((16, d), f32).at[:nb].set(c).at[nb].set(c_ctx)
    mod = _modulation(cvec, w_mod, b_mod).reshape(depth, 16, 6, d)
    ctx_row = nb

    perm = jnp.array([g * EXPERTS_PER_GROUP + m for m in range(EXPERTS_PER_GROUP)
                      for g in range(N_EXPERT_GROUPS)], dtype=i32)
    wr32 = jnp.zeros((d, LANES), f32).at[:, :N_EXPERTS].set(w_router[:, perm])
    wr_hi = wr32.astype(bf16)
    wr = jnp.stack([wr_hi, (wr32 - wr_hi.astype(f32)).astype(bf16)])
    bias_col = router_bias[perm][:, None]

    x2 = x.reshape(nb * seq, d)
    c2 = ctx.reshape(nb * ctx_len, d)
    lat_tiles = seq // TM_LAT
    moe_buf = None

    for l in range(depth):
        last = l == depth - 1
        pk = _pack_layer(l, w_in, conv_w, conv_b, dt_bias_f, dt_bias_b, a_log_f, a_log_b, d_skip, ssd_norm_g,
                         gk_up_f, gk_bias_f, gk_up_b, gk_bias_b, gla_norm_g, w_out, norm1_g, norm2_g)
        mod_l = mod[l]
        streams = {
            'ctx': _inproj(c2, mod_l, pk['n1g'], pk['w_packed'], pk['conv_w8'], pk['conv_b'],
                           tm=ctx_len, rowlen=ctx_len, tiles_per_row=1, fixed_row=ctx_row),
            'lat': _inproj(x2, mod_l, pk['n1g'], pk['w_packed'], pk['conv_w8'], pk['conv_b'],
                           tm=TM_LAT, rowlen=GRID_W, tiles_per_row=lat_tiles, fixed_row=None),
        }
        mix = _mixers(streams, pk, consts, nb)

        def merge(name, xres, tm, tiles_per_row, fixed_row):
            z, xs, bc, q, k, v, go, sm = streams[name]
            yf, yb, of, ob = mix[name]
            return _outproj(yf, yb, xs, z, of, ob, go, xres, mod_l, pk['dsk'], pk['sg'], pk['gg'],
                            pk['wout'], pk['n2g'], wr, tm=tm, tiles_per_row=tiles_per_row, fixed_row=fixed_row)

        xn_lat, h2_lat, lg_lat = merge('lat', x2, TM_LAT, lat_tiles, None)
        if last:
            parts, logits = [h2_lat], lg_lat
        else:
            xn_ctx, h2_ctx, lg_ctx = merge('ctx', c2, ctx_len, 1, ctx_row)
            parts, logits = [h2_ctx, h2_lat], jnp.concatenate([lg_ctx, lg_lat], axis=0)
        dest_tiles, gcol, y, moe_buf = _moe(parts, logits, bias_col, su, w_gate, w_up, w_down, l, moe_buf)
        yg = _sc_gather_rows(y, dest_tiles)
        lat_off = 0
        if not last:
            c2 = _combine_rows(yg, 0, gcol, xn_ctx, mod_l, final_norm_g[None, :],
                               tiles_per_row=1, fixed_row=ctx_row, final=False)
            lat_off = xn_ctx.shape[0] // MOE_TILE
        x2 = _combine_rows(yg, lat_off, gcol, xn_lat, mod_l, final_norm_g[None, :],
                           tiles_per_row=seq // MOE_TILE, fixed_row=None, final=last)
    return x2.reshape(nb, seq, d)
```

```python
import functools

import jax
import jax.numpy as jnp
from jax import lax
from jax.experimental import pallas as pl
from jax.experimental.pallas import tpu as pltpu
from jax.experimental.pallas import tpu_sc as plsc

f32 = jnp.float32
bf16 = jnp.bfloat16
i32 = jnp.int32
u32 = jnp.uint32

D_MODEL = 1024
SSD_HEADS = 8
SSD_HEAD_DIM = 64
SSD_WIDTH = 512
SSD_GROUPS = 2
SSD_STATE = 64
SSD_XBC = 768
GLA_HEADS = 4
GLA_KEY_DIM = 64
GLA_VAL_DIM = 128
GLA_QK = 256
GLA_WIDTH = 512
GLA_GATE_RANK = 16
GLA_GATE_NORM = 16.0
GRID_W = 64
N_EXPERTS = 32
N_EXPERT_GROUPS = 8
EXPERTS_PER_GROUP = 4
TOP_K = 2
EXPERT_FF = 512
NORM_EPS = 1e-6

LANES = 128
CHUNK = 128
CHUNKS_PER_STEP = 4
INTERLEAVE_WAVE = 8
ROW_SPLIT = 2
SSD_SPLIT = 3
GLA_SPLIT = 2
MAIN_COLS = 2816
SMALL_COLS = LANES
TM_LAT = 512
ROUTE_TILE = 512
MOE_TILE = 512
ROW_DMA_UNROLL = 8
SC_WINDOW = 128
SC_ROW_PARTS = 2
SC_PART_COLS = D_MODEL // 2 // SC_ROW_PARTS
PACKED_COLS = D_MODEL // 2
EXPERT_BLOCK = 512
VMEM_LIMIT = 48 * 1024 * 1024

_HI = lax.Precision.HIGHEST
_NT = (((1,), (1,)), ((), ()))


def _dot(a, b, precision=None):
    return jnp.dot(a, b, preferred_element_type=f32, precision=precision)


def _silu(x):
    return x * jax.nn.sigmoid(x)


def _row_spec(rows, block_index):
    return pl.BlockSpec((rows, PACKED_COLS), lambda *idx: (block_index(*idx), 0))


def _pack_rows(value):
    hi = pltpu.bitcast(value[:, :PACKED_COLS].astype(bf16).astype(f32), u32)
    lo = pltpu.bitcast(value[:, PACKED_COLS:].astype(bf16).astype(f32), u32)
    return hi | (lo >> 16)


def _unpack_rows(packed):
    hi = pltpu.bitcast(packed & jnp.uint32(0xFFFF0000), f32)
    lo = pltpu.bitcast(packed << 16, f32)
    return jnp.concatenate([hi, lo], axis=1)


def _cparams(sem):
    return pltpu.CompilerParams(dimension_semantics=sem, vmem_limit_bytes=VMEM_LIMIT)


def _mod_kernel(c_ref, w_ref, b_ref, o_ref):
    sc = _silu(c_ref[...]).astype(bf16)
    o_ref[...] = _dot(sc, w_ref[...].astype(bf16)) + b_ref[...]


def _modulation(cvec, w_mod, b_mod):
    depth, d, n = w_mod.shape
    tn = 1536
    return pl.pallas_call(
        _mod_kernel,
        out_shape=jax.ShapeDtypeStruct((depth, 16, n), f32),
        grid=(depth, n // tn),
        in_specs=[pl.BlockSpec((16, d), lambda l, j: (0, 0)),
                  pl.BlockSpec((None, d, tn), lambda l, j: (l, 0, j)),
                  pl.BlockSpec((None, 1, tn), lambda l, j: (l, 0, j))],
        out_specs=pl.BlockSpec((None, 16, tn), lambda l, j: (l, 0, j)),
        compiler_params=_cparams(("arbitrary", "arbitrary")),
        name="modulation",
    )(cvec, w_mod, b_mod.reshape(depth, 1, n))


def _inproj_kernel(x_ref, mod_ref, g_ref, w_ref, cw_ref, cb_ref,
                   z_ref, xs_ref, bc_ref, q_ref, k_ref, v_ref, go_ref, sm_ref, *, rowlen):
    tm = x_ref.shape[0]
    split = ROW_SPLIT if (tm // ROW_SPLIT) % rowlen == 0 else 1
    sub = tm // split
    m = mod_ref[...]
    geff = g_ref[...] * (1.0 + m[1:2])
    cw = cw_ref[...]

    def rows_gen(rows):
        x = x_ref[rows, :]
        inv = lax.rsqrt(jnp.mean(x * x, axis=-1, keepdims=True) + NORM_EPS)
        h = ((x * inv) * geff + m[0:1]).astype(bf16)
        yield

        def proj(lo, hi):
            return _dot(h, w_ref[:, lo:hi])

        z_ref[rows, :] = proj(0, 512).astype(bf16)
        yield

        xbc = proj(512, 1280)
        yield
        pos = lax.broadcasted_iota(i32, xbc.shape, 0) & (rowlen - 1)
        acc = xbc * cw[2:3]
        for d in (-2, -1, 1, 2):
            shifted = pltpu.roll(xbc, (-d) % sub, 0)
            valid = (pos >= -d) if d < 0 else (pos <= rowlen - 1 - d)
            acc = acc + jnp.where(valid, shifted, 0.0) * cw[2 + d:3 + d]
        yield
        act = _silu(acc + cb_ref[...])
        xs_ref[rows, :] = act[:, :SSD_WIDTH].astype(bf16)
        bc_ref[rows, :] = act[:, SSD_WIDTH:].astype(bf16)
        yield

        qk = proj(1280, 1792)
        q_ref[rows, :] = qk[:, :GLA_QK].astype(bf16)
        k_ref[rows, :] = qk[:, GLA_QK:].astype(bf16)
        yield
        v_ref[rows, :] = proj(1792, 2304).astype(bf16)
        yield
        go_ref[rows, :] = proj(2304, 2816).astype(bf16)
        yield
        sm_ref[rows, :] = proj(2816, 2944)

    _run_interleaved(rows_gen(pl.ds(r * sub, sub)) for r in range(split))


def _inproj(x2d, mod_l, norm_g, w_packed, conv_w8, conv_b, *, tm, rowlen, tiles_per_row, fixed_row):
    t = x2d.shape[0]
    nt = t // tm
    if fixed_row is None:
        mod_map = lambda i: (i // tiles_per_row, 0, 0)
    else:
        mod_map = lambda i: (fixed_row, 0, 0)
    tok = lambda w: pl.BlockSpec((tm, w), lambda i: (i, 0))
    const = lambda a: pl.BlockSpec(a.shape, lambda i: (0,) * a.ndim)
    widths = (512, 512, 256, 256, 256, 512, 512)
    return pl.pallas_call(
        functools.partial(_inproj_kernel, rowlen=rowlen),
        out_shape=tuple(jax.ShapeDtypeStruct((t, w), bf16) for w in widths)
        + (jax.ShapeDtypeStruct((t, SMALL_COLS), f32),),
        grid=(nt,),
        in_specs=[tok(D_MODEL), pl.BlockSpec((None, 6, D_MODEL), mod_map), const(norm_g),
                  const(w_packed), const(conv_w8), const(conv_b)],
        out_specs=tuple(tok(w) for w in widths) + (tok(SMALL_COLS),),
        compiler_params=_cparams(("arbitrary",)),
        name="inproj",
    )(x2d, mod_l, norm_g, w_packed, conv_w8, conv_b)


def _dot_split(a, b, parts, *, split_lhs):
    rest = a if split_lhs else b
    acc = None
    for _ in range(parts):
        piece = rest.astype(bf16)
        rest = rest - piece.astype(f32)
        term = _dot(piece, b) if split_lhs else _dot(a, piece)
        acc = term if acc is None else acc + term
    return acc


def _tri_masks(q):
    r = lax.broadcasted_iota(i32, (q, q), 0)
    c = lax.broadcasted_iota(i32, (q, q), 1)
    return r >= c, r <= c


def _ssd_chunk(load, store, carry, par, e, bm, rev):
    xb, bcv, sm = load()
    dtb, a_lane, dtbt, a_sub = par
    q = xb.shape[0]
    dt_col = jax.nn.softplus(sm + dtb)
    a_col = dt_col * a_lane
    dt_row = jax.nn.softplus(sm.T[0:SSD_HEADS, :] + dtbt)
    a_row = dt_row * a_sub
    yield
    lower, upper = _tri_masks(q)
    lo_b = jnp.where(lower, 1.0, 0.0).astype(bf16)
    up_b = jnp.where(upper, 1.0, 0.0).astype(bf16)
    if not rev:
        cs_col = _dot_split(lo_b, a_col, SSD_SPLIT, split_lhs=False)
        cs_row = _dot_split(a_row, up_b, SSD_SPLIT, split_lhs=True)
        mask = lower
        a_tot = cs_col[q - 1:q, :]
    else:
        cs_col = _dot_split(up_b, a_col, SSD_SPLIT, split_lhs=False)
        cs_row = _dot_split(a_row, lo_b, SSD_SPLIT, split_lhs=True)
        mask = upper
        a_tot = cs_col[0:1, :]
    yield

    w_exp = _dot((jnp.exp(a_tot - cs_col) * dt_col).astype(bf16), e)
    ecs_exp = _dot(jnp.exp(cs_col).astype(bf16), e)
    dec_exp = _dot_split(jnp.broadcast_to(jnp.exp(a_tot), (8, LANES)), e, SSD_SPLIT, split_lhs=True)[0:1]
    yield

    xw = (xb.astype(f32) * w_exp).astype(bf16)
    b_all = bcv[:, 0:LANES]
    c_all = bcv[:, LANES:2 * LANES]
    b_t = b_all.astype(f32).T.astype(bf16)
    yield
    s_upd = _dot(b_t, xw)
    yield

    lane = lax.broadcasted_iota(i32, (q, LANES), 1)
    zero_b = jnp.zeros((q, LANES), bf16)
    ys = []
    for g in range(SSD_GROUPS):
        in_g = (lane >= SSD_STATE * g) & (lane < SSD_STATE * (g + 1))
        cb = _dot(jnp.where(in_g, c_all, zero_b), b_t)
        yield
        for pp in range(2):
            h0 = 4 * g + 2 * pp
            ms = []
            for h in (h0, h0 + 1):
                seg = cs_col[:, h:h + 1] - cs_row[h:h + 1, :]
                dec = jnp.exp(jnp.where(mask, seg, -1e30))
                ms.append((cb * dec * dt_row[h:h + 1, :]).astype(bf16))
            xp = xb[:, h0 * SSD_HEAD_DIM:h0 * SSD_HEAD_DIM + LANES]
            rhs = jnp.concatenate([jnp.where(lane < SSD_HEAD_DIM, xp, zero_b),
                                   jnp.where(lane >= SSD_HEAD_DIM, xp, zero_b)], axis=0)
            yield
            ys.append(_dot(jnp.concatenate(ms, axis=1), rhs))
            yield
    y_intra = jnp.concatenate(ys, axis=1)
    yield

    s_old = carry['s']
    carry['s'] = (s_old * dec_exp + s_upd) * bm
    store((y_intra + _dot(c_all, s_old.astype(bf16)) * ecs_exp).astype(bf16))


def _run_interleaved(gens):
    gens = list(gens)
    done = object()
    for w in range(0, len(gens), INTERLEAVE_WAVE):
        live = gens[w:w + INTERLEAVE_WAVE]
        while live:
            live = [g for g in live if next(g, done) is not done]


def _chunk_rows(cps):
    return [(pl.ds(ci * CHUNK, CHUNK), pl.ds((cps - 1 - ci) * CHUNK, CHUNK)) for ci in range(cps)]


def _ssd_kernel(xf_ref, bcf_ref, smf_ref, xr_ref, bcr_ref, smr_ref, h0f_ref, h0r_ref,
                dtbf_ref, af_ref, dtbtf_ref, atf_ref, dtbr_ref, ar_ref, dtbtr_ref, atr_ref, e_ref, bm_ref,
                yf_ref, yr_ref, hff_ref, hfr_ref, st_f, st_r, *, cps, nblk):
    j = pl.program_id(1)

    @pl.when(j == 0)
    def _():
        st_f[...] = h0f_ref[...]
        st_r[...] = h0r_ref[...]

    e = e_ref[...]
    bm = bm_ref[...]
    par_f = (dtbf_ref[...], af_ref[...], dtbtf_ref[...], atf_ref[...])
    par_r = (dtbr_ref[...], ar_ref[...], dtbtr_ref[...], atr_ref[...])
    carry_f = {'s': st_f[...]}
    carry_r = {'s': st_r[...]}

    def chunk(refs, out_ref, rows, carry, par, rev):
        def store(y):
            out_ref[rows, :] = y
        return _ssd_chunk(lambda: tuple(r[rows, :] for r in refs), store, carry, par, e, bm, rev)

    gens = []
    for rows_f, rows_r in _chunk_rows(cps):
        gens.append(chunk((xf_ref, bcf_ref, smf_ref), yf_ref, rows_f, carry_f, par_f, False))
        gens.append(chunk((xr_ref, bcr_ref, smr_ref), yr_ref, rows_r, carry_r, par_r, True))
    _run_interleaved(gens)
    st_f[...] = carry_f['s']
    st_r[...] = carry_r['s']

    @pl.when(j == nblk - 1)
    def _():
        hff_ref[...] = carry_f['s']
        hfr_ref[...] = carry_r['s']


def _scan_specs(t, nb):
    per_row = t // nb
    cps = min(CHUNKS_PER_STEP, per_row // CHUNK)
    nblk = per_row // (cps * CHUNK)
    fmap = lambda b, j: (b * nblk + j, 0)
    rmap = lambda b, j: (b * nblk + nblk - 1 - j, 0)
    return cps, nblk, fmap, rmap


def _ssd_scan(xs, bc, sm, h0_f, h0_r, par_f, par_r, e_mat, bmask, *, nb):
    t = xs.shape[0]
    cps, nblk, fmap, rmap = _scan_specs(t, nb)
    blk = cps * CHUNK
    const = lambda a: pl.BlockSpec(a.shape, lambda b, j: (0,) * a.ndim)
    st = pl.BlockSpec((None, LANES, SSD_WIDTH), lambda b, j: (b, 0, 0))
    widths = (SSD_WIDTH, 2 * LANES, SMALL_COLS)
    tok_in = [pl.BlockSpec((blk, w), m) for m in (fmap, rmap) for w in widths]
    consts = (*par_f, *par_r, e_mat, bmask)
    return pl.pallas_call(
        functools.partial(_ssd_kernel, cps=cps, nblk=nblk),
        out_shape=(jax.ShapeDtypeStruct((t, SSD_WIDTH), bf16), jax.ShapeDtypeStruct((t, SSD_WIDTH), bf16),
                   jax.ShapeDtypeStruct((nb, LANES, SSD_WIDTH), f32),
                   jax.ShapeDtypeStruct((nb, LANES, SSD_WIDTH), f32)),
        grid=(nb, nblk),
        in_specs=tok_in + [st, st] + [const(a) for a in consts],
        out_specs=(pl.BlockSpec((blk, SSD_WIDTH), fmap), pl.BlockSpec((blk, SSD_WIDTH), rmap), st, st),
        scratch_shapes=[pltpu.VMEM((LANES, SSD_WIDTH), f32), pltpu.VMEM((LANES, SSD_WIDTH), f32)],
        compiler_params=_cparams(("arbitrary", "arbitrary")),
        name="ssd",
    )(xs, bc, sm, xs, bc, sm, h0_f, h0_r, *consts)


def _gla_chunk(load, store, carry, par, bm, rev):
    qb, kb, vb, sm = load()
    gup, gbias = par
    q = qb.shape[0]
    gp = _dot(sm.astype(bf16), gup) + gbias
    yield
    g = jax.nn.log_sigmoid(gp) * (1.0 / GLA_GATE_NORM)
    yield
    lower, upper = _tri_masks(q)
    if not rev:
        b = _dot_split(jnp.where(lower, 1.0, 0.0).astype(bf16), g, GLA_SPLIT, split_lhs=False)
        mask, mid = lower, q // 2 - 1
        b_tot = b[q - 1:q, :]
    else:
        b = _dot_split(jnp.where(upper, 1.0, 0.0).astype(bf16), g, GLA_SPLIT, split_lhs=False)
        mask, mid = upper, q // 2
        b_tot = b[0:1, :]
    b_mid = b[mid:mid + 1, :]
    yield

    qf = qb.astype(f32) * (GLA_KEY_DIM ** -0.5)
    kf = kb.astype(f32)
    qd = (qf * jnp.exp(b - b_mid)).astype(bf16)
    yield
    ki = (kf * jnp.exp(b_mid - b)).astype(bf16)
    yield
    q_st = (qf * jnp.exp(b)).astype(bf16)
    yield
    k_end = (kf * jnp.exp(b_tot - b)).astype(bf16)
    dec = jnp.exp(b_tot)
    yield
    v_t = vb.astype(f32).T.astype(bf16)
    yield
    s_upd = _dot(v_t, k_end)
    yield

    lane = lax.broadcasted_iota(i32, (q, LANES), 1)
    zero_b = jnp.zeros((q, LANES), bf16)
    outs = []
    for h in range(GLA_HEADS):
        p, hh = divmod(h, 2)
        qp = qd[:, p * LANES:(p + 1) * LANES]
        kp = ki[:, p * LANES:(p + 1) * LANES]
        in_h = (lane >= GLA_KEY_DIM * hh) & (lane < GLA_KEY_DIM * (hh + 1))
        s = lax.dot_general(jnp.where(in_h, qp, zero_b), kp, _NT, preferred_element_type=f32)
        yield
        attn = jnp.where(mask, s, 0.0).astype(bf16)
        outs.append(_dot(attn, vb[:, h * GLA_VAL_DIM:(h + 1) * GLA_VAL_DIM]))
        yield
    o_intra = jnp.concatenate(outs, axis=1)
    yield

    s_old = carry['s']
    carry['s'] = (s_old * dec + s_upd) * bm
    o_inter = lax.dot_general(q_st, s_old.astype(bf16), _NT, preferred_element_type=f32)
    store((o_intra + o_inter).astype(bf16))


def _gla_kernel(qf_ref, kf_ref, vf_ref, smf_ref, qr_ref, kr_ref, vr_ref, smr_ref, s0f_ref, s0r_ref,
                gupf_ref, gbf_ref, gupr_ref, gbr_ref, bm_ref,
                of_ref, or_ref, sff_ref, sfr_ref, st_f, st_r, *, cps, nblk):
    j = pl.program_id(1)

    @pl.when(j == 0)
    def _():
        st_f[...] = s0f_ref[...]
        st_r[...] = s0r_ref[...]

    bm = bm_ref[...]
    par_f = (gupf_ref[...], gbf_ref[...])
    par_r = (gupr_ref[...], gbr_ref[...])
    carry_f = {'s': st_f[...]}
    carry_r = {'s': st_r[...]}

    def chunk(refs, out_ref, rows, carry, par, rev):
        def store(o):
            out_ref[rows, :] = o
        return _gla_chunk(lambda: tuple(r[rows, :] for r in refs), store, carry, par, bm, rev)

    gens = []
    for rows_f, rows_r in _chunk_rows(cps):
        gens.append(chunk((qf_ref, kf_ref, vf_ref, smf_ref), of_ref, rows_f, carry_f, par_f, False))
        gens.append(chunk((qr_ref, kr_ref, vr_ref, smr_ref), or_ref, rows_r, carry_r, par_r, True))
    _run_interleaved(gens)
    st_f[...] = carry_f['s']
    st_r[...] = carry_r['s']

    @pl.when(j == nblk - 1)
    def _():
        sff_ref[...] = carry_f['s']
        sfr_ref[...] = carry_r['s']


def _gla_scan(qa, ka, va, sm, s0_f, s0_r, par_f, par_r, bmask, *, nb):
    t = qa.shape[0]
    cps, nblk, fmap, rmap = _scan_specs(t, nb)
    blk = cps * CHUNK
    const = lambda a: pl.BlockSpec(a.shape, lambda b, j: (0,) * a.ndim)
    st = pl.BlockSpec((None, GLA_WIDTH, GLA_QK), lambda b, j: (b, 0, 0))
    widths = (GLA_QK, GLA_QK, GLA_WIDTH, SMALL_COLS)
    tok_in = [pl.BlockSpec((blk, w), m) for m in (fmap, rmap) for w in widths]
    consts = (*par_f, *par_r, bmask)
    return pl.pallas_call(
        functools.partial(_gla_kernel, cps=cps, nblk=nblk),
        out_shape=(jax.ShapeDtypeStruct((t, GLA_WIDTH), bf16), jax.ShapeDtypeStruct((t, GLA_WIDTH), bf16),
                   jax.ShapeDtypeStruct((nb, GLA_WIDTH, GLA_QK), f32),
                   jax.ShapeDtypeStruct((nb, GLA_WIDTH, GLA_QK), f32)),
        grid=(nb, nblk),
        in_specs=tok_in + [st, st] + [const(a) for a in consts],
        out_specs=(pl.BlockSpec((blk, GLA_WIDTH), fmap), pl.BlockSpec((blk, GLA_WIDTH), rmap), st, st),
        scratch_shapes=[pltpu.VMEM((GLA_WIDTH, GLA_QK), f32), pltpu.VMEM((GLA_WIDTH, GLA_QK), f32)],
        compiler_params=_cparams(("arbitrary", "arbitrary")),
        name="gla",
    )(qa, ka, va, sm, qa, ka, va, sm, s0_f, s0_r, *consts)


def _outproj_kernel(yf_ref, yb_ref, xs_ref, z_ref, of_ref, ob_ref, go_ref, xres_ref, mod_ref,
                    dsk_ref, sg_ref, gg_ref, wout_ref, n2g_ref, wr_ref,
                    xnew_ref, h2_ref, lg_ref):
    tm = xres_ref.shape[0]
    sub = tm // ROW_SPLIT
    m = mod_ref[...]
    h2_gain = n2g_ref[...] * (1.0 + m[4:5])

    def rows_gen(rows):
        up = lambda r: r[rows, :].astype(f32)
        y = up(yf_ref) + up(yb_ref) + dsk_ref[...] * up(xs_ref)
        yield
        y = y * _silu(up(z_ref))
        yield
        y = y * lax.rsqrt(jnp.mean(y * y, axis=-1, keepdims=True) + NORM_EPS) * sg_ref[...]
        yield
        o = up(of_ref) + up(ob_ref)
        parts = []
        for h in range(GLA_HEADS):
            oh = o[:, h * GLA_VAL_DIM:(h + 1) * GLA_VAL_DIM]
            parts.append(oh * lax.rsqrt(jnp.mean(oh * oh, axis=-1, keepdims=True) + NORM_EPS))
        yield
        o = jnp.concatenate(parts, axis=1) * gg_ref[...] * _silu(up(go_ref))
        yield
        cat = jnp.concatenate([y, o], axis=1).astype(bf16)
        mix = _dot(cat, wout_ref[...])
        yield
        xn = xres_ref[rows, :] + m[2:3] * mix
        xnew_ref[rows, :] = xn
        yield
        inv = lax.rsqrt(jnp.mean(xn * xn, axis=-1, keepdims=True) + NORM_EPS)
        h2 = (xn * inv) * h2_gain + m[3:4]
        h2_ref[rows, :] = _pack_rows(h2)
        yield
        h2_hi = h2.astype(bf16)
        h2_lo = (h2 - h2_hi.astype(f32)).astype(bf16)
        wr_hi = wr_ref[0]
        lg_ref[rows, :] = _dot(h2_hi, wr_hi) + _dot(h2_hi, wr_ref[1]) + _dot(h2_lo, wr_hi)

    _run_interleaved(rows_gen(pl.ds(r * sub, sub)) for r in range(ROW_SPLIT))


def _outproj(yf, yb, xs, z, of, ob, go, xres, mod_l, dsk, sg, gg, wout, n2g, wr,
             *, tm, tiles_per_row, fixed_row):
    t = xres.shape[0]
    if fixed_row is None:
        mod_map = lambda i: (i // tiles_per_row, 0, 0)
    else:
        mod_map = lambda i: (fixed_row, 0, 0)
    tok = lambda w: pl.BlockSpec((tm, w), lambda i: (i, 0))
    const = lambda a: pl.BlockSpec(a.shape, lambda i: (0,) * a.ndim)
    return pl.pallas_call(
        _outproj_kernel,
        out_shape=(jax.ShapeDtypeStruct((t, D_MODEL), f32), jax.ShapeDtypeStruct((t, PACKED_COLS), u32),
                   jax.ShapeDtypeStruct((t, LANES), f32)),
        grid=(t // tm,),
        in_specs=[tok(512)] * 7 + [tok(D_MODEL), pl.BlockSpec((None, 6, D_MODEL), mod_map),
                                   const(dsk), const(sg), const(gg), const(wout), const(n2g), const(wr)],
        out_specs=(tok(D_MODEL), _row_spec(tm, lambda i: i), tok(LANES)),
        compiler_params=_cparams(("arbitrary",)),
        name="outproj",
    )(yf, yb, xs, z, of, ob, go, xres, mod_l, dsk, sg, gg, wout, n2g, wr)


def _route_kernel(lg_ref, bias_ref, su_ref, ids_ref, rank_ref, gcol_ref, cnt_ref, carry):
    i = pl.program_id(0)
    tm = lg_ref.shape[0]

    @pl.when(i == 0)
    def _():
        carry[...] = jnp.zeros_like(carry)

    s = jax.nn.sigmoid(lg_ref[...].T[0:N_EXPERTS, :])
    sel = s + bias_ref[...]
    a = [sel[8 * m:8 * (m + 1)] for m in range(EXPERTS_PER_GROUP)]
    sv = [s[8 * m:8 * (m + 1)] for m in range(EXPERTS_PER_GROUP)]
    hi01, lo01 = jnp.maximum(a[0], a[1]), jnp.minimum(a[0], a[1])
    hi23, lo23 = jnp.maximum(a[2], a[3]), jnp.minimum(a[2], a[3])
    gscore = jnp.maximum(hi01, hi23) + jnp.maximum(jnp.minimum(hi01, hi23), jnp.maximum(lo01, lo23))
    giota = lax.broadcasted_iota(i32, gscore.shape, 0)
    gmax = jnp.max(gscore, axis=0, keepdims=True)
    gidx = jnp.min(jnp.where(gscore == gmax, giota, N_EXPERT_GROUPS), axis=0, keepdims=True)
    pick = giota == gidx
    v = [jnp.sum(jnp.where(pick, a[m], 0.0), axis=0, keepdims=True) for m in range(4)]
    w = [jnp.sum(jnp.where(pick, sv[m], 0.0), axis=0, keepdims=True) for m in range(4)]

    def first_max(vals, excluded):
        best = vals[0]
        for m in range(1, 4):
            best = jnp.maximum(best, vals[m])
        idx = jnp.full(best.shape, 3, i32)
        for m in (2, 1, 0):
            hit = vals[m] == best
            if excluded is not None:
                hit = hit & (excluded != m)
            idx = jnp.where(hit, m, idx)
        return idx

    i1 = first_max(v, None)
    v_rest = [jnp.where(i1 == m, -jnp.inf, v[m]) for m in range(4)]
    i2 = first_max(v_rest, i1)

    def take(vals, idx):
        out = vals[3]
        for m in (2, 1, 0):
            out = jnp.where(idx == m, vals[m], out)
        return out

    w1, w2 = take(w, i1), take(w, i2)
    denom = w1 + w2
    id1 = gidx * EXPERTS_PER_GROUP + i1
    id2 = gidx * EXPERTS_PER_GROUP + i2
    row2 = lax.broadcasted_iota(i32, (TOP_K, tm), 0)
    ids_ref[...] = jnp.where(row2 == 0, id1, id2)

    eiota = lax.broadcasted_iota(i32, (N_EXPERTS, tm), 0)
    hit1, hit2 = eiota == id1, eiota == id2
    onehot = jnp.where(hit1, 1.0, 0.0) + jnp.where(hit2, 1.0, 0.0)
    before = _dot(onehot.astype(bf16), su_ref[...]) + carry[...]
    r1 = jnp.sum(jnp.where(hit1, before, 0.0), axis=0, keepdims=True)
    r2 = jnp.sum(jnp.where(hit2, before, 0.0), axis=0, keepdims=True)
    rank_ref[...] = jnp.where(row2 == 0, r1, r2).astype(i32)
    new_carry = carry[...] + jnp.sum(onehot, axis=1, keepdims=True)
    carry[...] = new_carry
    cnt_ref[...] = jnp.broadcast_to(new_carry, cnt_ref.shape)

    rows = lax.broadcasted_iota(i32, (LANES, tm), 0)
    gates = jnp.where(rows == 0, w1 / denom, jnp.where(rows == 1, w2 / denom, 0.0))
    gcol_ref[...] = gates.T


def _route(logits, bias_col, su):
    t = logits.shape[0]
    tm = ROUTE_TILE
    return pl.pallas_call(
        _route_kernel,
        out_shape=(jax.ShapeDtypeStruct((2, t), i32), jax.ShapeDtypeStruct((2, t), i32),
                   jax.ShapeDtypeStruct((t, LANES), f32), jax.ShapeDtypeStruct((N_EXPERTS, LANES), f32)),
        grid=(t // tm,),
        in_specs=[pl.BlockSpec((tm, LANES), lambda i: (i, 0)),
                  pl.BlockSpec((N_EXPERTS, 1), lambda i: (0, 0)),
                  pl.BlockSpec((tm, tm), lambda i: (0, 0))],
        out_specs=(pl.BlockSpec((2, tm), lambda i: (0, i)), pl.BlockSpec((2, tm), lambda i: (0, i)),
                   pl.BlockSpec((tm, LANES), lambda i: (i, 0)),
                   pl.BlockSpec((N_EXPERTS, LANES), lambda i: (0, 0))),
        scratch_shapes=[pltpu.VMEM((N_EXPERTS, 1), f32)],
        compiler_params=_cparams(("arbitrary",)),
        name="route",
    )(logits, bias_col, su)


def _row_copy(src_ref, src_row, dst_ref, dst_row, sem):
    return pltpu.make_async_copy(src_ref.at[pl.ds(src_row, 1)], dst_ref.at[pl.ds(dst_row, 1)], sem)


def _dest_kernel(ids_ref, rank_ref, pstart_ref, dest_ref):
    ids = ids_ref[...]
    tm = ids.shape[1]
    eiota = lax.broadcasted_iota(i32, (N_EXPERTS, tm), 0)
    ps = pstart_ref[...]
    rows = [jnp.sum(jnp.where(eiota == ids[k:k + 1, :], ps, 0.0), axis=0, keepdims=True) for k in range(TOP_K)]
    row2 = lax.broadcasted_iota(i32, (TOP_K, tm), 0)
    dest_ref[...] = jnp.where(row2 == 0, rows[0], rows[1]).astype(i32) + rank_ref[...]


def _dest(ids, rank, pstart_col):
    t = ids.shape[1]
    tm = MOE_TILE
    return pl.pallas_call(
        _dest_kernel,
        out_shape=jax.ShapeDtypeStruct((t // tm, TOP_K, tm), i32),
        grid=(t // tm,),
        in_specs=[pl.BlockSpec((TOP_K, tm), lambda i: (0, i)), pl.BlockSpec((TOP_K, tm), lambda i: (0, i)),
                  pl.BlockSpec((N_EXPERTS, 1), lambda i: (0, 0))],
        out_specs=pl.BlockSpec((None, TOP_K, tm), lambda i: (i, 0, 0)),
        compiler_params=_cparams(("arbitrary",)),
        name="dest",
    )(ids, rank, pstart_col)


def _dispatch_kernel(dest_ref, h_ref, buf_in_ref, buf_ref, sem):
    del buf_in_ref
    tm = dest_ref.shape[1]

    def issue(j, carry):
        _row_copy(h_ref, j, buf_ref, dest_ref[0, j], sem).start(priority=0)
        _row_copy(h_ref, j, buf_ref, dest_ref[1, j], sem).start(priority=1)
        return carry

    lax.fori_loop(0, tm, issue, 0, unroll=ROW_DMA_UNROLL)

    def drain(j, carry):
        _row_copy(h_ref, 0, buf_ref, 0, sem).wait()
        _row_copy(h_ref, 0, buf_ref, 0, sem).wait()
        return carry

    lax.fori_loop(0, tm, drain, 0, unroll=ROW_DMA_UNROLL)


def _dispatch(dest_tiles, tile_off, h2, buf):
    tm = dest_tiles.shape[2]
    return pl.pallas_call(
        _dispatch_kernel,
        out_shape=jax.ShapeDtypeStruct(buf.shape, buf.dtype),
        grid=(h2.shape[0] // tm,),
        in_specs=[pl.BlockSpec((None, TOP_K, tm), lambda i: (i + tile_off, 0, 0), memory_space=pltpu.SMEM),
                  _row_spec(tm, lambda i: i), pl.BlockSpec(memory_space=pl.ANY)],
        out_specs=pl.BlockSpec(memory_space=pl.ANY),
        scratch_shapes=[pltpu.SemaphoreType.DMA],
        input_output_aliases={2: 0},
        compiler_params=_cparams(("arbitrary",)),
        name="dispatch",
    )(dest_tiles, h2, buf)


def _expert_kernel(be_ref, nu_ref, x_ref, wg_ref, wu_ref, wd_ref, y_ref, wg_b, wu_b, wd_b):
    i = pl.program_id(0)
    fresh = jnp.logical_or(i == 0, be_ref[i] != be_ref[jnp.maximum(i - 1, 0)])

    @pl.when(jnp.logical_and(fresh, i < nu_ref[0]))
    def _():
        wg_b[...] = wg_ref[...].astype(bf16)
        wu_b[...] = wu_ref[...].astype(bf16)
        wd_b[...] = wd_ref[...].astype(bf16)

    @pl.when(i < nu_ref[0])
    def _():
        sub = x_ref.shape[0] // ROW_SPLIT

        def rows_gen(rows):
            xb = _unpack_rows(x_ref[rows, :]).astype(bf16)
            yield
            g = _dot(xb, wg_b[...])
            yield
            u = _dot(xb, wu_b[...])
            yield
            hmid = (_silu(g) * u).astype(bf16)
            yield
            packed = _pack_rows(_dot(hmid, wd_b[...]))
            for part in range(SC_ROW_PARTS):
                y_ref[part, rows, :] = packed[:, part * SC_PART_COLS:(part + 1) * SC_PART_COLS]

        _run_interleaved(rows_gen(pl.ds(r * sub, sub)) for r in range(ROW_SPLIT))

    @pl.when(i >= nu_ref[0])
    def _():
        y_ref[...] = jnp.zeros_like(y_ref)


def _experts(block_e, n_used, buf, w_gate, w_up, w_down, layer):
    p = buf.shape[0]
    nb = p // EXPERT_BLOCK
    d, ff = w_gate.shape[-2:]
    row_blocks = _row_spec(EXPERT_BLOCK, lambda i, be, nu: i)
    return pl.pallas_call(
        _expert_kernel,
        out_shape=jax.ShapeDtypeStruct((SC_ROW_PARTS, p, SC_PART_COLS), u32),
        grid_spec=pltpu.PrefetchScalarGridSpec(
            num_scalar_prefetch=2,
            grid=(nb,),
            in_specs=[row_blocks,
                      pl.BlockSpec((None, None, d, ff), lambda i, be, nu: (layer, be[i], 0, 0)),
                      pl.BlockSpec((None, None, d, ff), lambda i, be, nu: (layer, be[i], 0, 0)),
                      pl.BlockSpec((None, None, ff, d), lambda i, be, nu: (layer, be[i], 0, 0))],
            out_specs=pl.BlockSpec((SC_ROW_PARTS, EXPERT_BLOCK, SC_PART_COLS), lambda i, be, nu: (0, i, 0)),
            scratch_shapes=[pltpu.VMEM((d, ff), bf16), pltpu.VMEM((d, ff), bf16), pltpu.VMEM((ff, d), bf16)],
        ),
        compiler_params=_cparams(("arbitrary",)),
        name="experts",
    )(block_e, n_used, buf, w_gate, w_up, w_down)


def _combine_kernel(dest_ref, y_ref, gcol_ref, x_ref, mod_ref, fg_ref, o_ref, ybuf, sem, *, final):
    tm = dest_ref.shape[1]

    def issue(j, carry):
        _row_copy(y_ref, dest_ref[0, j], ybuf.at[0], j, sem).start(priority=0)
        _row_copy(y_ref, dest_ref[1, j], ybuf.at[1], j, sem).start(priority=1)
        return carry

    lax.fori_loop(0, tm, issue, 0, unroll=ROW_DMA_UNROLL)

    def drain(j, carry):
        _row_copy(y_ref, 0, ybuf.at[0], 0, sem).wait()
        _row_copy(y_ref, 0, ybuf.at[1], 0, sem).wait()
        return carry

    lax.fori_loop(0, tm, drain, 0, unroll=ROW_DMA_UNROLL)

    gc = gcol_ref[...]
    ffn = gc[:, 0:1] * _unpack_rows(ybuf[0]) + gc[:, 1:2] * _unpack_rows(ybuf[1])
    x = x_ref[...] + mod_ref[...][5:6] * ffn
    if final:
        x = x * lax.rsqrt(jnp.mean(x * x, axis=-1, keepdims=True) + NORM_EPS) * fg_ref[...]
    o_ref[...] = x


def _combine(dest_tiles, tile_off, y, gcol, xnew, mod_l, final_g, *, tiles_per_row, fixed_row, final):
    tm = dest_tiles.shape[2]
    t = xnew.shape[0]
    if fixed_row is None:
        mod_map = lambda i: (i // tiles_per_row, 0, 0)
    else:
        mod_map = lambda i: (fixed_row, 0, 0)
    return pl.pallas_call(
        functools.partial(_combine_kernel, final=final),
        out_shape=jax.ShapeDtypeStruct((t, D_MODEL), f32),
        grid=(t // tm,),
        in_specs=[pl.BlockSpec((None, TOP_K, tm), lambda i: (i + tile_off, 0, 0), memory_space=pltpu.SMEM),
                  pl.BlockSpec(memory_space=pl.ANY),
                  pl.BlockSpec((tm, LANES), lambda i: (i + tile_off, 0)),
                  pl.BlockSpec((tm, D_MODEL), lambda i: (i, 0)),
                  pl.BlockSpec((None, 6, D_MODEL), mod_map),
                  pl.BlockSpec((1, D_MODEL), lambda i: (0, 0))],
        out_specs=pl.BlockSpec((tm, D_MODEL), lambda i: (i, 0)),
        scratch_shapes=[pltpu.VMEM((TOP_K, tm, PACKED_COLS), u32), pltpu.SemaphoreType.DMA],
        compiler_params=_cparams(("arbitrary",)),
        name="combine",
    )(dest_tiles, y, gcol, xnew, mod_l, final_g)


def _sc_gather_rows(planes, dest_tiles):
    nt, _, tm = dest_tiles.shape
    t = nt * tm
    parts, p, cols = planes.shape
    dest = dest_tiles.transpose(1, 0, 2).reshape(TOP_K, 1, t)
    idx = (dest + (jnp.arange(parts, dtype=i32) * p)[None, :, None]).reshape(1, -1)
    n = idx.shape[1]
    halves = planes.reshape(parts * p, cols)
    mesh = plsc.VectorSubcoreMesh(core_axis_name="core", subcore_axis_name="subcore")

    @pl.kernel(out_type=jax.ShapeDtypeStruct((n, cols), planes.dtype), mesh=mesh, scratch_types=[])
    def gather(x_hbm, i_hbm, o_hbm):
        def body(i_vmem, o_vmem):
            pltpu.sync_copy(x_hbm.at[i_vmem.at[0]], o_vmem)

        pltpu.emit_pipeline(
            body,
            grid=(n // SC_WINDOW,),
            in_specs=[pl.BlockSpec((1, SC_WINDOW), lambda i: (0, i))],
            out_specs=[pl.BlockSpec((SC_WINDOW, cols), lambda i: (i, 0))],
            core_axis_name=("core", "subcore"),
            dimension_semantics=(pltpu.PARALLEL,),
        )(i_hbm, o_hbm)

    return gather(halves, idx).reshape(TOP_K, parts, t, cols)


def _combine_rows_kernel(*refs, final):
    part_refs = refs[:TOP_K * SC_ROW_PARTS]
    gcol_ref, x_ref, mod_ref, fg_ref, o_ref = refs[TOP_K * SC_ROW_PARTS:]
    gc = gcol_ref[...]
    rows = [_unpack_rows(jnp.concatenate([r[...] for r in part_refs[k * SC_ROW_PARTS:(k + 1) * SC_ROW_PARTS]], axis=1))
            for k in range(TOP_K)]
    ffn = gc[:, 0:1] * rows[0] + gc[:, 1:2] * rows[1]
    x = x_ref[...] + mod_ref[...][5:6] * ffn
    if final:
        x = x * lax.rsqrt(jnp.mean(x * x, axis=-1, keepdims=True) + NORM_EPS) * fg_ref[...]
    o_ref[...] = x


def _combine_rows(yg, tile_off, gcol, xnew, mod_l, final_g, *, tiles_per_row, fixed_row, final):
    tm = MOE_TILE
    t = xnew.shape[0]
    if fixed_row is None:
        mod_map = lambda i: (i // tiles_per_row, 0, 0)
    else:
        mod_map = lambda i: (fixed_row, 0, 0)
    piece = lambda k, part: pl.BlockSpec((None, None, tm, SC_PART_COLS), lambda i: (k, part, i + tile_off, 0))
    pieces = [piece(k, part) for k in range(TOP_K) for part in range(SC_ROW_PARTS)]
    return pl.pallas_call(
        functools.partial(_combine_rows_kernel, final=final),
        out_shape=jax.ShapeDtypeStruct((t, D_MODEL), f32),
        grid=(t // tm,),
        in_specs=pieces + [
                  pl.BlockSpec((tm, LANES), lambda i: (i + tile_off, 0)),
                  pl.BlockSpec((tm, D_MODEL), lambda i: (i, 0)),
                  pl.BlockSpec((None, 6, D_MODEL), mod_map),
                  pl.BlockSpec((1, D_MODEL), lambda i: (0, 0))],
        out_specs=pl.BlockSpec((tm, D_MODEL), lambda i: (i, 0)),
        compiler_params=_cparams(("arbitrary",)),
        name="combine_rows",
    )(*([yg] * len(pieces)), gcol, xnew, mod_l, final_g)


def _moe(h2_parts, logits, router_bias_col, su, w_gate, w_up, w_down, layer, buf):
    t = logits.shape[0]
    ids, rank, gcol, cnt = _route(logits, router_bias_col, su)
    counts = cnt[:, 0].astype(i32)
    padded = (counts + EXPERT_BLOCK - 1) // EXPERT_BLOCK * EXPERT_BLOCK
    pend = jnp.cumsum(padded)
    pstart = pend - padded
    nb_needed = (t * TOP_K) // EXPERT_BLOCK + N_EXPERTS
    if buf is None:
        buf = jnp.zeros((nb_needed * EXPERT_BLOCK, PACKED_COLS), u32)
    nb = buf.shape[0] // EXPERT_BLOCK
    assert nb >= nb_needed
    block_pos = jnp.arange(nb, dtype=i32) * EXPERT_BLOCK
    block_e = jnp.minimum(jnp.sum((pend[None, :] <= block_pos[:, None]).astype(i32), axis=1), N_EXPERTS - 1)
    n_used = (pend[-1:] // EXPERT_BLOCK).astype(i32)
    dest_tiles = _dest(ids, rank, pstart.astype(f32)[:, None])
    tile_off = 0
    for part in h2_parts:
        buf = _dispatch(dest_tiles, tile_off, part, buf)
        tile_off += part.shape[0] // MOE_TILE
    y = _experts(block_e, n_used, buf, w_gate, w_up, w_down, layer)
    return dest_tiles, gcol, y, buf


def _pack_layer(l, w_in, conv_w, conv_b, dt_bias_f, dt_bias_b, a_log_f, a_log_b, d_skip, ssd_norm_g,
                gk_up_f, gk_bias_f, gk_up_b, gk_bias_b, gla_norm_g, w_out, norm1_g, norm2_g):
    w = w_in[l]
    z, xbc, dt, q, k, v, go, gk = jnp.split(w, [512, 1280, 1288, 1544, 1800, 2312, 2824], axis=1)
    small = jnp.concatenate([dt, gk, jnp.zeros((D_MODEL, SMALL_COLS - 24), f32)], axis=1)
    w_packed = jnp.concatenate([z, xbc, q, k, v, go, small], axis=1).astype(bf16)

    def lane_row(vec):
        return jnp.zeros((1, LANES), f32).at[0, :SSD_HEADS].set(vec)

    def sub_col(vec):
        return jnp.broadcast_to(vec[:, None], (SSD_HEADS, CHUNK)).astype(f32)

    def gup(m):
        return jnp.zeros((SMALL_COLS, GLA_QK), f32).at[SSD_HEADS:SSD_HEADS + GLA_GATE_RANK].set(m).astype(bf16)

    a_f = -jnp.exp(a_log_f[l])
    a_b = -jnp.exp(a_log_b[l])
    return dict(
        w_packed=w_packed,
        conv_w8=jnp.zeros((8, SSD_XBC), f32).at[:5].set(conv_w[l]),
        conv_b=conv_b[l][None, :],
        ssd_f=(lane_row(dt_bias_f[l]), lane_row(a_f), sub_col(dt_bias_f[l]), sub_col(a_f)),
        ssd_b=(lane_row(dt_bias_b[l]), lane_row(a_b), sub_col(dt_bias_b[l]), sub_col(a_b)),
        gla_f=(gup(gk_up_f[l]), gk_bias_f[l][None, :]),
        gla_b=(gup(gk_up_b[l]), gk_bias_b[l][None, :]),
        dsk=jnp.repeat(d_skip[l], SSD_HEAD_DIM)[None, :],
        sg=ssd_norm_g[l][None, :],
        gg=jnp.tile(gla_norm_g[l], GLA_HEADS)[None, :],
        wout=w_out[l].astype(bf16),
        n1g=norm1_g[l][None, :],
        n2g=norm2_g[l][None, :],
    )


def _constants():
    r = jnp.arange(LANES)[:, None]
    c = jnp.arange(SSD_WIDTH)[None, :]
    e_mat = ((c // SSD_HEAD_DIM) == r).astype(bf16)
    ssd_mask = ((r // SSD_STATE) == (c // (SSD_WIDTH // SSD_GROUPS))).astype(f32)
    rr = jnp.arange(GLA_WIDTH)[:, None]
    cc = jnp.arange(GLA_QK)[None, :]
    gla_mask = ((rr // GLA_VAL_DIM) == (cc // GLA_KEY_DIM)).astype(f32)
    k = jnp.arange(ROUTE_TILE)
    su = (k[:, None] < k[None, :]).astype(bf16)
    return e_mat, ssd_mask, gla_mask, su


def _mixers(streams, pk, consts, nb):
    e_mat, ssd_mask, gla_mask, _ = consts
    out = {}
    h0_f = jnp.zeros((nb, LANES, SSD_WIDTH), f32)
    h0_b = h0_f
    s0_f = jnp.zeros((nb, GLA_WIDTH, GLA_QK), f32)
    s0_b = s0_f
    for name in ('ctx', 'lat'):
        z, xs, bc, q, k, v, go, sm = streams[name]
        yf, yb, h0_f, h0_b = _ssd_scan(xs, bc, sm, h0_f, h0_b, pk['ssd_f'], pk['ssd_b'], e_mat, ssd_mask, nb=nb)
        of, ob, s0_f, s0_b = _gla_scan(q, k, v, sm, s0_f, s0_b, pk['gla_f'], pk['gla_b'], gla_mask, nb=nb)
        out[name] = (yf, yb, of, ob)
    return out


def kernel(x, c, ctx, c_ctx, w_mod, b_mod, norm1_g, norm2_g, w_in, conv_w, conv_b, dt_bias_f, dt_bias_b, a_log_f, a_log_b, d_skip, ssd_norm_g, gk_up_f, gk_bias_f, gk_up_b, gk_bias_b, gla_norm_g, w_out, w_router, router_bias, w_gate, w_up, w_down, final_norm_g):
    nb, seq, d = x.shape
    ctx_len = ctx.shape[1]
    depth = w_mod.shape[0]
    consts = _constants()
    su = consts[3]

    cvec = jnp.zeros((16, d), f32).at[:nb].set(c).at[nb].set(c_ctx)
    mod = _modulation(cvec, w_mod, b_mod).reshape(depth, 16, 6, d)
    ctx_row = nb

    perm = jnp.array([g * EXPERTS_PER_GROUP + m for m in range(EXPERTS_PER_GROUP)
                      for g in range(N_EXPERT_GROUPS)], dtype=i32)
    wr32 = jnp.zeros((d, LANES), f32).at[:, :N_EXPERTS].set(w_router[:, perm])
    wr_hi = wr32.astype(bf16)
    wr = jnp.stack([wr_hi, (wr32 - wr_hi.astype(f32)).astype(bf16)])
    bias_col = router_bias[perm][:, None]

    x2 = x.reshape(nb * seq, d)
    c2 = ctx.reshape(nb * ctx_len, d)
    lat_tiles = seq // TM_LAT
    moe_buf = None

    for l in range(depth):
        last = l == depth - 1
        pk = _pack_layer(l, w_in, conv_w, conv_b, dt_bias_f, dt_bias_b, a_log_f, a_log_b, d_skip, ssd_norm_g,
                         gk_up_f, gk_bias_f, gk_up_b, gk_bias_b, gla_norm_g, w_out, norm1_g, norm2_g)
        mod_l = mod[l]
        streams = {
            'ctx': _inproj(c2, mod_l, pk['n1g'], pk['w_packed'], pk['conv_w8'], pk['conv_b'],
                           tm=ctx_len, rowlen=ctx_len, tiles_per_row=1, fixed_row=ctx_row),
            'lat': _inproj(x2, mod_l, pk['n1g'], pk['w_packed'], pk['conv_w8'], pk['conv_b'],
                           tm=TM_LAT, rowlen=GRID_W, tiles_per_row=lat_tiles, fixed_row=None),
        }
        mix = _mixers(streams, pk, consts, nb)

        def merge(name, xres, tm, tiles_per_row, fixed_row):
            z, xs, bc, q, k, v, go, sm = streams[name]
            yf, yb, of, ob = mix[name]
            return _outproj(yf, yb, xs, z, of, ob, go, xres, mod_l, pk['dsk'], pk['sg'], pk['gg'],
                            pk['wout'], pk['n2g'], wr, tm=tm, tiles_per_row=tiles_per_row, fixed_row=fixed_row)

        xn_lat, h2_lat, lg_lat = merge('lat', x2, TM_LAT, lat_tiles, None)
        if last:
            parts, logits = [h2_lat], lg_lat
        else:
            xn_ctx, h2_ctx, lg_ctx = merge('ctx', c2, ctx_len, 1, ctx_row)
            parts, logits = [h2_ctx, h2_lat], jnp.concatenate([lg_ctx, lg_lat], axis=0)
        dest_tiles, gcol, y, moe_buf = _moe(parts, logits, bias_col, su, w_gate, w_up, w_down, l, moe_buf)
        yg = _sc_gather_rows(y, dest_tiles)
        lat_off = 0
        if not last:
            c2 = _combine_rows(yg, 0, gcol, xn_ctx, mod_l, final_norm_g[None, :],
                               tiles_per_row=1, fixed_row=ctx_row, final=False)
            lat_off = xn_ctx.shape[0] // MOE_TILE
        x2 = _combine_rows(yg, lat_off, gcol, xn_lat, mod_l, final_norm_g[None, :],
                           tiles_per_row=seq // MOE_TILE, fixed_row=None, final=last)
    return x2.reshape(nb, seq, d)
```

```python
import functools

import jax
import jax.numpy as jnp
from jax import lax
from jax.experimental import pallas as pl
from jax.experimental.pallas import tpu as pltpu
from jax.experimental.pallas import tpu_sc as plsc

f32 = jnp.float32
bf16 = jnp.bfloat16
i32 = jnp.int32
u32 = jnp.uint32

D_MODEL = 1024
SSD_HEADS = 8
SSD_HEAD_DIM = 64
SSD_WIDTH = 512
SSD_GROUPS = 2
SSD_STATE = 64
SSD_XBC = 768
GLA_HEADS = 4
GLA_KEY_DIM = 64
GLA_VAL_DIM = 128
GLA_QK = 256
GLA_WIDTH = 512
GLA_GATE_RANK = 16
GLA_GATE_NORM = 16.0
GRID_W = 64
N_EXPERTS = 32
N_EXPERT_GROUPS = 8
EXPERTS_PER_GROUP = 4
TOP_K = 2
EXPERT_FF = 512
NORM_EPS = 1e-6

LANES = 128
CHUNK = 128
CHUNKS_PER_STEP = 4
INTERLEAVE_WAVE = 8
ROW_SPLIT = 2
SSD_SPLIT = 3
GLA_SPLIT = 2
MAIN_COLS = 2816
SMALL_COLS = LANES
TM_LAT = 512
ROUTE_TILE = 512
MOE_TILE = 512
SC_WINDOW = 128
SC_ROW_PARTS = 2
SC_PART_COLS = D_MODEL // 2 // SC_ROW_PARTS
PACKED_COLS = D_MODEL // 2
EXPERT_BLOCK = 512
VMEM_LIMIT = 48 * 1024 * 1024

_HI = lax.Precision.HIGHEST
_NT = (((1,), (1,)), ((), ()))


def _dot(a, b, precision=None):
    return jnp.dot(a, b, preferred_element_type=f32, precision=precision)


def _silu(x):
    return x * jax.nn.sigmoid(x)


def _plane_spec(rows, block_index):
    return pl.BlockSpec((SC_ROW_PARTS, rows, SC_PART_COLS), lambda *idx: (0, block_index(*idx), 0))


def _store_planes(ref, rows, packed):
    for part in range(SC_ROW_PARTS):
        ref[part, rows, :] = packed[:, part * SC_PART_COLS:(part + 1) * SC_PART_COLS]


def _load_planes(ref, rows):
    return jnp.concatenate([ref[part, rows, :] for part in range(SC_ROW_PARTS)], axis=1)


def _pack_rows(value):
    hi = pltpu.bitcast(value[:, :PACKED_COLS].astype(bf16).astype(f32), u32)
    lo = pltpu.bitcast(value[:, PACKED_COLS:].astype(bf16).astype(f32), u32)
    return hi | (lo >> 16)


def _unpack_rows(packed):
    hi = pltpu.bitcast(packed & jnp.uint32(0xFFFF0000), f32)
    lo = pltpu.bitcast(packed << 16, f32)
    return jnp.concatenate([hi, lo], axis=1)


def _cparams(sem):
    return pltpu.CompilerParams(dimension_semantics=sem, vmem_limit_bytes=VMEM_LIMIT)


def _mod_kernel(c_ref, w_ref, b_ref, o_ref):
    sc = _silu(c_ref[...]).astype(bf16)
    o_ref[...] = _dot(sc, w_ref[...].astype(bf16)) + b_ref[...]


def _modulation(cvec, w_mod, b_mod):
    depth, d, n = w_mod.shape
    tn = 1536
    return pl.pallas_call(
        _mod_kernel,
        out_shape=jax.ShapeDtypeStruct((depth, 16, n), f32),
        grid=(depth, n // tn),
        in_specs=[pl.BlockSpec((16, d), lambda l, j: (0, 0)),
                  pl.BlockSpec((None, d, tn), lambda l, j: (l, 0, j)),
                  pl.BlockSpec((None, 1, tn), lambda l, j: (l, 0, j))],
        out_specs=pl.BlockSpec((None, 16, tn), lambda l, j: (l, 0, j)),
        compiler_params=_cparams(("arbitrary", "arbitrary")),
        name="modulation",
    )(cvec, w_mod, b_mod.reshape(depth, 1, n))


def _inproj_kernel(x_ref, mod_ref, g_ref, w_ref, cw_ref, cb_ref,
                   z_ref, xs_ref, bc_ref, q_ref, k_ref, v_ref, go_ref, sm_ref, *, rowlen):
    tm = x_ref.shape[0]
    split = ROW_SPLIT if (tm // ROW_SPLIT) % rowlen == 0 else 1
    sub = tm // split
    m = mod_ref[...]
    geff = g_ref[...] * (1.0 + m[1:2])
    cw = cw_ref[...]

    def rows_gen(rows):
        x = x_ref[rows, :]
        inv = lax.rsqrt(jnp.mean(x * x, axis=-1, keepdims=True) + NORM_EPS)
        h = ((x * inv) * geff + m[0:1]).astype(bf16)
        yield

        def proj(lo, hi):
            return _dot(h, w_ref[:, lo:hi])

        z_ref[rows, :] = proj(0, 512).astype(bf16)
        yield

        xbc = proj(512, 1280)
        yield
        pos = lax.broadcasted_iota(i32, xbc.shape, 0) & (rowlen - 1)
        acc = xbc * cw[2:3]
        for d in (-2, -1, 1, 2):
            shifted = pltpu.roll(xbc, (-d) % sub, 0)
            valid = (pos >= -d) if d < 0 else (pos <= rowlen - 1 - d)
            acc = acc + jnp.where(valid, shifted, 0.0) * cw[2 + d:3 + d]
        yield
        act = _silu(acc + cb_ref[...])
        xs_ref[rows, :] = act[:, :SSD_WIDTH].astype(bf16)
        bc_ref[rows, :] = act[:, SSD_WIDTH:].astype(bf16)
        yield

        qk = proj(1280, 1792)
        q_ref[rows, :] = qk[:, :GLA_QK].astype(bf16)
        k_ref[rows, :] = qk[:, GLA_QK:].astype(bf16)
        yield
        v_ref[rows, :] = proj(1792, 2304).astype(bf16)
        yield
        go_ref[rows, :] = proj(2304, 2816).astype(bf16)
        yield
        sm_ref[rows, :] = proj(2816, 2944)

    _run_interleaved(rows_gen(pl.ds(r * sub, sub)) for r in range(split))


def _inproj(x2d, mod_l, norm_g, w_packed, conv_w8, conv_b, *, tm, rowlen, tiles_per_row, fixed_row):
    t = x2d.shape[0]
    nt = t // tm
    if fixed_row is None:
        mod_map = lambda i: (i // tiles_per_row, 0, 0)
    else:
        mod_map = lambda i: (fixed_row, 0, 0)
    tok = lambda w: pl.BlockSpec((tm, w), lambda i: (i, 0))
    const = lambda a: pl.BlockSpec(a.shape, lambda i: (0,) * a.ndim)
    widths = (512, 512, 256, 256, 256, 512, 512)
    return pl.pallas_call(
        functools.partial(_inproj_kernel, rowlen=rowlen),
        out_shape=tuple(jax.ShapeDtypeStruct((t, w), bf16) for w in widths)
        + (jax.ShapeDtypeStruct((t, SMALL_COLS), f32),),
        grid=(nt,),
        in_specs=[tok(D_MODEL), pl.BlockSpec((None, 6, D_MODEL), mod_map), const(norm_g),
                  const(w_packed), const(conv_w8), const(conv_b)],
        out_specs=tuple(tok(w) for w in widths) + (tok(SMALL_COLS),),
        compiler_params=_cparams(("arbitrary",)),
        name="inproj",
    )(x2d, mod_l, norm_g, w_packed, conv_w8, conv_b)


def _dot_split(a, b, parts, *, split_lhs):
    rest = a if split_lhs else b
    acc = None
    for _ in range(parts):
        piece = rest.astype(bf16)
        rest = rest - piece.astype(f32)
        term = _dot(piece, b) if split_lhs else _dot(a, piece)
        acc = term if acc is None else acc + term
    return acc


def _tri_masks(q):
    r = lax.broadcasted_iota(i32, (q, q), 0)
    c = lax.broadcasted_iota(i32, (q, q), 1)
    return r >= c, r <= c


def _ssd_chunk(load, store, carry, par, e, bm, rev):
    xb, bcv, sm = load()
    dtb, a_lane, dtbt, a_sub = par
    q = xb.shape[0]
    dt_col = jax.nn.softplus(sm + dtb)
    a_col = dt_col * a_lane
    dt_row = jax.nn.softplus(sm.T[0:SSD_HEADS, :] + dtbt)
    a_row = dt_row * a_sub
    yield
    lower, upper = _tri_masks(q)
    lo_b = jnp.where(lower, 1.0, 0.0).astype(bf16)
    up_b = jnp.where(upper, 1.0, 0.0).astype(bf16)
    if not rev:
        cs_col = _dot_split(lo_b, a_col, SSD_SPLIT, split_lhs=False)
        cs_row = _dot_split(a_row, up_b, SSD_SPLIT, split_lhs=True)
        mask = lower
        a_tot = cs_col[q - 1:q, :]
    else:
        cs_col = _dot_split(up_b, a_col, SSD_SPLIT, split_lhs=False)
        cs_row = _dot_split(a_row, lo_b, SSD_SPLIT, split_lhs=True)
        mask = upper
        a_tot = cs_col[0:1, :]
    yield

    w_exp = _dot((jnp.exp(a_tot - cs_col) * dt_col).astype(bf16), e)
    ecs_exp = _dot(jnp.exp(cs_col).astype(bf16), e)
    dec_exp = _dot_split(jnp.broadcast_to(jnp.exp(a_tot), (8, LANES)), e, SSD_SPLIT, split_lhs=True)[0:1]
    yield

    xw = (xb.astype(f32) * w_exp).astype(bf16)
    b_all = bcv[:, 0:LANES]
    c_all = bcv[:, LANES:2 * LANES]
    b_t = b_all.astype(f32).T.astype(bf16)
    yield
    s_upd = _dot(b_t, xw)
    yield

    lane = lax.broadcasted_iota(i32, (q, LANES), 1)
    zero_b = jnp.zeros((q, LANES), bf16)
    ys = []
    for g in range(SSD_GROUPS):
        in_g = (lane >= SSD_STATE * g) & (lane < SSD_STATE * (g + 1))
        cb = _dot(jnp.where(in_g, c_all, zero_b), b_t)
        yield
        for pp in range(2):
            h0 = 4 * g + 2 * pp
            ms = []
            for h in (h0, h0 + 1):
                seg = cs_col[:, h:h + 1] - cs_row[h:h + 1, :]
                dec = jnp.exp(jnp.where(mask, seg, -1e30))
                ms.append((cb * dec * dt_row[h:h + 1, :]).astype(bf16))
            xp = xb[:, h0 * SSD_HEAD_DIM:h0 * SSD_HEAD_DIM + LANES]
            rhs = jnp.concatenate([jnp.where(lane < SSD_HEAD_DIM, xp, zero_b),
                                   jnp.where(lane >= SSD_HEAD_DIM, xp, zero_b)], axis=0)
            yield
            ys.append(_dot(jnp.concatenate(ms, axis=1), rhs))
            yield
    y_intra = jnp.concatenate(ys, axis=1)
    yield

    s_old = carry['s']
    carry['s'] = (s_old * dec_exp + s_upd) * bm
    store((y_intra + _dot(c_all, s_old.astype(bf16)) * ecs_exp).astype(bf16))


def _run_interleaved(gens):
    gens = list(gens)
    done = object()
    for w in range(0, len(gens), INTERLEAVE_WAVE):
        live = gens[w:w + INTERLEAVE_WAVE]
        while live:
            live = [g for g in live if next(g, done) is not done]


def _chunk_rows(cps):
    return [(pl.ds(ci * CHUNK, CHUNK), pl.ds((cps - 1 - ci) * CHUNK, CHUNK)) for ci in range(cps)]


def _ssd_kernel(xf_ref, bcf_ref, smf_ref, xr_ref, bcr_ref, smr_ref, h0f_ref, h0r_ref,
                dtbf_ref, af_ref, dtbtf_ref, atf_ref, dtbr_ref, ar_ref, dtbtr_ref, atr_ref, e_ref, bm_ref,
                yf_ref, yr_ref, hff_ref, hfr_ref, st_f, st_r, *, cps, nblk):
    j = pl.program_id(1)

    @pl.when(j == 0)
    def _():
        st_f[...] = h0f_ref[...]
        st_r[...] = h0r_ref[...]

    e = e_ref[...]
    bm = bm_ref[...]
    par_f = (dtbf_ref[...], af_ref[...], dtbtf_ref[...], atf_ref[...])
    par_r = (dtbr_ref[...], ar_ref[...], dtbtr_ref[...], atr_ref[...])
    carry_f = {'s': st_f[...]}
    carry_r = {'s': st_r[...]}

    def chunk(refs, out_ref, rows, carry, par, rev):
        def store(y):
            out_ref[rows, :] = y
        return _ssd_chunk(lambda: tuple(r[rows, :] for r in refs), store, carry, par, e, bm, rev)

    gens = []
    for rows_f, rows_r in _chunk_rows(cps):
        gens.append(chunk((xf_ref, bcf_ref, smf_ref), yf_ref, rows_f, carry_f, par_f, False))
        gens.append(chunk((xr_ref, bcr_ref, smr_ref), yr_ref, rows_r, carry_r, par_r, True))
    _run_interleaved(gens)
    st_f[...] = carry_f['s']
    st_r[...] = carry_r['s']

    @pl.when(j == nblk - 1)
    def _():
        hff_ref[...] = carry_f['s']
        hfr_ref[...] = carry_r['s']


def _scan_specs(t, nb):
    per_row = t // nb
    cps = min(CHUNKS_PER_STEP, per_row // CHUNK)
    nblk = per_row // (cps * CHUNK)
    fmap = lambda b, j: (b * nblk + j, 0)
    rmap = lambda b, j: (b * nblk + nblk - 1 - j, 0)
    return cps, nblk, fmap, rmap


def _ssd_scan(xs, bc, sm, h0_f, h0_r, par_f, par_r, e_mat, bmask, *, nb):
    t = xs.shape[0]
    cps, nblk, fmap, rmap = _scan_specs(t, nb)
    blk = cps * CHUNK
    const = lambda a: pl.BlockSpec(a.shape, lambda b, j: (0,) * a.ndim)
    st = pl.BlockSpec((None, LANES, SSD_WIDTH), lambda b, j: (b, 0, 0))
    widths = (SSD_WIDTH, 2 * LANES, SMALL_COLS)
    tok_in = [pl.BlockSpec((blk, w), m) for m in (fmap, rmap) for w in widths]
    consts = (*par_f, *par_r, e_mat, bmask)
    return pl.pallas_call(
        functools.partial(_ssd_kernel, cps=cps, nblk=nblk),
        out_shape=(jax.ShapeDtypeStruct((t, SSD_WIDTH), bf16), jax.ShapeDtypeStruct((t, SSD_WIDTH), bf16),
                   jax.ShapeDtypeStruct((nb, LANES, SSD_WIDTH), f32),
                   jax.ShapeDtypeStruct((nb, LANES, SSD_WIDTH), f32)),
        grid=(nb, nblk),
        in_specs=tok_in + [st, st] + [const(a) for a in consts],
        out_specs=(pl.BlockSpec((blk, SSD_WIDTH), fmap), pl.BlockSpec((blk, SSD_WIDTH), rmap), st, st),
        scratch_shapes=[pltpu.VMEM((LANES, SSD_WIDTH), f32), pltpu.VMEM((LANES, SSD_WIDTH), f32)],
        compiler_params=_cparams(("arbitrary", "arbitrary")),
        name="ssd",
    )(xs, bc, sm, xs, bc, sm, h0_f, h0_r, *consts)


def _gla_chunk(load, store, carry, par, bm, rev):
    qb, kb, vb, sm = load()
    gup, gbias = par
    q = qb.shape[0]
    gp = _dot(sm.astype(bf16), gup) + gbias
    yield
    g = jax.nn.log_sigmoid(gp) * (1.0 / GLA_GATE_NORM)
    yield
    lower, upper = _tri_masks(q)
    if not rev:
        b = _dot_split(jnp.where(lower, 1.0, 0.0).astype(bf16), g, GLA_SPLIT, split_lhs=False)
        mask, mid = lower, q // 2 - 1
        b_tot = b[q - 1:q, :]
    else:
        b = _dot_split(jnp.where(upper, 1.0, 0.0).astype(bf16), g, GLA_SPLIT, split_lhs=False)
        mask, mid = upper, q // 2
        b_tot = b[0:1, :]
    b_mid = b[mid:mid + 1, :]
    yield

    qf = qb.astype(f32) * (GLA_KEY_DIM ** -0.5)
    kf = kb.astype(f32)
    qd = (qf * jnp.exp(b - b_mid)).astype(bf16)
    yield
    ki = (kf * jnp.exp(b_mid - b)).astype(bf16)
    yield
    q_st = (qf * jnp.exp(b)).astype(bf16)
    yield
    k_end = (kf * jnp.exp(b_tot - b)).astype(bf16)
    dec = jnp.exp(b_tot)
    yield
    v_t = vb.astype(f32).T.astype(bf16)
    yield
    s_upd = _dot(v_t, k_end)
    yield

    lane = lax.broadcasted_iota(i32, (q, LANES), 1)
    zero_b = jnp.zeros((q, LANES), bf16)
    outs = []
    for h in range(GLA_HEADS):
        p, hh = divmod(h, 2)
        qp = qd[:, p * LANES:(p + 1) * LANES]
        kp = ki[:, p * LANES:(p + 1) * LANES]
        in_h = (lane >= GLA_KEY_DIM * hh) & (lane < GLA_KEY_DIM * (hh + 1))
        s = lax.dot_general(jnp.where(in_h, qp, zero_b), kp, _NT, preferred_element_type=f32)
        yield
        attn = jnp.where(mask, s, 0.0).astype(bf16)
        outs.append(_dot(attn, vb[:, h * GLA_VAL_DIM:(h + 1) * GLA_VAL_DIM]))
        yield
    o_intra = jnp.concatenate(outs, axis=1)
    yield

    s_old = carry['s']
    carry['s'] = (s_old * dec + s_upd) * bm
    o_inter = lax.dot_general(q_st, s_old.astype(bf16), _NT, preferred_element_type=f32)
    store((o_intra + o_inter).astype(bf16))


def _gla_kernel(qf_ref, kf_ref, vf_ref, smf_ref, qr_ref, kr_ref, vr_ref, smr_ref, s0f_ref, s0r_ref,
                gupf_ref, gbf_ref, gupr_ref, gbr_ref, bm_ref,
                of_ref, or_ref, sff_ref, sfr_ref, st_f, st_r, *, cps, nblk):
    j = pl.program_id(1)

    @pl.when(j == 0)
    def _():
        st_f[...] = s0f_ref[...]
        st_r[...] = s0r_ref[...]

    bm = bm_ref[...]
    par_f = (gupf_ref[...], gbf_ref[...])
    par_r = (gupr_ref[...], gbr_ref[...])
    carry_f = {'s': st_f[...]}
    carry_r = {'s': st_r[...]}

    def chunk(refs, out_ref, rows, carry, par, rev):
        def store(o):
            out_ref[rows, :] = o
        return _gla_chunk(lambda: tuple(r[rows, :] for r in refs), store, carry, par, bm, rev)

    gens = []
    for rows_f, rows_r in _chunk_rows(cps):
        gens.append(chunk((qf_ref, kf_ref, vf_ref, smf_ref), of_ref, rows_f, carry_f, par_f, False))
        gens.append(chunk((qr_ref, kr_ref, vr_ref, smr_ref), or_ref, rows_r, carry_r, par_r, True))
    _run_interleaved(gens)
    st_f[...] = carry_f['s']
    st_r[...] = carry_r['s']

    @pl.when(j == nblk - 1)
    def _():
        sff_ref[...] = carry_f['s']
        sfr_ref[...] = carry_r['s']


def _gla_scan(qa, ka, va, sm, s0_f, s0_r, par_f, par_r, bmask, *, nb):
    t = qa.shape[0]
    cps, nblk, fmap, rmap = _scan_specs(t, nb)
    blk = cps * CHUNK
    const = lambda a: pl.BlockSpec(a.shape, lambda b, j: (0,) * a.ndim)
    st = pl.BlockSpec((None, GLA_WIDTH, GLA_QK), lambda b, j: (b, 0, 0))
    widths = (GLA_QK, GLA_QK, GLA_WIDTH, SMALL_COLS)
    tok_in = [pl.BlockSpec((blk, w), m) for m in (fmap, rmap) for w in widths]
    consts = (*par_f, *par_r, bmask)
    return pl.pallas_call(
        functools.partial(_gla_kernel, cps=cps, nblk=nblk),
        out_shape=(jax.ShapeDtypeStruct((t, GLA_WIDTH), bf16), jax.ShapeDtypeStruct((t, GLA_WIDTH), bf16),
                   jax.ShapeDtypeStruct((nb, GLA_WIDTH, GLA_QK), f32),
                   jax.ShapeDtypeStruct((nb, GLA_WIDTH, GLA_QK), f32)),
        grid=(nb, nblk),
        in_specs=tok_in + [st, st] + [const(a) for a in consts],
        out_specs=(pl.BlockSpec((blk, GLA_WIDTH), fmap), pl.BlockSpec((blk, GLA_WIDTH), rmap), st, st),
        scratch_shapes=[pltpu.VMEM((GLA_WIDTH, GLA_QK), f32), pltpu.VMEM((GLA_WIDTH, GLA_QK), f32)],
        compiler_params=_cparams(("arbitrary", "arbitrary")),
        name="gla",
    )(qa, ka, va, sm, qa, ka, va, sm, s0_f, s0_r, *consts)


def _outproj_kernel(yf_ref, yb_ref, xs_ref, z_ref, of_ref, ob_ref, go_ref, xres_ref, mod_ref,
                    dsk_ref, sg_ref, gg_ref, wout_ref, n2g_ref, wr_ref,
                    xnew_ref, h2_ref, lg_ref):
    tm = xres_ref.shape[0]
    sub = tm // ROW_SPLIT
    m = mod_ref[...]
    h2_gain = n2g_ref[...] * (1.0 + m[4:5])

    def rows_gen(rows):
        up = lambda r: r[rows, :].astype(f32)
        y = up(yf_ref) + up(yb_ref) + dsk_ref[...] * up(xs_ref)
        yield
        y = y * _silu(up(z_ref))
        yield
        y = y * lax.rsqrt(jnp.mean(y * y, axis=-1, keepdims=True) + NORM_EPS) * sg_ref[...]
        yield
        o = up(of_ref) + up(ob_ref)
        parts = []
        for h in range(GLA_HEADS):
            oh = o[:, h * GLA_VAL_DIM:(h + 1) * GLA_VAL_DIM]
            parts.append(oh * lax.rsqrt(jnp.mean(oh * oh, axis=-1, keepdims=True) + NORM_EPS))
        yield
        o = jnp.concatenate(parts, axis=1) * gg_ref[...] * _silu(up(go_ref))
        yield
        cat = jnp.concatenate([y, o], axis=1).astype(bf16)
        mix = _dot(cat, wout_ref[...])
        yield
        xn = xres_ref[rows, :] + m[2:3] * mix
        xnew_ref[rows, :] = xn
        yield
        inv = lax.rsqrt(jnp.mean(xn * xn, axis=-1, keepdims=True) + NORM_EPS)
        h2 = (xn * inv) * h2_gain + m[3:4]
        _store_planes(h2_ref, rows, _pack_rows(h2))
        yield
        h2_hi = h2.astype(bf16)
        h2_lo = (h2 - h2_hi.astype(f32)).astype(bf16)
        wr_hi = wr_ref[0]
        lg_ref[rows, :] = _dot(h2_hi, wr_hi) + _dot(h2_hi, wr_ref[1]) + _dot(h2_lo, wr_hi)

    _run_interleaved(rows_gen(pl.ds(r * sub, sub)) for r in range(ROW_SPLIT))


def _outproj(yf, yb, xs, z, of, ob, go, xres, mod_l, dsk, sg, gg, wout, n2g, wr,
             *, tm, tiles_per_row, fixed_row):
    t = xres.shape[0]
    if fixed_row is None:
        mod_map = lambda i: (i // tiles_per_row, 0, 0)
    else:
        mod_map = lambda i: (fixed_row, 0, 0)
    tok = lambda w: pl.BlockSpec((tm, w), lambda i: (i, 0))
    const = lambda a: pl.BlockSpec(a.shape, lambda i: (0,) * a.ndim)
    return pl.pallas_call(
        _outproj_kernel,
        out_shape=(jax.ShapeDtypeStruct((t, D_MODEL), f32), jax.ShapeDtypeStruct((SC_ROW_PARTS, t, SC_PART_COLS), u32),
                   jax.ShapeDtypeStruct((t, LANES), f32)),
        grid=(t // tm,),
        in_specs=[tok(512)] * 7 + [tok(D_MODEL), pl.BlockSpec((None, 6, D_MODEL), mod_map),
                                   const(dsk), const(sg), const(gg), const(wout), const(n2g), const(wr)],
        out_specs=(tok(D_MODEL), _plane_spec(tm, lambda i: i), tok(LANES)),
        compiler_params=_cparams(("arbitrary",)),
        name="outproj",
    )(yf, yb, xs, z, of, ob, go, xres, mod_l, dsk, sg, gg, wout, n2g, wr)


def _route_kernel(lg_ref, bias_ref, su_ref, ids_ref, rank_ref, gcol_ref, cnt_ref, carry):
    i = pl.program_id(0)
    tm = lg_ref.shape[0]

    @pl.when(i == 0)
    def _():
        carry[...] = jnp.zeros_like(carry)

    s = jax.nn.sigmoid(lg_ref[...].T[0:N_EXPERTS, :])
    sel = s + bias_ref[...]
    a = [sel[8 * m:8 * (m + 1)] for m in range(EXPERTS_PER_GROUP)]
    sv = [s[8 * m:8 * (m + 1)] for m in range(EXPERTS_PER_GROUP)]
    hi01, lo01 = jnp.maximum(a[0], a[1]), jnp.minimum(a[0], a[1])
    hi23, lo23 = jnp.maximum(a[2], a[3]), jnp.minimum(a[2], a[3])
    gscore = jnp.maximum(hi01, hi23) + jnp.maximum(jnp.minimum(hi01, hi23), jnp.maximum(lo01, lo23))
    giota = lax.broadcasted_iota(i32, gscore.shape, 0)
    gmax = jnp.max(gscore, axis=0, keepdims=True)
    gidx = jnp.min(jnp.where(gscore == gmax, giota, N_EXPERT_GROUPS), axis=0, keepdims=True)
    pick = giota == gidx
    v = [jnp.sum(jnp.where(pick, a[m], 0.0), axis=0, keepdims=True) for m in range(4)]
    w = [jnp.sum(jnp.where(pick, sv[m], 0.0), axis=0, keepdims=True) for m in range(4)]

    def first_max(vals, excluded):
        best = vals[0]
        for m in range(1, 4):
            best = jnp.maximum(best, vals[m])
        idx = jnp.full(best.shape, 3, i32)
        for m in (2, 1, 0):
            hit = vals[m] == best
            if excluded is not None:
                hit = hit & (excluded != m)
            idx = jnp.where(hit, m, idx)
        return idx

    i1 = first_max(v, None)
    v_rest = [jnp.where(i1 == m, -jnp.inf, v[m]) for m in range(4)]
    i2 = first_max(v_rest, i1)

    def take(vals, idx):
        out = vals[3]
        for m in (2, 1, 0):
            out = jnp.where(idx == m, vals[m], out)
        return out

    w1, w2 = take(w, i1), take(w, i2)
    denom = w1 + w2
    id1 = gidx * EXPERTS_PER_GROUP + i1
    id2 = gidx * EXPERTS_PER_GROUP + i2
    row2 = lax.broadcasted_iota(i32, (TOP_K, tm), 0)
    ids_ref[...] = jnp.where(row2 == 0, id1, id2)

    eiota = lax.broadcasted_iota(i32, (N_EXPERTS, tm), 0)
    hit1, hit2 = eiota == id1, eiota == id2
    onehot = jnp.where(hit1, 1.0, 0.0) + jnp.where(hit2, 1.0, 0.0)
    before = _dot(onehot.astype(bf16), su_ref[...]) + carry[...]
    r1 = jnp.sum(jnp.where(hit1, before, 0.0), axis=0, keepdims=True)
    r2 = jnp.sum(jnp.where(hit2, before, 0.0), axis=0, keepdims=True)
    rank_ref[...] = jnp.where(row2 == 0, r1, r2).astype(i32)
    new_carry = carry[...] + jnp.sum(onehot, axis=1, keepdims=True)
    carry[...] = new_carry
    cnt_ref[...] = jnp.broadcast_to(new_carry, cnt_ref.shape)

    rows = lax.broadcasted_iota(i32, (LANES, tm), 0)
    gates = jnp.where(rows == 0, w1 / denom, jnp.where(rows == 1, w2 / denom, 0.0))
    gcol_ref[...] = gates.T


def _route(logits, bias_col, su):
    t = logits.shape[0]
    tm = ROUTE_TILE
    return pl.pallas_call(
        _route_kernel,
        out_shape=(jax.ShapeDtypeStruct((2, t), i32), jax.ShapeDtypeStruct((2, t), i32),
                   jax.ShapeDtypeStruct((t, LANES), f32), jax.ShapeDtypeStruct((N_EXPERTS, LANES), f32)),
        grid=(t // tm,),
        in_specs=[pl.BlockSpec((tm, LANES), lambda i: (i, 0)),
                  pl.BlockSpec((N_EXPERTS, 1), lambda i: (0, 0)),
                  pl.BlockSpec((tm, tm), lambda i: (0, 0))],
        out_specs=(pl.BlockSpec((2, tm), lambda i: (0, i)), pl.BlockSpec((2, tm), lambda i: (0, i)),
                   pl.BlockSpec((tm, LANES), lambda i: (i, 0)),
                   pl.BlockSpec((N_EXPERTS, LANES), lambda i: (0, 0))),
        scratch_shapes=[pltpu.VMEM((N_EXPERTS, 1), f32)],
        compiler_params=_cparams(("arbitrary",)),
        name="route",
    )(logits, bias_col, su)


def _dest_kernel(ids_ref, rank_ref, pstart_ref, dest_ref):
    ids = ids_ref[...]
    tm = ids.shape[1]
    eiota = lax.broadcasted_iota(i32, (N_EXPERTS, tm), 0)
    ps = pstart_ref[...]
    rows = [jnp.sum(jnp.where(eiota == ids[k:k + 1, :], ps, 0.0), axis=0, keepdims=True) for k in range(TOP_K)]
    row2 = lax.broadcasted_iota(i32, (TOP_K, tm), 0)
    dest_ref[...] = jnp.where(row2 == 0, rows[0], rows[1]).astype(i32) + rank_ref[...]


def _dest(ids, rank, pstart_col):
    t = ids.shape[1]
    tm = MOE_TILE
    return pl.pallas_call(
        _dest_kernel,
        out_shape=jax.ShapeDtypeStruct((t // tm, TOP_K, tm), i32),
        grid=(t // tm,),
        in_specs=[pl.BlockSpec((TOP_K, tm), lambda i: (0, i)), pl.BlockSpec((TOP_K, tm), lambda i: (0, i)),
                  pl.BlockSpec((N_EXPERTS, 1), lambda i: (0, 0))],
        out_specs=pl.BlockSpec((None, TOP_K, tm), lambda i: (i, 0, 0)),
        compiler_params=_cparams(("arbitrary",)),
        name="dest",
    )(ids, rank, pstart_col)


def _expert_kernel(be_ref, nu_ref, nv_ref, x_ref, wg_ref, wu_ref, wd_ref, y_ref, wg_b, wu_b, wd_b):
    i = pl.program_id(0)
    fresh = jnp.logical_or(i == 0, be_ref[i] != be_ref[jnp.maximum(i - 1, 0)])

    @pl.when(jnp.logical_and(fresh, i < nu_ref[0]))
    def _():
        wg_b[...] = wg_ref[...].astype(bf16)
        wu_b[...] = wu_ref[...].astype(bf16)
        wd_b[...] = wd_ref[...].astype(bf16)

    @pl.when(i < nu_ref[0])
    def _():
        sub = x_ref.shape[1] // ROW_SPLIT
        n_valid = nv_ref[i]

        def rows_gen(r):
            rows = pl.ds(r * sub, sub)
            packed_x = _load_planes(x_ref, rows)
            row = lax.broadcasted_iota(i32, packed_x.shape, 0) + r * sub
            xb = _unpack_rows(jnp.where(row < n_valid, packed_x, jnp.uint32(0))).astype(bf16)
            yield
            g = _dot(xb, wg_b[...])
            yield
            u = _dot(xb, wu_b[...])
            yield
            hmid = (_silu(g) * u).astype(bf16)
            yield
            _store_planes(y_ref, rows, _pack_rows(_dot(hmid, wd_b[...])))

        _run_interleaved(rows_gen(r) for r in range(ROW_SPLIT))

    @pl.when(i >= nu_ref[0])
    def _():
        y_ref[...] = jnp.zeros_like(y_ref)


def _experts(block_e, n_used, block_valid, buf, w_gate, w_up, w_down, layer):
    p = buf.shape[1]
    nb = p // EXPERT_BLOCK
    d, ff = w_gate.shape[-2:]
    row_blocks = _plane_spec(EXPERT_BLOCK, lambda i, be, nu, nv: i)
    weights = lambda a, b: pl.BlockSpec((None, None, a, b), lambda i, be, nu, nv: (layer, be[i], 0, 0))
    return pl.pallas_call(
        _expert_kernel,
        out_shape=jax.ShapeDtypeStruct(buf.shape, u32),
        grid_spec=pltpu.PrefetchScalarGridSpec(
            num_scalar_prefetch=3,
            grid=(nb,),
            in_specs=[row_blocks, weights(d, ff), weights(d, ff), weights(ff, d)],
            out_specs=row_blocks,
            scratch_shapes=[pltpu.VMEM((d, ff), bf16), pltpu.VMEM((d, ff), bf16), pltpu.VMEM((ff, d), bf16)],
        ),
        compiler_params=_cparams(("arbitrary",)),
        name="experts",
    )(block_e, n_used, block_valid, buf, w_gate, w_up, w_down)


def _sc_gather_rows(planes, dest_tiles):
    nt, _, tm = dest_tiles.shape
    t = nt * tm
    parts, p, cols = planes.shape
    dest = dest_tiles.transpose(1, 0, 2).reshape(TOP_K, 1, t)
    idx = (dest + (jnp.arange(parts, dtype=i32) * p)[None, :, None]).reshape(1, -1)
    n = idx.shape[1]
    halves = planes.reshape(parts * p, cols)
    mesh = plsc.VectorSubcoreMesh(core_axis_name="core", subcore_axis_name="subcore")

    @pl.kernel(out_type=jax.ShapeDtypeStruct((n, cols), planes.dtype), mesh=mesh, scratch_types=[])
    def gather(x_hbm, i_hbm, o_hbm):
        def body(i_vmem, o_vmem):
            pltpu.sync_copy(x_hbm.at[i_vmem.at[0]], o_vmem)

        pltpu.emit_pipeline(
            body,
            grid=(n // SC_WINDOW,),
            in_specs=[pl.BlockSpec((1, SC_WINDOW), lambda i: (0, i))],
            out_specs=[pl.BlockSpec((SC_WINDOW, cols), lambda i: (i, 0))],
            core_axis_name=("core", "subcore"),
            dimension_semantics=(pltpu.PARALLEL,),
        )(i_hbm, o_hbm)

    return gather(halves, idx).reshape(TOP_K, parts, t, cols)


def _combine_rows_kernel(*refs, final):
    part_refs = refs[:TOP_K * SC_ROW_PARTS]
    gcol_ref, x_ref, mod_ref, fg_ref, o_ref = refs[TOP_K * SC_ROW_PARTS:]
    gc = gcol_ref[...]
    rows = [_unpack_rows(jnp.concatenate([r[...] for r in part_refs[k * SC_ROW_PARTS:(k + 1) * SC_ROW_PARTS]], axis=1))
            for k in range(TOP_K)]
    ffn = gc[:, 0:1] * rows[0] + gc[:, 1:2] * rows[1]
    x = x_ref[...] + mod_ref[...][5:6] * ffn
    if final:
        x = x * lax.rsqrt(jnp.mean(x * x, axis=-1, keepdims=True) + NORM_EPS) * fg_ref[...]
    o_ref[...] = x


def _combine_rows(yg, tile_off, gcol, xnew, mod_l, final_g, *, tiles_per_row, fixed_row, final):
    tm = MOE_TILE
    t = xnew.shape[0]
    if fixed_row is None:
        mod_map = lambda i: (i // tiles_per_row, 0, 0)
    else:
        mod_map = lambda i: (fixed_row, 0, 0)
    piece = lambda k, part: pl.BlockSpec((None, None, tm, SC_PART_COLS), lambda i: (k, part, i + tile_off, 0))
    pieces = [piece(k, part) for k in range(TOP_K) for part in range(SC_ROW_PARTS)]
    return pl.pallas_call(
        functools.partial(_combine_rows_kernel, final=final),
        out_shape=jax.ShapeDtypeStruct((t, D_MODEL), f32),
        grid=(t // tm,),
        in_specs=pieces + [
                  pl.BlockSpec((tm, LANES), lambda i: (i + tile_off, 0)),
                  pl.BlockSpec((tm, D_MODEL), lambda i: (i, 0)),
                  pl.BlockSpec((None, 6, D_MODEL), mod_map),
                  pl.BlockSpec((1, D_MODEL), lambda i: (0, 0))],
        out_specs=pl.BlockSpec((tm, D_MODEL), lambda i: (i, 0)),
        compiler_params=_cparams(("arbitrary",)),
        name="combine_rows",
    )(*([yg] * len(pieces)), gcol, xnew, mod_l, final_g)


def _sc_scatter_rows(planes, dest_tiles, p):
    nt, _, tm = dest_tiles.shape
    t = nt * tm
    parts, _, cols = planes.shape
    dest = dest_tiles.transpose(1, 0, 2).reshape(TOP_K, 1, t)
    idx = (dest + (jnp.arange(parts, dtype=i32) * p)[None, :, None]).reshape(1, -1)
    n = idx.shape[1]
    src = planes.reshape(parts * t, cols)
    src_windows = src.shape[0] // SC_WINDOW
    mesh = plsc.VectorSubcoreMesh(core_axis_name="core", subcore_axis_name="subcore")

    @pl.kernel(out_type=jax.ShapeDtypeStruct((parts * p, cols), planes.dtype), mesh=mesh, scratch_types=[])
    def scatter(x_hbm, i_hbm, o_hbm):
        def body(x_vmem, i_vmem):
            pltpu.sync_copy(x_vmem, o_hbm.at[i_vmem.at[0]])

        pltpu.emit_pipeline(
            body,
            grid=(n // SC_WINDOW,),
            in_specs=[pl.BlockSpec((SC_WINDOW, cols), lambda i: (i % src_windows, 0)),
                      pl.BlockSpec((1, SC_WINDOW), lambda i: (0, i))],
            out_specs=[],
            core_axis_name=("core", "subcore"),
            dimension_semantics=(pltpu.PARALLEL,),
        )(x_hbm, i_hbm)

    return scatter(src, idx).reshape(parts, p, cols)


def _moe(h2_planes, logits, router_bias_col, su, w_gate, w_up, w_down, layer):
    t = logits.shape[0]
    ids, rank, gcol, cnt = _route(logits, router_bias_col, su)
    counts = cnt[:, 0].astype(i32)
    padded = (counts + EXPERT_BLOCK - 1) // EXPERT_BLOCK * EXPERT_BLOCK
    pend = jnp.cumsum(padded)
    pstart = pend - padded
    nb = (t * TOP_K) // EXPERT_BLOCK + N_EXPERTS
    block_pos = jnp.arange(nb, dtype=i32) * EXPERT_BLOCK
    block_e = jnp.minimum(jnp.sum((pend[None, :] <= block_pos[:, None]).astype(i32), axis=1), N_EXPERTS - 1)
    block_valid = jnp.clip(pstart[block_e] + counts[block_e] - block_pos, 0, EXPERT_BLOCK).astype(i32)
    n_used = (pend[-1:] // EXPERT_BLOCK).astype(i32)
    dest_tiles = _dest(ids, rank, pstart.astype(f32)[:, None])
    buf = _sc_scatter_rows(h2_planes, dest_tiles, nb * EXPERT_BLOCK)
    y = _experts(block_e, n_used, block_valid, buf, w_gate, w_up, w_down, layer)
    return dest_tiles, gcol, y


def _pack_layer(l, w_in, conv_w, conv_b, dt_bias_f, dt_bias_b, a_log_f, a_log_b, d_skip, ssd_norm_g,
                gk_up_f, gk_bias_f, gk_up_b, gk_bias_b, gla_norm_g, w_out, norm1_g, norm2_g):
    w = w_in[l]
    z, xbc, dt, q, k, v, go, gk = jnp.split(w, [512, 1280, 1288, 1544, 1800, 2312, 2824], axis=1)
    small = jnp.concatenate([dt, gk, jnp.zeros((D_MODEL, SMALL_COLS - 24), f32)], axis=1)
    w_packed = jnp.concatenate([z, xbc, q, k, v, go, small], axis=1).astype(bf16)

    def lane_row(vec):
        return jnp.zeros((1, LANES), f32).at[0, :SSD_HEADS].set(vec)

    def sub_col(vec):
        return jnp.broadcast_to(vec[:, None], (SSD_HEADS, CHUNK)).astype(f32)

    def gup(m):
        return jnp.zeros((SMALL_COLS, GLA_QK), f32).at[SSD_HEADS:SSD_HEADS + GLA_GATE_RANK].set(m).astype(bf16)

    a_f = -jnp.exp(a_log_f[l])
    a_b = -jnp.exp(a_log_b[l])
    return dict(
        w_packed=w_packed,
        conv_w8=jnp.zeros((8, SSD_XBC), f32).at[:5].set(conv_w[l]),
        conv_b=conv_b[l][None, :],
        ssd_f=(lane_row(dt_bias_f[l]), lane_row(a_f), sub_col(dt_bias_f[l]), sub_col(a_f)),
        ssd_b=(lane_row(dt_bias_b[l]), lane_row(a_b), sub_col(dt_bias_b[l]), sub_col(a_b)),
        gla_f=(gup(gk_up_f[l]), gk_bias_f[l][None, :]),
        gla_b=(gup(gk_up_b[l]), gk_bias_b[l][None, :]),
        dsk=jnp.repeat(d_skip[l], SSD_HEAD_DIM)[None, :],
        sg=ssd_norm_g[l][None, :],
        gg=jnp.tile(gla_norm_g[l], GLA_HEADS)[None, :],
        wout=w_out[l].astype(bf16),
        n1g=norm1_g[l][None, :],
        n2g=norm2_g[l][None, :],
    )


def _constants():
    r = jnp.arange(LANES)[:, None]
    c = jnp.arange(SSD_WIDTH)[None, :]
    e_mat = ((c // SSD_HEAD_DIM) == r).astype(bf16)
    ssd_mask = ((r // SSD_STATE) == (c // (SSD_WIDTH // SSD_GROUPS))).astype(f32)
    rr = jnp.arange(GLA_WIDTH)[:, None]
    cc = jnp.arange(GLA_QK)[None, :]
    gla_mask = ((rr // GLA_VAL_DIM) == (cc // GLA_KEY_DIM)).astype(f32)
    k = jnp.arange(ROUTE_TILE)
    su = (k[:, None] < k[None, :]).astype(bf16)
    return e_mat, ssd_mask, gla_mask, su


def _mixers(streams, pk, consts, nb):
    e_mat, ssd_mask, gla_mask, _ = consts
    out = {}
    h0_f = jnp.zeros((nb, LANES, SSD_WIDTH), f32)
    h0_b = h0_f
    s0_f = jnp.zeros((nb, GLA_WIDTH, GLA_QK), f32)
    s0_b = s0_f
    for name in ('ctx', 'lat'):
        z, xs, bc, q, k, v, go, sm = streams[name]
        yf, yb, h0_f, h0_b = _ssd_scan(xs, bc, sm, h0_f, h0_b, pk['ssd_f'], pk['ssd_b'], e_mat, ssd_mask, nb=nb)
        of, ob, s0_f, s0_b = _gla_scan(q, k, v, sm, s0_f, s0_b, pk['gla_f'], pk['gla_b'], gla_mask, nb=nb)
        out[name] = (yf, yb, of, ob)
    return out


def kernel(x, c, ctx, c_ctx, w_mod, b_mod, norm1_g, norm2_g, w_in, conv_w, conv_b, dt_bias_f, dt_bias_b, a_log_f, a_log_b, d_skip, ssd_norm_g, gk_up_f, gk_bias_f, gk_up_b, gk_bias_b, gla_norm_g, w_out, w_router, router_bias, w_gate, w_up, w_down, final_norm_g):
    nb, seq, d = x.shape
    ctx_len = ctx.shape[1]
    depth = w_mod.shape[0]
    consts = _constants()
    su = consts[3]

    cvec = jnp.zeros((16, d), f32).at[:nb].set(c).at[nb].set(c_ctx)
    mod = _modulation(cvec, w_mod, b_mod).reshape(depth, 16, 6, d)
    ctx_row = nb

    perm = jnp.array([g * EXPERTS_PER_GROUP + m for m in range(EXPERTS_PER_GROUP)
                      for g in range(N_EXPERT_GROUPS)], dtype=i32)
    wr32 = jnp.zeros((d, LANES), f32).at[:, :N_EXPERTS].set(w_router[:, perm])
    wr_hi = wr32.astype(bf16)
    wr = jnp.stack([wr_hi, (wr32 - wr_hi.astype(f32)).astype(bf16)])
    bias_col = router_bias[perm][:, None]

    x2 = x.reshape(nb * seq, d)
    c2 = ctx.reshape(nb * ctx_len, d)
    lat_tiles = seq // TM_LAT

    for l in range(depth):
        last = l == depth - 1
        pk = _pack_layer(l, w_in, conv_w, conv_b, dt_bias_f, dt_bias_b, a_log_f, a_log_b, d_skip, ssd_norm_g,
                         gk_up_f, gk_bias_f, gk_up_b, gk_bias_b, gla_norm_g, w_out, norm1_g, norm2_g)
        mod_l = mod[l]
        streams = {
            'ctx': _inproj(c2, mod_l, pk['n1g'], pk['w_packed'], pk['conv_w8'], pk['conv_b'],
                           tm=ctx_len, rowlen=ctx_len, tiles_per_row=1, fixed_row=ctx_row),
            'lat': _inproj(x2, mod_l, pk['n1g'], pk['w_packed'], pk['conv_w8'], pk['conv_b'],
                           tm=TM_LAT, rowlen=GRID_W, tiles_per_row=lat_tiles, fixed_row=None),
        }
        mix = _mixers(streams, pk, consts, nb)

        def merge(name, xres, tm, tiles_per_row, fixed_row):
            z, xs, bc, q, k, v, go, sm = streams[name]
            yf, yb, of, ob = mix[name]
            return _outproj(yf, yb, xs, z, of, ob, go, xres, mod_l, pk['dsk'], pk['sg'], pk['gg'],
                            pk['wout'], pk['n2g'], wr, tm=tm, tiles_per_row=tiles_per_row, fixed_row=fixed_row)

        xn_lat, h2_lat, lg_lat = merge('lat', x2, TM_LAT, lat_tiles, None)
        if last:
            h2, logits = h2_lat, lg_lat
        else:
            xn_ctx, h2_ctx, lg_ctx = merge('ctx', c2, ctx_len, 1, ctx_row)
            h2 = jnp.concatenate([h2_ctx, h2_lat], axis=1)
            logits = jnp.concatenate([lg_ctx, lg_lat], axis=0)
        dest_tiles, gcol, y = _moe(h2, logits, bias_col, su, w_gate, w_up, w_down, l)
        yg = _sc_gather_rows(y, dest_tiles)
        lat_off = 0
        if not last:
            c2 = _combine_rows(yg, 0, gcol, xn_ctx, mod_l, final_norm_g[None, :],
                               tiles_per_row=1, fixed_row=ctx_row, final=False)
            lat_off = xn_ctx.shape[0] // MOE_TILE
        x2 = _combine_rows(yg, lat_off, gcol, xn_lat, mod_l, final_norm_g[None, :],
                           tiles_per_row=seq // MOE_TILE, fixed_row=None, final=last)
    return x2.reshape(nb, seq, d)
```

```python
import functools

import jax
import jax.numpy as jnp
from jax import lax
from jax.experimental import pallas as pl
from jax.experimental.pallas import tpu as pltpu
from jax.experimental.pallas import tpu_sc as plsc

f32 = jnp.float32
bf16 = jnp.bfloat16
i32 = jnp.int32
u32 = jnp.uint32

D_MODEL = 1024
SSD_HEADS = 8
SSD_HEAD_DIM = 64
SSD_WIDTH = 512
SSD_GROUPS = 2
SSD_STATE = 64
SSD_XBC = 768
GLA_HEADS = 4
GLA_KEY_DIM = 64
GLA_VAL_DIM = 128
GLA_QK = 256
GLA_WIDTH = 512
GLA_GATE_RANK = 16
GLA_GATE_NORM = 16.0
GRID_W = 64
N_EXPERTS = 32
N_EXPERT_GROUPS = 8
EXPERTS_PER_GROUP = 4
TOP_K = 2
EXPERT_FF = 512
NORM_EPS = 1e-6

LANES = 128
CHUNK = 128
CHUNKS_PER_STEP = 4
INTERLEAVE_WAVE = 8
ROW_SPLIT = 2
SSD_SPLIT = 3
GLA_SPLIT = 2
MAIN_COLS = 2816
SMALL_COLS = LANES
TM_LAT = 512
ROUTE_TILE = 512
MOE_TILE = 512
SC_WINDOW = 128
SC_ROW_PARTS = 2
SC_PART_COLS = D_MODEL // 2 // SC_ROW_PARTS
PACKED_COLS = D_MODEL // 2
EXPERT_BLOCK = 512
VMEM_LIMIT = 48 * 1024 * 1024

_HI = lax.Precision.HIGHEST
_NT = (((1,), (1,)), ((), ()))


def _dot(a, b, precision=None):
    return jnp.dot(a, b, preferred_element_type=f32, precision=precision)


def _silu(x):
    return x * jax.nn.sigmoid(x)


def _plane_spec(rows, block_index):
    return pl.BlockSpec((SC_ROW_PARTS, rows, SC_PART_COLS), lambda *idx: (0, block_index(*idx), 0))


def _store_planes(ref, rows, packed):
    for part in range(SC_ROW_PARTS):
        ref[part, rows, :] = packed[:, part * SC_PART_COLS:(part + 1) * SC_PART_COLS]


def _load_planes(ref, rows):
    return jnp.concatenate([ref[part, rows, :] for part in range(SC_ROW_PARTS)], axis=1)


def _pack_rows(value):
    hi = pltpu.bitcast(value[:, :PACKED_COLS].astype(bf16).astype(f32), u32)
    lo = pltpu.bitcast(value[:, PACKED_COLS:].astype(bf16).astype(f32), u32)
    return hi | (lo >> 16)


def _unpack_rows(packed):
    hi = pltpu.bitcast(packed & jnp.uint32(0xFFFF0000), f32)
    lo = pltpu.bitcast(packed << 16, f32)
    return jnp.concatenate([hi, lo], axis=1)


def _cparams(sem):
    return pltpu.CompilerParams(dimension_semantics=sem, vmem_limit_bytes=VMEM_LIMIT)


def _mod_kernel(c_ref, w_ref, b_ref, o_ref):
    sc = _silu(c_ref[...]).astype(bf16)
    o_ref[...] = _dot(sc, w_ref[...].astype(bf16)) + b_ref[...]


def _modulation(cvec, w_mod, b_mod):
    depth, d, n = w_mod.shape
    tn = 1536
    return pl.pallas_call(
        _mod_kernel,
        out_shape=jax.ShapeDtypeStruct((depth, 16, n), f32),
        grid=(depth, n // tn),
        in_specs=[pl.BlockSpec((16, d), lambda l, j: (0, 0)),
                  pl.BlockSpec((None, d, tn), lambda l, j: (l, 0, j)),
                  pl.BlockSpec((None, 1, tn), lambda l, j: (l, 0, j))],
        out_specs=pl.BlockSpec((None, 16, tn), lambda l, j: (l, 0, j)),
        compiler_params=_cparams(("arbitrary", "arbitrary")),
        name="modulation",
    )(cvec, w_mod, b_mod.reshape(depth, 1, n))


def _inproj_kernel(x_ref, mod_ref, g_ref, w_ref, cw_ref, cb_ref,
                   z_ref, xs_ref, bc_ref, q_ref, k_ref, v_ref, go_ref, sm_ref, *, rowlen):
    tm = x_ref.shape[0]
    split = ROW_SPLIT if (tm // ROW_SPLIT) % rowlen == 0 else 1
    sub = tm // split
    m = mod_ref[...]
    geff = g_ref[...] * (1.0 + m[1:2])
    cw = cw_ref[...]

    def rows_gen(rows):
        x = x_ref[rows, :]
        inv = lax.rsqrt(jnp.mean(x * x, axis=-1, keepdims=True) + NORM_EPS)
        h = ((x * inv) * geff + m[0:1]).astype(bf16)
        yield

        def proj(lo, hi):
            return _dot(h, w_ref[:, lo:hi])

        z_ref[rows, :] = proj(0, 512).astype(bf16)
        yield

        xbc = proj(512, 1280)
        yield
        pos = lax.broadcasted_iota(i32, xbc.shape, 0) & (rowlen - 1)
        acc = xbc * cw[2:3]
        for d in (-2, -1, 1, 2):
            shifted = pltpu.roll(xbc, (-d) % sub, 0)
            valid = (pos >= -d) if d < 0 else (pos <= rowlen - 1 - d)
            acc = acc + jnp.where(valid, shifted, 0.0) * cw[2 + d:3 + d]
        yield
        act = _silu(acc + cb_ref[...])
        xs_ref[rows, :] = act[:, :SSD_WIDTH].astype(bf16)
        bc_ref[rows, :] = act[:, SSD_WIDTH:].astype(bf16)
        yield

        qk = proj(1280, 1792)
        q_ref[rows, :] = qk[:, :GLA_QK].astype(bf16)
        k_ref[rows, :] = qk[:, GLA_QK:].astype(bf16)
        yield
        v_ref[rows, :] = proj(1792, 2304).astype(bf16)
        yield
        go_ref[rows, :] = proj(2304, 2816).astype(bf16)
        yield
        sm_ref[rows, :] = proj(2816, 2944)

    _run_interleaved(rows_gen(pl.ds(r * sub, sub)) for r in range(split))


def _inproj(x2d, mod_l, norm_g, w_packed, conv_w8, conv_b, *, tm, rowlen, tiles_per_row, fixed_row):
    t = x2d.shape[0]
    nt = t // tm
    if fixed_row is None:
        mod_map = lambda i: (i // tiles_per_row, 0, 0)
    else:
        mod_map = lambda i: (fixed_row, 0, 0)
    tok = lambda w: pl.BlockSpec((tm, w), lambda i: (i, 0))
    const = lambda a: pl.BlockSpec(a.shape, lambda i: (0,) * a.ndim)
    widths = (512, 512, 256, 256, 256, 512, 512)
    return pl.pallas_call(
        functools.partial(_inproj_kernel, rowlen=rowlen),
        out_shape=tuple(jax.ShapeDtypeStruct((t, w), bf16) for w in widths)
        + (jax.ShapeDtypeStruct((t, SMALL_COLS), f32),),
        grid=(nt,),
        in_specs=[tok(D_MODEL), pl.BlockSpec((None, 6, D_MODEL), mod_map), const(norm_g),
                  const(w_packed), const(conv_w8), const(conv_b)],
        out_specs=tuple(tok(w) for w in widths) + (tok(SMALL_COLS),),
        compiler_params=_cparams(("arbitrary",)),
        name="inproj",
    )(x2d, mod_l, norm_g, w_packed, conv_w8, conv_b)


def _dot_split(a, b, parts, *, split_lhs):
    rest = a if split_lhs else b
    acc = None
    for _ in range(parts):
        piece = rest.astype(bf16)
        rest = rest - piece.astype(f32)
        term = _dot(piece, b) if split_lhs else _dot(a, piece)
        acc = term if acc is None else acc + term
    return acc


def _tri_masks(q):
    r = lax.broadcasted_iota(i32, (q, q), 0)
    c = lax.broadcasted_iota(i32, (q, q), 1)
    return r >= c, r <= c


def _ssd_chunk(load, store, carry, par, e, bm, rev):
    xb, bcv, sm = load()
    dtb, a_lane, dtbt, a_sub = par
    q = xb.shape[0]
    dt_col = jax.nn.softplus(sm + dtb)
    a_col = dt_col * a_lane
    dt_row = jax.nn.softplus(sm.T[0:SSD_HEADS, :] + dtbt)
    a_row = dt_row * a_sub
    yield
    lower, upper = _tri_masks(q)
    lo_b = jnp.where(lower, 1.0, 0.0).astype(bf16)
    up_b = jnp.where(upper, 1.0, 0.0).astype(bf16)
    if not rev:
        cs_col = _dot_split(lo_b, a_col, SSD_SPLIT, split_lhs=False)
        cs_row = _dot_split(a_row, up_b, SSD_SPLIT, split_lhs=True)
        mask = lower
        a_tot = cs_col[q - 1:q, :]
    else:
        cs_col = _dot_split(up_b, a_col, SSD_SPLIT, split_lhs=False)
        cs_row = _dot_split(a_row, lo_b, SSD_SPLIT, split_lhs=True)
        mask = upper
        a_tot = cs_col[0:1, :]
    yield

    w_exp = _dot((jnp.exp(a_tot - cs_col) * dt_col).astype(bf16), e)
    ecs_exp = _dot(jnp.exp(cs_col).astype(bf16), e)
    dec_exp = _dot_split(jnp.broadcast_to(jnp.exp(a_tot), (8, LANES)), e, SSD_SPLIT, split_lhs=True)[0:1]
    yield

    xw = (xb.astype(f32) * w_exp).astype(bf16)
    b_all = bcv[:, 0:LANES]
    c_all = bcv[:, LANES:2 * LANES]
    b_t = b_all.astype(f32).T.astype(bf16)
    yield
    s_upd = _dot(b_t, xw)
    yield

    lane = lax.broadcasted_iota(i32, (q, LANES), 1)
    zero_b = jnp.zeros((q, LANES), bf16)
    ys = []
    for g in range(SSD_GROUPS):
        in_g = (lane >= SSD_STATE * g) & (lane < SSD_STATE * (g + 1))
        cb = _dot(jnp.where(in_g, c_all, zero_b), b_t)
        yield
        for pp in range(2):
            h0 = 4 * g + 2 * pp
            ms = []
            for h in (h0, h0 + 1):
                seg = cs_col[:, h:h + 1] - cs_row[h:h + 1, :]
                dec = jnp.exp(jnp.where(mask, seg, -1e30))
                ms.append((cb * dec * dt_row[h:h + 1, :]).astype(bf16))
            xp = xb[:, h0 * SSD_HEAD_DIM:h0 * SSD_HEAD_DIM + LANES]
            rhs = jnp.concatenate([jnp.where(lane < SSD_HEAD_DIM, xp, zero_b),
                                   jnp.where(lane >= SSD_HEAD_DIM, xp, zero_b)], axis=0)
            yield
            ys.append(_dot(jnp.concatenate(ms, axis=1), rhs))
            yield
    y_intra = jnp.concatenate(ys, axis=1)
    yield

    s_old = carry['s']
    carry['s'] = (s_old * dec_exp + s_upd) * bm
    store((y_intra + _dot(c_all, s_old.astype(bf16)) * ecs_exp).astype(bf16))


def _run_interleaved(gens):
    gens = list(gens)
    done = object()
    for w in range(0, len(gens), INTERLEAVE_WAVE):
        live = gens[w:w + INTERLEAVE_WAVE]
        while live:
            live = [g for g in live if next(g, done) is not done]


def _chunk_rows(cps):
    return [(pl.ds(ci * CHUNK, CHUNK), pl.ds((cps - 1 - ci) * CHUNK, CHUNK)) for ci in range(cps)]


def _ssd_kernel(xf_ref, bcf_ref, smf_ref, xr_ref, bcr_ref, smr_ref, h0f_ref, h0r_ref,
                dtbf_ref, af_ref, dtbtf_ref, atf_ref, dtbr_ref, ar_ref, dtbtr_ref, atr_ref, e_ref, bm_ref,
                yf_ref, yr_ref, hff_ref, hfr_ref, st_f, st_r, *, cps, nblk):
    j = pl.program_id(1)

    @pl.when(j == 0)
    def _():
        st_f[...] = h0f_ref[...]
        st_r[...] = h0r_ref[...]

    e = e_ref[...]
    bm = bm_ref[...]
    par_f = (dtbf_ref[...], af_ref[...], dtbtf_ref[...], atf_ref[...])
    par_r = (dtbr_ref[...], ar_ref[...], dtbtr_ref[...], atr_ref[...])
    carry_f = {'s': st_f[...]}
    carry_r = {'s': st_r[...]}

    def chunk(refs, out_ref, rows, carry, par, rev):
        def store(y):
            out_ref[rows, :] = y
        return _ssd_chunk(lambda: tuple(r[rows, :] for r in refs), store, carry, par, e, bm, rev)

    gens = []
    for rows_f, rows_r in _chunk_rows(cps):
        gens.append(chunk((xf_ref, bcf_ref, smf_ref), yf_ref, rows_f, carry_f, par_f, False))
        gens.append(chunk((xr_ref, bcr_ref, smr_ref), yr_ref, rows_r, carry_r, par_r, True))
    _run_interleaved(gens)
    st_f[...] = carry_f['s']
    st_r[...] = carry_r['s']

    @pl.when(j == nblk - 1)
    def _():
        hff_ref[...] = carry_f['s']
        hfr_ref[...] = carry_r['s']


def _scan_specs(t, nb):
    per_row = t // nb
    cps = min(CHUNKS_PER_STEP, per_row // CHUNK)
    nblk = per_row // (cps * CHUNK)
    fmap = lambda b, j: (b * nblk + j, 0)
    rmap = lambda b, j: (b * nblk + nblk - 1 - j, 0)
    return cps, nblk, fmap, rmap


def _ssd_scan(xs, bc, sm, h0_f, h0_r, par_f, par_r, e_mat, bmask, *, nb):
    t = xs.shape[0]
    cps, nblk, fmap, rmap = _scan_specs(t, nb)
    blk = cps * CHUNK
    const = lambda a: pl.BlockSpec(a.shape, lambda b, j: (0,) * a.ndim)
    st = pl.BlockSpec((None, LANES, SSD_WIDTH), lambda b, j: (b, 0, 0))
    widths = (SSD_WIDTH, 2 * LANES, SMALL_COLS)
    tok_in = [pl.BlockSpec((blk, w), m) for m in (fmap, rmap) for w in widths]
    consts = (*par_f, *par_r, e_mat, bmask)
    return pl.pallas_call(
        functools.partial(_ssd_kernel, cps=cps, nblk=nblk),
        out_shape=(jax.ShapeDtypeStruct((t, SSD_WIDTH), bf16), jax.ShapeDtypeStruct((t, SSD_WIDTH), bf16),
                   jax.ShapeDtypeStruct((nb, LANES, SSD_WIDTH), f32),
                   jax.ShapeDtypeStruct((nb, LANES, SSD_WIDTH), f32)),
        grid=(nb, nblk),
        in_specs=tok_in + [st, st] + [const(a) for a in consts],
        out_specs=(pl.BlockSpec((blk, SSD_WIDTH), fmap), pl.BlockSpec((blk, SSD_WIDTH), rmap), st, st),
        scratch_shapes=[pltpu.VMEM((LANES, SSD_WIDTH), f32), pltpu.VMEM((LANES, SSD_WIDTH), f32)],
        compiler_params=_cparams(("arbitrary", "arbitrary")),
        name="ssd",
    )(xs, bc, sm, xs, bc, sm, h0_f, h0_r, *consts)


def _gla_chunk(load, store, carry, par, bm, rev):
    qb, kb, vb, sm = load()
    gup, gbias = par
    q = qb.shape[0]
    gp = _dot(sm.astype(bf16), gup) + gbias
    yield
    g = jax.nn.log_sigmoid(gp) * (1.0 / GLA_GATE_NORM)
    yield
    lower, upper = _tri_masks(q)
    if not rev:
        b = _dot_split(jnp.where(lower, 1.0, 0.0).astype(bf16), g, GLA_SPLIT, split_lhs=False)
        mask, mid = lower, q // 2 - 1
        b_tot = b[q - 1:q, :]
    else:
        b = _dot_split(jnp.where(upper, 1.0, 0.0).astype(bf16), g, GLA_SPLIT, split_lhs=False)
        mask, mid = upper, q // 2
        b_tot = b[0:1, :]
    b_mid = b[mid:mid + 1, :]
    yield

    qf = qb.astype(f32) * (GLA_KEY_DIM ** -0.5)
    kf = kb.astype(f32)
    qd = (qf * jnp.exp(b - b_mid)).astype(bf16)
    yield
    ki = (kf * jnp.exp(b_mid - b)).astype(bf16)
    yield
    q_st = (qf * jnp.exp(b)).astype(bf16)
    yield
    k_end = (kf * jnp.exp(b_tot - b)).astype(bf16)
    dec = jnp.exp(b_tot)
    yield
    v_t = vb.astype(f32).T.astype(bf16)
    yield
    s_upd = _dot(v_t, k_end)
    yield

    lane = lax.broadcasted_iota(i32, (q, LANES), 1)
    zero_b = jnp.zeros((q, LANES), bf16)
    outs = []
    for h in range(GLA_HEADS):
        p, hh = divmod(h, 2)
        qp = qd[:, p * LANES:(p + 1) * LANES]
        kp = ki[:, p * LANES:(p + 1) * LANES]
        in_h = (lane >= GLA_KEY_DIM * hh) & (lane < GLA_KEY_DIM * (hh + 1))
        s = lax.dot_general(jnp.where(in_h, qp, zero_b), kp, _NT, preferred_element_type=f32)
        yield
        attn = jnp.where(mask, s, 0.0).astype(bf16)
        outs.append(_dot(attn, vb[:, h * GLA_VAL_DIM:(h + 1) * GLA_VAL_DIM]))
        yield
    o_intra = jnp.concatenate(outs, axis=1)
    yield

    s_old = carry['s']
    carry['s'] = (s_old * dec + s_upd) * bm
    o_inter = lax.dot_general(q_st, s_old.astype(bf16), _NT, preferred_element_type=f32)
    store((o_intra + o_inter).astype(bf16))


def _gla_kernel(qf_ref, kf_ref, vf_ref, smf_ref, qr_ref, kr_ref, vr_ref, smr_ref, s0f_ref, s0r_ref,
                gupf_ref, gbf_ref, gupr_ref, gbr_ref, bm_ref,
                of_ref, or_ref, sff_ref, sfr_ref, st_f, st_r, *, cps, nblk):
    j = pl.program_id(1)

    @pl.when(j == 0)
    def _():
        st_f[...] = s0f_ref[...]
        st_r[...] = s0r_ref[...]

    bm = bm_ref[...]
    par_f = (gupf_ref[...], gbf_ref[...])
    par_r = (gupr_ref[...], gbr_ref[...])
    carry_f = {'s': st_f[...]}
    carry_r = {'s': st_r[...]}

    def chunk(refs, out_ref, rows, carry, par, rev):
        def store(o):
            out_ref[rows, :] = o
        return _gla_chunk(lambda: tuple(r[rows, :] for r in refs), store, carry, par, bm, rev)

    gens = []
    for rows_f, rows_r in _chunk_rows(cps):
        gens.append(chunk((qf_ref, kf_ref, vf_ref, smf_ref), of_ref, rows_f, carry_f, par_f, False))
        gens.append(chunk((qr_ref, kr_ref, vr_ref, smr_ref), or_ref, rows_r, carry_r, par_r, True))
    _run_interleaved(gens)
    st_f[...] = carry_f['s']
    st_r[...] = carry_r['s']

    @pl.when(j == nblk - 1)
    def _():
        sff_ref[...] = carry_f['s']
        sfr_ref[...] = carry_r['s']


def _gla_scan(qa, ka, va, sm, s0_f, s0_r, par_f, par_r, bmask, *, nb):
    t = qa.shape[0]
    cps, nblk, fmap, rmap = _scan_specs(t, nb)
    blk = cps * CHUNK
    const = lambda a: pl.BlockSpec(a.shape, lambda b, j: (0,) * a.ndim)
    st = pl.BlockSpec((None, GLA_WIDTH, GLA_QK), lambda b, j: (b, 0, 0))
    widths = (GLA_QK, GLA_QK, GLA_WIDTH, SMALL_COLS)
    tok_in = [pl.BlockSpec((blk, w), m) for m in (fmap, rmap) for w in widths]
    consts = (*par_f, *par_r, bmask)
    return pl.pallas_call(
        functools.partial(_gla_kernel, cps=cps, nblk=nblk),
        out_shape=(jax.ShapeDtypeStruct((t, GLA_WIDTH), bf16), jax.ShapeDtypeStruct((t, GLA_WIDTH), bf16),
                   jax.ShapeDtypeStruct((nb, GLA_WIDTH, GLA_QK), f32),
                   jax.ShapeDtypeStruct((nb, GLA_WIDTH, GLA_QK), f32)),
        grid=(nb, nblk),
        in_specs=tok_in + [st, st] + [const(a) for a in consts],
        out_specs=(pl.BlockSpec((blk, GLA_WIDTH), fmap), pl.BlockSpec((blk, GLA_WIDTH), rmap), st, st),
        scratch_shapes=[pltpu.VMEM((GLA_WIDTH, GLA_QK), f32), pltpu.VMEM((GLA_WIDTH, GLA_QK), f32)],
        compiler_params=_cparams(("arbitrary", "arbitrary")),
        name="gla",
    )(qa, ka, va, sm, qa, ka, va, sm, s0_f, s0_r, *consts)


def _outproj_kernel(yf_ref, yb_ref, xs_ref, z_ref, of_ref, ob_ref, go_ref, xres_ref, mod_ref,
                    dsk_ref, sg_ref, gg_ref, wout_ref, n2g_ref, wr_ref,
                    xnew_ref, h2_ref, lg_ref):
    tm = xres_ref.shape[0]
    sub = tm // ROW_SPLIT
    m = mod_ref[...]
    h2_gain = n2g_ref[...] * (1.0 + m[4:5])

    def rows_gen(rows):
        up = lambda r: r[rows, :].astype(f32)
        y = up(yf_ref) + up(yb_ref) + dsk_ref[...] * up(xs_ref)
        yield
        y = y * _silu(up(z_ref))
        yield
        y = y * lax.rsqrt(jnp.mean(y * y, axis=-1, keepdims=True) + NORM_EPS) * sg_ref[...]
        yield
        o = up(of_ref) + up(ob_ref)
        parts = []
        for h in range(GLA_HEADS):
            oh = o[:, h * GLA_VAL_DIM:(h + 1) * GLA_VAL_DIM]
            parts.append(oh * lax.rsqrt(jnp.mean(oh * oh, axis=-1, keepdims=True) + NORM_EPS))
        yield
        o = jnp.concatenate(parts, axis=1) * gg_ref[...] * _silu(up(go_ref))
        yield
        cat = jnp.concatenate([y, o], axis=1).astype(bf16)
        mix = _dot(cat, wout_ref[...])
        yield
        xn = xres_ref[rows, :] + m[2:3] * mix
        xnew_ref[rows, :] = xn
        yield
        inv = lax.rsqrt(jnp.mean(xn * xn, axis=-1, keepdims=True) + NORM_EPS)
        h2 = (xn * inv) * h2_gain + m[3:4]
        _store_planes(h2_ref, rows, _pack_rows(h2))
        yield
        h2_hi = h2.astype(bf16)
        h2_lo = (h2 - h2_hi.astype(f32)).astype(bf16)
        wr_hi = wr_ref[0]
        lg_ref[rows, :] = _dot(h2_hi, wr_hi) + _dot(h2_hi, wr_ref[1]) + _dot(h2_lo, wr_hi)

    _run_interleaved(rows_gen(pl.ds(r * sub, sub)) for r in range(ROW_SPLIT))


def _outproj(yf, yb, xs, z, of, ob, go, xres, mod_l, dsk, sg, gg, wout, n2g, wr,
             *, tm, tiles_per_row, fixed_row):
    t = xres.shape[0]
    if fixed_row is None:
        mod_map = lambda i: (i // tiles_per_row, 0, 0)
    else:
        mod_map = lambda i: (fixed_row, 0, 0)
    tok = lambda w: pl.BlockSpec((tm, w), lambda i: (i, 0))
    const = lambda a: pl.BlockSpec(a.shape, lambda i: (0,) * a.ndim)
    return pl.pallas_call(
        _outproj_kernel,
        out_shape=(jax.ShapeDtypeStruct((t, D_MODEL), f32), jax.ShapeDtypeStruct((SC_ROW_PARTS, t, SC_PART_COLS), u32),
                   jax.ShapeDtypeStruct((t, LANES), f32)),
        grid=(t // tm,),
        in_specs=[tok(512)] * 7 + [tok(D_MODEL), pl.BlockSpec((None, 6, D_MODEL), mod_map),
                                   const(dsk), const(sg), const(gg), const(wout), const(n2g), const(wr)],
        out_specs=(tok(D_MODEL), _plane_spec(tm, lambda i: i), tok(LANES)),
        compiler_params=_cparams(("arbitrary",)),
        name="outproj",
    )(yf, yb, xs, z, of, ob, go, xres, mod_l, dsk, sg, gg, wout, n2g, wr)


def _route_kernel(lg_ref, bias_ref, su_ref, ids_ref, rank_ref, gcol_ref, cnt_ref, carry):
    i = pl.program_id(0)
    tm = lg_ref.shape[0]

    @pl.when(i == 0)
    def _():
        carry[...] = jnp.zeros_like(carry)

    s = jax.nn.sigmoid(lg_ref[...].T[0:N_EXPERTS, :])
    sel = s + bias_ref[...]
    a = [sel[8 * m:8 * (m + 1)] for m in range(EXPERTS_PER_GROUP)]
    sv = [s[8 * m:8 * (m + 1)] for m in range(EXPERTS_PER_GROUP)]
    hi01, lo01 = jnp.maximum(a[0], a[1]), jnp.minimum(a[0], a[1])
    hi23, lo23 = jnp.maximum(a[2], a[3]), jnp.minimum(a[2], a[3])
    gscore = jnp.maximum(hi01, hi23) + jnp.maximum(jnp.minimum(hi01, hi23), jnp.maximum(lo01, lo23))
    giota = lax.broadcasted_iota(i32, gscore.shape, 0)
    gmax = jnp.max(gscore, axis=0, keepdims=True)
    gidx = jnp.min(jnp.where(gscore == gmax, giota, N_EXPERT_GROUPS), axis=0, keepdims=True)
    pick = giota == gidx
    v = [jnp.sum(jnp.where(pick, a[m], 0.0), axis=0, keepdims=True) for m in range(4)]
    w = [jnp.sum(jnp.where(pick, sv[m], 0.0), axis=0, keepdims=True) for m in range(4)]

    def first_max(vals, excluded):
        best = vals[0]
        for m in range(1, 4):
            best = jnp.maximum(best, vals[m])
        idx = jnp.full(best.shape, 3, i32)
        for m in (2, 1, 0):
            hit = vals[m] == best
            if excluded is not None:
                hit = hit & (excluded != m)
            idx = jnp.where(hit, m, idx)
        return idx

    i1 = first_max(v, None)
    v_rest = [jnp.where(i1 == m, -jnp.inf, v[m]) for m in range(4)]
    i2 = first_max(v_rest, i1)

    def take(vals, idx):
        out = vals[3]
        for m in (2, 1, 0):
            out = jnp.where(idx == m, vals[m], out)
        return out

    w1, w2 = take(w, i1), take(w, i2)
    denom = w1 + w2
    id1 = gidx * EXPERTS_PER_GROUP + i1
    id2 = gidx * EXPERTS_PER_GROUP + i2
    row2 = lax.broadcasted_iota(i32, (TOP_K, tm), 0)
    ids_ref[...] = jnp.where(row2 == 0, id1, id2)

    eiota = lax.broadcasted_iota(i32, (N_EXPERTS, tm), 0)
    hit1, hit2 = eiota == id1, eiota == id2
    onehot = jnp.where(hit1, 1.0, 0.0) + jnp.where(hit2, 1.0, 0.0)
    before = _dot(onehot.astype(bf16), su_ref[...]) + carry[...]
    r1 = jnp.sum(jnp.where(hit1, before, 0.0), axis=0, keepdims=True)
    r2 = jnp.sum(jnp.where(hit2, before, 0.0), axis=0, keepdims=True)
    rank_ref[...] = jnp.where(row2 == 0, r1, r2).astype(i32)
    new_carry = carry[...] + jnp.sum(onehot, axis=1, keepdims=True)
    carry[...] = new_carry
    cnt_ref[...] = jnp.broadcast_to(new_carry, cnt_ref.shape)

    rows = lax.broadcasted_iota(i32, (LANES, tm), 0)
    gates = jnp.where(rows == 0, w1 / denom, jnp.where(rows == 1, w2 / denom, 0.0))
    gcol_ref[...] = gates.T


def _route(logits, bias_col, su):
    t = logits.shape[0]
    tm = ROUTE_TILE
    return pl.pallas_call(
        _route_kernel,
        out_shape=(jax.ShapeDtypeStruct((2, t), i32), jax.ShapeDtypeStruct((2, t), i32),
                   jax.ShapeDtypeStruct((t, LANES), f32), jax.ShapeDtypeStruct((N_EXPERTS, LANES), f32)),
        grid=(t // tm,),
        in_specs=[pl.BlockSpec((tm, LANES), lambda i: (i, 0)),
                  pl.BlockSpec((N_EXPERTS, 1), lambda i: (0, 0)),
                  pl.BlockSpec((tm, tm), lambda i: (0, 0))],
        out_specs=(pl.BlockSpec((2, tm), lambda i: (0, i)), pl.BlockSpec((2, tm), lambda i: (0, i)),
                   pl.BlockSpec((tm, LANES), lambda i: (i, 0)),
                   pl.BlockSpec((N_EXPERTS, LANES), lambda i: (0, 0))),
        scratch_shapes=[pltpu.VMEM((N_EXPERTS, 1), f32)],
        compiler_params=_cparams(("arbitrary",)),
        name="route",
    )(logits, bias_col, su)


def _dest_kernel(ids_ref, rank_ref, pstart_ref, idx_ref, *, p):
    ids = ids_ref[...]
    tm = ids.shape[1]
    eiota = lax.broadcasted_iota(i32, (N_EXPERTS, tm), 0)
    ps = pstart_ref[...]
    rank = rank_ref[...]
    row = lax.broadcasted_iota(i32, (TOP_K * SC_ROW_PARTS, tm), 0)
    out = jnp.zeros((TOP_K * SC_ROW_PARTS, tm), i32)
    for k in range(TOP_K):
        start = jnp.sum(jnp.where(eiota == ids[k:k + 1, :], ps, 0.0), axis=0, keepdims=True).astype(i32)
        dest = start + rank[k:k + 1, :]
        for part in range(SC_ROW_PARTS):
            out = jnp.where(row == k * SC_ROW_PARTS + part, dest + part * p, out)
    idx_ref[...] = out


def _dest(ids, rank, pstart_col, p):
    t = ids.shape[1]
    tm = MOE_TILE
    return pl.pallas_call(
        functools.partial(_dest_kernel, p=p),
        out_shape=jax.ShapeDtypeStruct((TOP_K * SC_ROW_PARTS, t), i32),
        grid=(t // tm,),
        in_specs=[pl.BlockSpec((TOP_K, tm), lambda i: (0, i)), pl.BlockSpec((TOP_K, tm), lambda i: (0, i)),
                  pl.BlockSpec((N_EXPERTS, 1), lambda i: (0, 0))],
        out_specs=pl.BlockSpec((TOP_K * SC_ROW_PARTS, tm), lambda i: (0, i)),
        compiler_params=_cparams(("arbitrary",)),
        name="dest",
    )(ids, rank, pstart_col).reshape(1, -1)


def _expert_kernel(be_ref, nu_ref, nv_ref, x_ref, wg_ref, wu_ref, wd_ref, y_ref, wg_b, wu_b, wd_b):
    i = pl.program_id(0)
    fresh = jnp.logical_or(i == 0, be_ref[i] != be_ref[jnp.maximum(i - 1, 0)])

    @pl.when(jnp.logical_and(fresh, i < nu_ref[0]))
    def _():
        wg_b[...] = wg_ref[...].astype(bf16)
        wu_b[...] = wu_ref[...].astype(bf16)
        wd_b[...] = wd_ref[...].astype(bf16)

    @pl.when(i < nu_ref[0])
    def _():
        sub = x_ref.shape[1] // ROW_SPLIT
        n_valid = nv_ref[i]

        def rows_gen(r):
            rows = pl.ds(r * sub, sub)
            packed_x = _load_planes(x_ref, rows)
            row = lax.broadcasted_iota(i32, packed_x.shape, 0) + r * sub
            xb = _unpack_rows(jnp.where(row < n_valid, packed_x, jnp.uint32(0))).astype(bf16)
            yield
            g = _dot(xb, wg_b[...])
            yield
            u = _dot(xb, wu_b[...])
            yield
            hmid = (_silu(g) * u).astype(bf16)
            yield
            _store_planes(y_ref, rows, _pack_rows(_dot(hmid, wd_b[...])))

        _run_interleaved(rows_gen(r) for r in range(ROW_SPLIT))

    @pl.when(i >= nu_ref[0])
    def _():
        y_ref[...] = jnp.zeros_like(y_ref)


def _experts(block_e, n_used, block_valid, buf, w_gate, w_up, w_down, layer):
    p = buf.shape[1]
    nb = p // EXPERT_BLOCK
    d, ff = w_gate.shape[-2:]
    row_blocks = _plane_spec(EXPERT_BLOCK, lambda i, be, nu, nv: i)
    weights = lambda a, b: pl.BlockSpec((None, None, a, b), lambda i, be, nu, nv: (layer, be[i], 0, 0))
    return pl.pallas_call(
        _expert_kernel,
        out_shape=jax.ShapeDtypeStruct(buf.shape, u32),
        grid_spec=pltpu.PrefetchScalarGridSpec(
            num_scalar_prefetch=3,
            grid=(nb,),
            in_specs=[row_blocks, weights(d, ff), weights(d, ff), weights(ff, d)],
            out_specs=row_blocks,
            scratch_shapes=[pltpu.VMEM((d, ff), bf16), pltpu.VMEM((d, ff), bf16), pltpu.VMEM((ff, d), bf16)],
        ),
        compiler_params=_cparams(("arbitrary",)),
        name="experts",
    )(block_e, n_used, block_valid, buf, w_gate, w_up, w_down)


def _sc_gather_rows(planes, idx):
    parts, p, cols = planes.shape
    n = idx.shape[1]
    t = n // (TOP_K * parts)
    halves = planes.reshape(parts * p, cols)
    mesh = plsc.VectorSubcoreMesh(core_axis_name="core", subcore_axis_name="subcore")

    @pl.kernel(out_type=jax.ShapeDtypeStruct((n, cols), planes.dtype), mesh=mesh, scratch_types=[])
    def gather(x_hbm, i_hbm, o_hbm):
        def body(i_vmem, o_vmem):
            pltpu.sync_copy(x_hbm.at[i_vmem.at[0]], o_vmem)

        pltpu.emit_pipeline(
            body,
            grid=(n // SC_WINDOW,),
            in_specs=[pl.BlockSpec((1, SC_WINDOW), lambda i: (0, i))],
            out_specs=[pl.BlockSpec((SC_WINDOW, cols), lambda i: (i, 0))],
            core_axis_name=("core", "subcore"),
            dimension_semantics=(pltpu.PARALLEL,),
        )(i_hbm, o_hbm)

    return gather(halves, idx).reshape(TOP_K, parts, t, cols)


def _combine_rows_kernel(*refs, final):
    part_refs = refs[:TOP_K * SC_ROW_PARTS]
    gcol_ref, x_ref, mod_ref, fg_ref, o_ref = refs[TOP_K * SC_ROW_PARTS:]
    gc = gcol_ref[...]
    rows = [_unpack_rows(jnp.concatenate([r[...] for r in part_refs[k * SC_ROW_PARTS:(k + 1) * SC_ROW_PARTS]], axis=1))
            for k in range(TOP_K)]
    ffn = gc[:, 0:1] * rows[0] + gc[:, 1:2] * rows[1]
    x = x_ref[...] + mod_ref[...][5:6] * ffn
    if final:
        x = x * lax.rsqrt(jnp.mean(x * x, axis=-1, keepdims=True) + NORM_EPS) * fg_ref[...]
    o_ref[...] = x


def _combine_rows(yg, tile_off, gcol, xnew, mod_l, final_g, *, tiles_per_row, fixed_row, final):
    tm = MOE_TILE
    t = xnew.shape[0]
    if fixed_row is None:
        mod_map = lambda i: (i // tiles_per_row, 0, 0)
    else:
        mod_map = lambda i: (fixed_row, 0, 0)
    piece = lambda k, part: pl.BlockSpec((None, None, tm, SC_PART_COLS), lambda i: (k, part, i + tile_off, 0))
    pieces = [piece(k, part) for k in range(TOP_K) for part in range(SC_ROW_PARTS)]
    return pl.pallas_call(
        functools.partial(_combine_rows_kernel, final=final),
        out_shape=jax.ShapeDtypeStruct((t, D_MODEL), f32),
        grid=(t // tm,),
        in_specs=pieces + [
                  pl.BlockSpec((tm, LANES), lambda i: (i + tile_off, 0)),
                  pl.BlockSpec((tm, D_MODEL), lambda i: (i, 0)),
                  pl.BlockSpec((None, 6, D_MODEL), mod_map),
                  pl.BlockSpec((1, D_MODEL), lambda i: (0, 0))],
        out_specs=pl.BlockSpec((tm, D_MODEL), lambda i: (i, 0)),
        compiler_params=_cparams(("arbitrary",)),
        name="combine_rows",
    )(*([yg] * len(pieces)), gcol, xnew, mod_l, final_g)


def _sc_scatter_rows(planes, idx, p):
    parts, t, cols = planes.shape
    n = idx.shape[1]
    src = planes.reshape(parts * t, cols)
    src_windows = src.shape[0] // SC_WINDOW
    mesh = plsc.VectorSubcoreMesh(core_axis_name="core", subcore_axis_name="subcore")

    @pl.kernel(out_type=jax.ShapeDtypeStruct((parts * p, cols), planes.dtype), mesh=mesh, scratch_types=[])
    def scatter(x_hbm, i_hbm, o_hbm):
        def body(x_vmem, i_vmem):
            pltpu.sync_copy(x_vmem, o_hbm.at[i_vmem.at[0]])

        pltpu.emit_pipeline(
            body,
            grid=(n // SC_WINDOW,),
            in_specs=[pl.BlockSpec((SC_WINDOW, cols), lambda i: (i % src_windows, 0)),
                      pl.BlockSpec((1, SC_WINDOW), lambda i: (0, i))],
            out_specs=[],
            core_axis_name=("core", "subcore"),
            dimension_semantics=(pltpu.PARALLEL,),
        )(x_hbm, i_hbm)

    return scatter(src, idx).reshape(parts, p, cols)


def _moe(h2_planes, logits, router_bias_col, su, w_gate, w_up, w_down, layer):
    t = logits.shape[0]
    ids, rank, gcol, cnt = _route(logits, router_bias_col, su)
    counts = cnt[:, 0].astype(i32)
    padded = (counts + EXPERT_BLOCK - 1) // EXPERT_BLOCK * EXPERT_BLOCK
    pend = jnp.cumsum(padded)
    pstart = pend - padded
    nb = (t * TOP_K) // EXPERT_BLOCK + N_EXPERTS
    block_pos = jnp.arange(nb, dtype=i32) * EXPERT_BLOCK
    block_e = jnp.minimum(jnp.sum((pend[None, :] <= block_pos[:, None]).astype(i32), axis=1), N_EXPERTS - 1)
    block_valid = jnp.clip(pstart[block_e] + counts[block_e] - block_pos, 0, EXPERT_BLOCK).astype(i32)
    n_used = (pend[-1:] // EXPERT_BLOCK).astype(i32)
    p = nb * EXPERT_BLOCK
    idx = _dest(ids, rank, pstart.astype(f32)[:, None], p)
    buf = _sc_scatter_rows(h2_planes, idx, p)
    y = _experts(block_e, n_used, block_valid, buf, w_gate, w_up, w_down, layer)
    return idx, gcol, y


def _pack_layer(l, w_in, conv_w, conv_b, dt_bias_f, dt_bias_b, a_log_f, a_log_b, d_skip, ssd_norm_g,
                gk_up_f, gk_bias_f, gk_up_b, gk_bias_b, gla_norm_g, w_out, norm1_g, norm2_g):
    w = w_in[l]
    z, xbc, dt, q, k, v, go, gk = jnp.split(w, [512, 1280, 1288, 1544, 1800, 2312, 2824], axis=1)
    small = jnp.concatenate([dt, gk, jnp.zeros((D_MODEL, SMALL_COLS - 24), f32)], axis=1)
    w_packed = jnp.concatenate([z, xbc, q, k, v, go, small], axis=1).astype(bf16)

    def lane_row(vec):
        return jnp.zeros((1, LANES), f32).at[0, :SSD_HEADS].set(vec)

    def sub_col(vec):
        return jnp.broadcast_to(vec[:, None], (SSD_HEADS, CHUNK)).astype(f32)

    def gup(m):
        return jnp.zeros((SMALL_COLS, GLA_QK), f32).at[SSD_HEADS:SSD_HEADS + GLA_GATE_RANK].set(m).astype(bf16)

    a_f = -jnp.exp(a_log_f[l])
    a_b = -jnp.exp(a_log_b[l])
    return dict(
        w_packed=w_packed,
        conv_w8=jnp.zeros((8, SSD_XBC), f32).at[:5].set(conv_w[l]),
        conv_b=conv_b[l][None, :],
        ssd_f=(lane_row(dt_bias_f[l]), lane_row(a_f), sub_col(dt_bias_f[l]), sub_col(a_f)),
        ssd_b=(lane_row(dt_bias_b[l]), lane_row(a_b), sub_col(dt_bias_b[l]), sub_col(a_b)),
        gla_f=(gup(gk_up_f[l]), gk_bias_f[l][None, :]),
        gla_b=(gup(gk_up_b[l]), gk_bias_b[l][None, :]),
        dsk=jnp.repeat(d_skip[l], SSD_HEAD_DIM)[None, :],
        sg=ssd_norm_g[l][None, :],
        gg=jnp.tile(gla_norm_g[l], GLA_HEADS)[None, :],
        wout=w_out[l].astype(bf16),
        n1g=norm1_g[l][None, :],
        n2g=norm2_g[l][None, :],
    )


def _constants():
    r = jnp.arange(LANES)[:, None]
    c = jnp.arange(SSD_WIDTH)[None, :]
    e_mat = ((c // SSD_HEAD_DIM) == r).astype(bf16)
    ssd_mask = ((r // SSD_STATE) == (c // (SSD_WIDTH // SSD_GROUPS))).astype(f32)
    rr = jnp.arange(GLA_WIDTH)[:, None]
    cc = jnp.arange(GLA_QK)[None, :]
    gla_mask = ((rr // GLA_VAL_DIM) == (cc // GLA_KEY_DIM)).astype(f32)
    k = jnp.arange(ROUTE_TILE)
    su = (k[:, None] < k[None, :]).astype(bf16)
    return e_mat, ssd_mask, gla_mask, su


def _mixers(streams, pk, consts, nb):
    e_mat, ssd_mask, gla_mask, _ = consts
    out = {}
    h0_f = jnp.zeros((nb, LANES, SSD_WIDTH), f32)
    h0_b = h0_f
    s0_f = jnp.zeros((nb, GLA_WIDTH, GLA_QK), f32)
    s0_b = s0_f
    for name in ('ctx', 'lat'):
        z, xs, bc, q, k, v, go, sm = streams[name]
        yf, yb, h0_f, h0_b = _ssd_scan(xs, bc, sm, h0_f, h0_b, pk['ssd_f'], pk['ssd_b'], e_mat, ssd_mask, nb=nb)
        of, ob, s0_f, s0_b = _gla_scan(q, k, v, sm, s0_f, s0_b, pk['gla_f'], pk['gla_b'], gla_mask, nb=nb)
        out[name] = (yf, yb, of, ob)
    return out


def kernel(x, c, ctx, c_ctx, w_mod, b_mod, norm1_g, norm2_g, w_in, conv_w, conv_b, dt_bias_f, dt_bias_b, a_log_f, a_log_b, d_skip, ssd_norm_g, gk_up_f, gk_bias_f, gk_up_b, gk_bias_b, gla_norm_g, w_out, w_router, router_bias, w_gate, w_up, w_down, final_norm_g):
    nb, seq, d = x.shape
    ctx_len = ctx.shape[1]
    depth = w_mod.shape[0]
    consts = _constants()
    su = consts[3]

    cvec = jnp.zeros((16, d), f32).at[:nb].set(c).at[nb].set(c_ctx)
    mod = _modulation(cvec, w_mod, b_mod).reshape(depth, 16, 6, d)
    ctx_row = nb

    perm = jnp.array([g * EXPERTS_PER_GROUP + m for m in range(EXPERTS_PER_GROUP)
                      for g in range(N_EXPERT_GROUPS)], dtype=i32)
    wr32 = jnp.zeros((d, LANES), f32).at[:, :N_EXPERTS].set(w_router[:, perm])
    wr_hi = wr32.astype(bf16)
    wr = jnp.stack([wr_hi, (wr32 - wr_hi.astype(f32)).astype(bf16)])
    bias_col = router_bias[perm][:, None]

    x2 = x.reshape(nb * seq, d)
    c2 = ctx.reshape(nb * ctx_len, d)
    lat_tiles = seq // TM_LAT

    for l in range(depth):
        last = l == depth - 1
        pk = _pack_layer(l, w_in, conv_w, conv_b, dt_bias_f, dt_bias_b, a_log_f, a_log_b, d_skip, ssd_norm_g,
                         gk_up_f, gk_bias_f, gk_up_b, gk_bias_b, gla_norm_g, w_out, norm1_g, norm2_g)
        mod_l = mod[l]
        streams = {
            'ctx': _inproj(c2, mod_l, pk['n1g'], pk['w_packed'], pk['conv_w8'], pk['conv_b'],
                           tm=ctx_len, rowlen=ctx_len, tiles_per_row=1, fixed_row=ctx_row),
            'lat': _inproj(x2, mod_l, pk['n1g'], pk['w_packed'], pk['conv_w8'], pk['conv_b'],
                           tm=TM_LAT, rowlen=GRID_W, tiles_per_row=lat_tiles, fixed_row=None),
        }
        mix = _mixers(streams, pk, consts, nb)

        def merge(name, xres, tm, tiles_per_row, fixed_row):
            z, xs, bc, q, k, v, go, sm = streams[name]
            yf, yb, of, ob = mix[name]
            return _outproj(yf, yb, xs, z, of, ob, go, xres, mod_l, pk['dsk'], pk['sg'], pk['gg'],
                            pk['wout'], pk['n2g'], wr, tm=tm, tiles_per_row=tiles_per_row, fixed_row=fixed_row)

        xn_lat, h2_lat, lg_lat = merge('lat', x2, TM_LAT, lat_tiles, None)
        if last:
            h2, logits = h2_lat, lg_lat
        else:
            xn_ctx, h2_ctx, lg_ctx = merge('ctx', c2, ctx_len, 1, ctx_row)
            h2 = jnp.concatenate([h2_ctx, h2_lat], axis=1)
            logits = jnp.concatenate([lg_ctx, lg_lat], axis=0)
        row_idx, gcol, y = _moe(h2, logits, bias_col, su, w_gate, w_up, w_down, l)
        yg = _sc_gather_rows(y, row_idx)
        lat_off = 0
        if not last:
            c2 = _combine_rows(yg, 0, gcol, xn_ctx, mod_l, final_norm_g[None, :],
                               tiles_per_row=1, fixed_row=ctx_row, final=False)
            lat_off = xn_ctx.shape[0] // MOE_TILE
        x2 = _combine_rows(yg, lat_off, gcol, xn_lat, mod_l, final_norm_g[None, :],
                           tiles_per_row=seq // MOE_TILE, fixed_row=None, final=last)
    return x2.reshape(nb, seq, d)
```

```python
import functools

import jax
import jax.numpy as jnp
from jax import lax
from jax.experimental import pallas as pl
from jax.experimental.pallas import tpu as pltpu
from jax.experimental.pallas import tpu_sc as plsc

f32 = jnp.float32
bf16 = jnp.bfloat16
i32 = jnp.int32
u32 = jnp.uint32

D_MODEL = 1024
SSD_HEADS = 8
SSD_HEAD_DIM = 64
SSD_WIDTH = 512
SSD_GROUPS = 2
SSD_STATE = 64
SSD_XBC = 768
GLA_HEADS = 4
GLA_KEY_DIM = 64
GLA_VAL_DIM = 128
GLA_QK = 256
GLA_WIDTH = 512
GLA_GATE_RANK = 16
GLA_GATE_NORM = 16.0
GRID_W = 64
N_EXPERTS = 32
N_EXPERT_GROUPS = 8
EXPERTS_PER_GROUP = 4
TOP_K = 2
EXPERT_FF = 512
NORM_EPS = 1e-6

LANES = 128
CHUNK = 128
CHUNKS_PER_STEP = 8
INTERLEAVE_WAVE = 8
ROW_SPLIT = 2
SSD_SPLIT = 3
GLA_SPLIT = 2
MAIN_COLS = 2816
SMALL_COLS = LANES
TM_LAT = 512
ROUTE_TILE = 512
MOE_TILE = 512
SC_WINDOW = 128
SC_ROW_PARTS = 2
SC_PART_COLS = D_MODEL // 2 // SC_ROW_PARTS
PACKED_COLS = D_MODEL // 2
EXPERT_BLOCK = 512
VMEM_LIMIT = 48 * 1024 * 1024

_HI = lax.Precision.HIGHEST
_NT = (((1,), (1,)), ((), ()))


def _dot(a, b, precision=None):
    return jnp.dot(a, b, preferred_element_type=f32, precision=precision)


def _silu(x):
    return x * jax.nn.sigmoid(x)


def _plane_spec(rows, block_index):
    return pl.BlockSpec((SC_ROW_PARTS, rows, SC_PART_COLS), lambda *idx: (0, block_index(*idx), 0))


def _store_planes(ref, rows, packed):
    for part in range(SC_ROW_PARTS):
        ref[part, rows, :] = packed[:, part * SC_PART_COLS:(part + 1) * SC_PART_COLS]


def _load_planes(ref, rows):
    return jnp.concatenate([ref[part, rows, :] for part in range(SC_ROW_PARTS)], axis=1)


def _pack_rows(value):
    hi = pltpu.bitcast(value[:, :PACKED_COLS].astype(bf16).astype(f32), u32)
    lo = pltpu.bitcast(value[:, PACKED_COLS:].astype(bf16).astype(f32), u32)
    return hi | (lo >> 16)


def _unpack_rows(packed):
    hi = pltpu.bitcast(packed & jnp.uint32(0xFFFF0000), f32)
    lo = pltpu.bitcast(packed << 16, f32)
    return jnp.concatenate([hi, lo], axis=1)


def _cparams(sem):
    return pltpu.CompilerParams(dimension_semantics=sem, vmem_limit_bytes=VMEM_LIMIT)


def _mod_kernel(c_ref, w_ref, b_ref, o_ref):
    sc = _silu(c_ref[...]).astype(bf16)
    o_ref[...] = _dot(sc, w_ref[...].astype(bf16)) + b_ref[...]


def _modulation(cvec, w_mod, b_mod):
    depth, d, n = w_mod.shape
    tn = 1536
    return pl.pallas_call(
        _mod_kernel,
        out_shape=jax.ShapeDtypeStruct((depth, 16, n), f32),
        grid=(depth, n // tn),
        in_specs=[pl.BlockSpec((16, d), lambda l, j: (0, 0)),
                  pl.BlockSpec((None, d, tn), lambda l, j: (l, 0, j)),
                  pl.BlockSpec((None, 1, tn), lambda l, j: (l, 0, j))],
        out_specs=pl.BlockSpec((None, 16, tn), lambda l, j: (l, 0, j)),
        compiler_params=_cparams(("arbitrary", "arbitrary")),
        name="modulation",
    )(cvec, w_mod, b_mod.reshape(depth, 1, n))


def _inproj_kernel(x_ref, mod_ref, g_ref, w_ref, cw_ref, cb_ref,
                   z_ref, xs_ref, bc_ref, q_ref, k_ref, v_ref, go_ref, sm_ref, *, rowlen):
    tm = x_ref.shape[0]
    split = ROW_SPLIT if (tm // ROW_SPLIT) % rowlen == 0 else 1
    sub = tm // split
    m = mod_ref[...]
    geff = g_ref[...] * (1.0 + m[1:2])
    cw = cw_ref[...]

    def rows_gen(rows):
        x = x_ref[rows, :]
        inv = lax.rsqrt(jnp.mean(x * x, axis=-1, keepdims=True) + NORM_EPS)
        h = ((x * inv) * geff + m[0:1]).astype(bf16)
        yield

        def proj(lo, hi):
            return _dot(h, w_ref[:, lo:hi])

        z_ref[rows, :] = proj(0, 512).astype(bf16)
        yield

        xbc = proj(512, 1280)
        yield
        pos = lax.broadcasted_iota(i32, xbc.shape, 0) & (rowlen - 1)
        acc = xbc * cw[2:3]
        for d in (-2, -1, 1, 2):
            shifted = pltpu.roll(xbc, (-d) % sub, 0)
            valid = (pos >= -d) if d < 0 else (pos <= rowlen - 1 - d)
            acc = acc + jnp.where(valid, shifted, 0.0) * cw[2 + d:3 + d]
        yield
        act = _silu(acc + cb_ref[...])
        xs_ref[rows, :] = act[:, :SSD_WIDTH].astype(bf16)
        bc_ref[rows, :] = act[:, SSD_WIDTH:].astype(bf16)
        yield

        qk = proj(1280, 1792)
        q_ref[rows, :] = qk[:, :GLA_QK].astype(bf16)
        k_ref[rows, :] = qk[:, GLA_QK:].astype(bf16)
        yield
        v_ref[rows, :] = proj(1792, 2304).astype(bf16)
        yield
        go_ref[rows, :] = proj(2304, 2816).astype(bf16)
        yield
        sm_ref[rows, :] = proj(2816, 2944)

    _run_interleaved(rows_gen(pl.ds(r * sub, sub)) for r in range(split))


def _inproj(x2d, mod_l, norm_g, w_packed, conv_w8, conv_b, *, tm, rowlen, tiles_per_row, fixed_row):
    t = x2d.shape[0]
    nt = t // tm
    if fixed_row is None:
        mod_map = lambda i: (i // tiles_per_row, 0, 0)
    else:
        mod_map = lambda i: (fixed_row, 0, 0)
    tok = lambda w: pl.BlockSpec((tm, w), lambda i: (i, 0))
    const = lambda a: pl.BlockSpec(a.shape, lambda i: (0,) * a.ndim)
    widths = (512, 512, 256, 256, 256, 512, 512)
    return pl.pallas_call(
        functools.partial(_inproj_kernel, rowlen=rowlen),
        out_shape=tuple(jax.ShapeDtypeStruct((t, w), bf16) for w in widths)
        + (jax.ShapeDtypeStruct((t, SMALL_COLS), f32),),
        grid=(nt,),
        in_specs=[tok(D_MODEL), pl.BlockSpec((None, 6, D_MODEL), mod_map), const(norm_g),
                  const(w_packed), const(conv_w8), const(conv_b)],
        out_specs=tuple(tok(w) for w in widths) + (tok(SMALL_COLS),),
        compiler_params=_cparams(("arbitrary",)),
        name="inproj",
    )(x2d, mod_l, norm_g, w_packed, conv_w8, conv_b)


def _dot_split(a, b, parts, *, split_lhs):
    rest = a if split_lhs else b
    acc = None
    for _ in range(parts):
        piece = rest.astype(bf16)
        rest = rest - piece.astype(f32)
        term = _dot(piece, b) if split_lhs else _dot(a, piece)
        acc = term if acc is None else acc + term
    return acc


def _tri_masks(q):
    r = lax.broadcasted_iota(i32, (q, q), 0)
    c = lax.broadcasted_iota(i32, (q, q), 1)
    return r >= c, r <= c


def _ssd_chunk(load, store, carry, par, e, bm, rev):
    xb, bcv, sm = load()
    dtb, a_lane, dtbt, a_sub = par
    q = xb.shape[0]
    dt_col = jax.nn.softplus(sm + dtb)
    a_col = dt_col * a_lane
    dt_row = jax.nn.softplus(sm.T[0:SSD_HEADS, :] + dtbt)
    a_row = dt_row * a_sub
    yield
    lower, upper = _tri_masks(q)
    lo_b = jnp.where(lower, 1.0, 0.0).astype(bf16)
    up_b = jnp.where(upper, 1.0, 0.0).astype(bf16)
    if not rev:
        cs_col = _dot_split(lo_b, a_col, SSD_SPLIT, split_lhs=False)
        cs_row = _dot_split(a_row, up_b, SSD_SPLIT, split_lhs=True)
        mask = lower
        a_tot = cs_col[q - 1:q, :]
    else:
        cs_col = _dot_split(up_b, a_col, SSD_SPLIT, split_lhs=False)
        cs_row = _dot_split(a_row, lo_b, SSD_SPLIT, split_lhs=True)
        mask = upper
        a_tot = cs_col[0:1, :]
    row_term = cs_row - jnp.log(dt_row)
    yield

    w_exp = _dot((jnp.exp(a_tot - cs_col) * dt_col).astype(bf16), e)
    ecs_exp = _dot(jnp.exp(cs_col).astype(bf16), e)
    dec_exp = _dot_split(jnp.broadcast_to(jnp.exp(a_tot), (8, LANES)), e, SSD_SPLIT, split_lhs=True)[0:1]
    yield

    xw = (xb.astype(f32) * w_exp).astype(bf16)
    b_all = bcv[:, 0:LANES]
    c_all = bcv[:, LANES:2 * LANES]
    b_t = b_all.astype(f32).T.astype(bf16)
    yield
    s_upd = _dot(b_t, xw)
    yield

    lane = lax.broadcasted_iota(i32, (q, LANES), 1)
    zero_b = jnp.zeros((q, LANES), bf16)
    ys = []
    for g in range(SSD_GROUPS):
        in_g = (lane >= SSD_STATE * g) & (lane < SSD_STATE * (g + 1))
        cb = _dot(jnp.where(in_g, c_all, zero_b), b_t)
        yield
        for pp in range(2):
            h0 = 4 * g + 2 * pp
            ms = []
            for h in (h0, h0 + 1):
                seg = cs_col[:, h:h + 1] - row_term[h:h + 1, :]
                ms.append((cb * jnp.exp(jnp.where(mask, seg, -1e30))).astype(bf16))
            xp = xb[:, h0 * SSD_HEAD_DIM:h0 * SSD_HEAD_DIM + LANES]
            rhs = jnp.concatenate([jnp.where(lane < SSD_HEAD_DIM, xp, zero_b),
                                   jnp.where(lane >= SSD_HEAD_DIM, xp, zero_b)], axis=0)
            yield
            ys.append(_dot(jnp.concatenate(ms, axis=1), rhs))
            yield
    y_intra = jnp.concatenate(ys, axis=1)
    yield

    s_old = carry['s']
    carry['s'] = (s_old * dec_exp + s_upd) * bm
    store((y_intra + _dot(c_all, s_old.astype(bf16)) * ecs_exp).astype(bf16))


def _run_interleaved(gens):
    gens = list(gens)
    done = object()
    for w in range(0, len(gens), INTERLEAVE_WAVE):
        live = gens[w:w + INTERLEAVE_WAVE]
        while live:
            live = [g for g in live if next(g, done) is not done]


def _chunk_rows(cps):
    return [(pl.ds(ci * CHUNK, CHUNK), pl.ds((cps - 1 - ci) * CHUNK, CHUNK)) for ci in range(cps)]


def _ssd_kernel(xf_ref, bcf_ref, smf_ref, xr_ref, bcr_ref, smr_ref, h0f_ref, h0r_ref,
                dtbf_ref, af_ref, dtbtf_ref, atf_ref, dtbr_ref, ar_ref, dtbtr_ref, atr_ref, e_ref, bm_ref,
                yf_ref, yr_ref, hff_ref, hfr_ref, st_f, st_r, *, cps, nblk):
    j = pl.program_id(1)

    @pl.when(j == 0)
    def _():
        st_f[...] = h0f_ref[...]
        st_r[...] = h0r_ref[...]

    e = e_ref[...]
    bm = bm_ref[...]
    par_f = (dtbf_ref[...], af_ref[...], dtbtf_ref[...], atf_ref[...])
    par_r = (dtbr_ref[...], ar_ref[...], dtbtr_ref[...], atr_ref[...])
    carry_f = {'s': st_f[...]}
    carry_r = {'s': st_r[...]}

    def chunk(refs, out_ref, rows, carry, par, rev):
        def store(y):
            out_ref[rows, :] = y
        return _ssd_chunk(lambda: tuple(r[rows, :] for r in refs), store, carry, par, e, bm, rev)

    gens = []
    for rows_f, rows_r in _chunk_rows(cps):
        gens.append(chunk((xf_ref, bcf_ref, smf_ref), yf_ref, rows_f, carry_f, par_f, False))
        gens.append(chunk((xr_ref, bcr_ref, smr_ref), yr_ref, rows_r, carry_r, par_r, True))
    _run_interleaved(gens)
    st_f[...] = carry_f['s']
    st_r[...] = carry_r['s']

    @pl.when(j == nblk - 1)
    def _():
        hff_ref[...] = carry_f['s']
        hfr_ref[...] = carry_r['s']


def _scan_specs(t, nb):
    per_row = t // nb
    cps = min(CHUNKS_PER_STEP, per_row // CHUNK)
    nblk = per_row // (cps * CHUNK)
    fmap = lambda b, j: (b * nblk + j, 0)
    rmap = lambda b, j: (b * nblk + nblk - 1 - j, 0)
    return cps, nblk, fmap, rmap


def _ssd_scan(xs, bc, sm, h0_f, h0_r, par_f, par_r, e_mat, bmask, *, nb):
    t = xs.shape[0]
    cps, nblk, fmap, rmap = _scan_specs(t, nb)
    blk = cps * CHUNK
    const = lambda a: pl.BlockSpec(a.shape, lambda b, j: (0,) * a.ndim)
    st = pl.BlockSpec((None, LANES, SSD_WIDTH), lambda b, j: (b, 0, 0))
    widths = (SSD_WIDTH, 2 * LANES, SMALL_COLS)
    tok_in = [pl.BlockSpec((blk, w), m) for m in (fmap, rmap) for w in widths]
    consts = (*par_f, *par_r, e_mat, bmask)
    return pl.pallas_call(
        functools.partial(_ssd_kernel, cps=cps, nblk=nblk),
        out_shape=(jax.ShapeDtypeStruct((t, SSD_WIDTH), bf16), jax.ShapeDtypeStruct((t, SSD_WIDTH), bf16),
                   jax.ShapeDtypeStruct((nb, LANES, SSD_WIDTH), f32),
                   jax.ShapeDtypeStruct((nb, LANES, SSD_WIDTH), f32)),
        grid=(nb, nblk),
        in_specs=tok_in + [st, st] + [const(a) for a in consts],
        out_specs=(pl.BlockSpec((blk, SSD_WIDTH), fmap), pl.BlockSpec((blk, SSD_WIDTH), rmap), st, st),
        scratch_shapes=[pltpu.VMEM((LANES, SSD_WIDTH), f32), pltpu.VMEM((LANES, SSD_WIDTH), f32)],
        compiler_params=_cparams(("arbitrary", "arbitrary")),
        name="ssd",
    )(xs, bc, sm, xs, bc, sm, h0_f, h0_r, *consts)


def _gla_chunk(load, store, carry, par, bm, rev):
    qb, kb, vb, sm = load()
    gup, gbias = par
    q = qb.shape[0]
    gp = _dot(sm.astype(bf16), gup) + gbias
    yield
    g = jax.nn.log_sigmoid(gp) * (1.0 / GLA_GATE_NORM)
    yield
    lower, upper = _tri_masks(q)
    if not rev:
        b = _dot_split(jnp.where(lower, 1.0, 0.0).astype(bf16), g, GLA_SPLIT, split_lhs=False)
        mask, mid = lower, q // 2 - 1
        b_tot = b[q - 1:q, :]
    else:
        b = _dot_split(jnp.where(upper, 1.0, 0.0).astype(bf16), g, GLA_SPLIT, split_lhs=False)
        mask, mid = upper, q // 2
        b_tot = b[0:1, :]
    b_mid = b[mid:mid + 1, :]
    yield

    qf = qb.astype(f32) * (GLA_KEY_DIM ** -0.5)
    kf = kb.astype(f32)
    qd = (qf * jnp.exp(b - b_mid)).astype(bf16)
    yield
    ki = (kf * jnp.exp(b_mid - b)).astype(bf16)
    yield
    q_st = (qf * jnp.exp(b)).astype(bf16)
    yield
    k_end = (kf * jnp.exp(b_tot - b)).astype(bf16)
    dec = jnp.exp(b_tot)
    yield
    v_t = vb.astype(f32).T.astype(bf16)
    yield
    s_upd = _dot(v_t, k_end)
    yield

    lane = lax.broadcasted_iota(i32, (q, LANES), 1)
    zero_b = jnp.zeros((q, LANES), bf16)
    outs = []
    for h in range(GLA_HEADS):
        p, hh = divmod(h, 2)
        qp = qd[:, p * LANES:(p + 1) * LANES]
        kp = ki[:, p * LANES:(p + 1) * LANES]
        in_h = (lane >= GLA_KEY_DIM * hh) & (lane < GLA_KEY_DIM * (hh + 1))
        s = lax.dot_general(jnp.where(in_h, qp, zero_b), kp, _NT, preferred_element_type=f32)
        yield
        attn = jnp.where(mask, s, 0.0).astype(bf16)
        outs.append(_dot(attn, vb[:, h * GLA_VAL_DIM:(h + 1) * GLA_VAL_DIM]))
        yield
    o_intra = jnp.concatenate(outs, axis=1)
    yield

    s_old = carry['s']
    carry['s'] = (s_old * dec + s_upd) * bm
    o_inter = lax.dot_general(q_st, s_old.astype(bf16), _NT, preferred_element_type=f32)
    store((o_intra + o_inter).astype(bf16))


def _gla_kernel(qf_ref, kf_ref, vf_ref, smf_ref, qr_ref, kr_ref, vr_ref, smr_ref, s0f_ref, s0r_ref,
                gupf_ref, gbf_ref, gupr_ref, gbr_ref, bm_ref,
                of_ref, or_ref, sff_ref, sfr_ref, st_f, st_r, *, cps, nblk):
    j = pl.program_id(1)

    @pl.when(j == 0)
    def _():
        st_f[...] = s0f_ref[...]
        st_r[...] = s0r_ref[...]

    bm = bm_ref[...]
    par_f = (gupf_ref[...], gbf_ref[...])
    par_r = (gupr_ref[...], gbr_ref[...])
    carry_f = {'s': st_f[...]}
    carry_r = {'s': st_r[...]}

    def chunk(refs, out_ref, rows, carry, par, rev):
        def store(o):
            out_ref[rows, :] = o
        return _gla_chunk(lambda: tuple(r[rows, :] for r in refs), store, carry, par, bm, rev)

    gens = []
    for rows_f, rows_r in _chunk_rows(cps):
        gens.append(chunk((qf_ref, kf_ref, vf_ref, smf_ref), of_ref, rows_f, carry_f, par_f, False))
        gens.append(chunk((qr_ref, kr_ref, vr_ref, smr_ref), or_ref, rows_r, carry_r, par_r, True))
    _run_interleaved(gens)
    st_f[...] = carry_f['s']
    st_r[...] = carry_r['s']

    @pl.when(j == nblk - 1)
    def _():
        sff_ref[...] = carry_f['s']
        sfr_ref[...] = carry_r['s']


def _gla_scan(qa, ka, va, sm, s0_f, s0_r, par_f, par_r, bmask, *, nb):
    t = qa.shape[0]
    cps, nblk, fmap, rmap = _scan_specs(t, nb)
    blk = cps * CHUNK
    const = lambda a: pl.BlockSpec(a.shape, lambda b, j: (0,) * a.ndim)
    st = pl.BlockSpec((None, GLA_WIDTH, GLA_QK), lambda b, j: (b, 0, 0))
    widths = (GLA_QK, GLA_QK, GLA_WIDTH, SMALL_COLS)
    tok_in = [pl.BlockSpec((blk, w), m) for m in (fmap, rmap) for w in widths]
    consts = (*par_f, *par_r, bmask)
    return pl.pallas_call(
        functools.partial(_gla_kernel, cps=cps, nblk=nblk),
        out_shape=(jax.ShapeDtypeStruct((t, GLA_WIDTH), bf16), jax.ShapeDtypeStruct((t, GLA_WIDTH), bf16),
                   jax.ShapeDtypeStruct((nb, GLA_WIDTH, GLA_QK), f32),
                   jax.ShapeDtypeStruct((nb, GLA_WIDTH, GLA_QK), f32)),
        grid=(nb, nblk),
        in_specs=tok_in + [st, st] + [const(a) for a in consts],
        out_specs=(pl.BlockSpec((blk, GLA_WIDTH), fmap), pl.BlockSpec((blk, GLA_WIDTH), rmap), st, st),
        scratch_shapes=[pltpu.VMEM((GLA_WIDTH, GLA_QK), f32), pltpu.VMEM((GLA_WIDTH, GLA_QK), f32)],
        compiler_params=_cparams(("arbitrary", "arbitrary")),
        name="gla",
    )(qa, ka, va, sm, qa, ka, va, sm, s0_f, s0_r, *consts)


def _outproj_kernel(yf_ref, yb_ref, xs_ref, z_ref, of_ref, ob_ref, go_ref, xres_ref, mod_ref,
                    dsk_ref, sg_ref, gg_ref, wout_ref, n2g_ref, wr_ref,
                    xnew_ref, h2_ref, lg_ref):
    tm = xres_ref.shape[0]
    sub = tm // ROW_SPLIT
    m = mod_ref[...]
    h2_gain = n2g_ref[...] * (1.0 + m[4:5])

    def rows_gen(rows):
        up = lambda r: r[rows, :].astype(f32)
        y = up(yf_ref) + up(yb_ref) + dsk_ref[...] * up(xs_ref)
        yield
        y = y * _silu(up(z_ref))
        yield
        y = y * lax.rsqrt(jnp.mean(y * y, axis=-1, keepdims=True) + NORM_EPS) * sg_ref[...]
        yield
        o = up(of_ref) + up(ob_ref)
        parts = []
        for h in range(GLA_HEADS):
            oh = o[:, h * GLA_VAL_DIM:(h + 1) * GLA_VAL_DIM]
            parts.append(oh * lax.rsqrt(jnp.mean(oh * oh, axis=-1, keepdims=True) + NORM_EPS))
        yield
        o = jnp.concatenate(parts, axis=1) * gg_ref[...] * _silu(up(go_ref))
        yield
        cat = jnp.concatenate([y, o], axis=1).astype(bf16)
        mix = _dot(cat, wout_ref[...])
        yield
        xn = xres_ref[rows, :] + m[2:3] * mix
        xnew_ref[rows, :] = xn
        yield
        inv = lax.rsqrt(jnp.mean(xn * xn, axis=-1, keepdims=True) + NORM_EPS)
        h2 = (xn * inv) * h2_gain + m[3:4]
        _store_planes(h2_ref, rows, _pack_rows(h2))
        yield
        h2_hi = h2.astype(bf16)
        h2_lo = (h2 - h2_hi.astype(f32)).astype(bf16)
        wr_hi = wr_ref[0]
        lg_ref[rows, :] = _dot(h2_hi, wr_hi) + _dot(h2_hi, wr_ref[1]) + _dot(h2_lo, wr_hi)

    _run_interleaved(rows_gen(pl.ds(r * sub, sub)) for r in range(ROW_SPLIT))


def _outproj(yf, yb, xs, z, of, ob, go, xres, mod_l, dsk, sg, gg, wout, n2g, wr,
             *, tm, tiles_per_row, fixed_row):
    t = xres.shape[0]
    if fixed_row is None:
        mod_map = lambda i: (i // tiles_per_row, 0, 0)
    else:
        mod_map = lambda i: (fixed_row, 0, 0)
    tok = lambda w: pl.BlockSpec((tm, w), lambda i: (i, 0))
    const = lambda a: pl.BlockSpec(a.shape, lambda i: (0,) * a.ndim)
    return pl.pallas_call(
        _outproj_kernel,
        out_shape=(jax.ShapeDtypeStruct((t, D_MODEL), f32), jax.ShapeDtypeStruct((SC_ROW_PARTS, t, SC_PART_COLS), u32),
                   jax.ShapeDtypeStruct((t, LANES), f32)),
        grid=(t // tm,),
        in_specs=[tok(512)] * 7 + [tok(D_MODEL), pl.BlockSpec((None, 6, D_MODEL), mod_map),
                                   const(dsk), const(sg), const(gg), const(wout), const(n2g), const(wr)],
        out_specs=(tok(D_MODEL), _plane_spec(tm, lambda i: i), tok(LANES)),
        compiler_params=_cparams(("arbitrary",)),
        name="outproj",
    )(yf, yb, xs, z, of, ob, go, xres, mod_l, dsk, sg, gg, wout, n2g, wr)


def _route_kernel(lg_ref, bias_ref, su_ref, ids_ref, rank_ref, gcol_ref, cnt_ref, carry):
    i = pl.program_id(0)
    tm = lg_ref.shape[0]

    @pl.when(i == 0)
    def _():
        carry[...] = jnp.zeros_like(carry)

    s = jax.nn.sigmoid(lg_ref[...].T[0:N_EXPERTS, :])
    sel = s + bias_ref[...]
    a = [sel[8 * m:8 * (m + 1)] for m in range(EXPERTS_PER_GROUP)]
    sv = [s[8 * m:8 * (m + 1)] for m in range(EXPERTS_PER_GROUP)]
    hi01, lo01 = jnp.maximum(a[0], a[1]), jnp.minimum(a[0], a[1])
    hi23, lo23 = jnp.maximum(a[2], a[3]), jnp.minimum(a[2], a[3])
    gscore = jnp.maximum(hi01, hi23) + jnp.maximum(jnp.minimum(hi01, hi23), jnp.maximum(lo01, lo23))
    giota = lax.broadcasted_iota(i32, gscore.shape, 0)
    gmax = jnp.max(gscore, axis=0, keepdims=True)
    gidx = jnp.min(jnp.where(gscore == gmax, giota, N_EXPERT_GROUPS), axis=0, keepdims=True)
    pick = giota == gidx
    v = [jnp.sum(jnp.where(pick, a[m], 0.0), axis=0, keepdims=True) for m in range(4)]
    w = [jnp.sum(jnp.where(pick, sv[m], 0.0), axis=0, keepdims=True) for m in range(4)]

    def first_max(vals, excluded):
        best = vals[0]
        for m in range(1, 4):
            best = jnp.maximum(best, vals[m])
        idx = jnp.full(best.shape, 3, i32)
        for m in (2, 1, 0):
            hit = vals[m] == best
            if excluded is not None:
                hit = hit & (excluded != m)
            idx = jnp.where(hit, m, idx)
        return idx

    i1 = first_max(v, None)
    v_rest = [jnp.where(i1 == m, -jnp.inf, v[m]) for m in range(4)]
    i2 = first_max(v_rest, i1)

    def take(vals, idx):
        out = vals[3]
        for m in (2, 1, 0):
            out = jnp.where(idx == m, vals[m], out)
        return out

    w1, w2 = take(w, i1), take(w, i2)
    denom = w1 + w2
    id1 = gidx * EXPERTS_PER_GROUP + i1
    id2 = gidx * EXPERTS_PER_GROUP + i2
    row2 = lax.broadcasted_iota(i32, (TOP_K, tm), 0)
    ids_ref[...] = jnp.where(row2 == 0, id1, id2)

    eiota = lax.broadcasted_iota(i32, (N_EXPERTS, tm), 0)
    hit1, hit2 = eiota == id1, eiota == id2
    onehot = jnp.where(hit1, 1.0, 0.0) + jnp.where(hit2, 1.0, 0.0)
    before = _dot(onehot.astype(bf16), su_ref[...]) + carry[...]
    r1 = jnp.sum(jnp.where(hit1, before, 0.0), axis=0, keepdims=True)
    r2 = jnp.sum(jnp.where(hit2, before, 0.0), axis=0, keepdims=True)
    rank_ref[...] = jnp.where(row2 == 0, r1, r2).astype(i32)
    new_carry = carry[...] + jnp.sum(onehot, axis=1, keepdims=True)
    carry[...] = new_carry
    cnt_ref[...] = jnp.broadcast_to(new_carry, cnt_ref.shape)

    rows = lax.broadcasted_iota(i32, (LANES, tm), 0)
    gates = jnp.where(rows == 0, w1 / denom, jnp.where(rows == 1, w2 / denom, 0.0))
    gcol_ref[...] = gates.T


def _route(logits, bias_col, su):
    t = logits.shape[0]
    tm = ROUTE_TILE
    return pl.pallas_call(
        _route_kernel,
        out_shape=(jax.ShapeDtypeStruct((2, t), i32), jax.ShapeDtypeStruct((2, t), i32),
                   jax.ShapeDtypeStruct((t, LANES), f32), jax.ShapeDtypeStruct((N_EXPERTS, LANES), f32)),
        grid=(t // tm,),
        in_specs=[pl.BlockSpec((tm, LANES), lambda i: (i, 0)),
                  pl.BlockSpec((N_EXPERTS, 1), lambda i: (0, 0)),
                  pl.BlockSpec((tm, tm), lambda i: (0, 0))],
        out_specs=(pl.BlockSpec((2, tm), lambda i: (0, i)), pl.BlockSpec((2, tm), lambda i: (0, i)),
                   pl.BlockSpec((tm, LANES), lambda i: (i, 0)),
                   pl.BlockSpec((N_EXPERTS, LANES), lambda i: (0, 0))),
        scratch_shapes=[pltpu.VMEM((N_EXPERTS, 1), f32)],
        compiler_params=_cparams(("arbitrary",)),
        name="route",
    )(logits, bias_col, su)


def _dest_kernel(ids_ref, rank_ref, pstart_ref, idx_ref, *, p):
    ids = ids_ref[...]
    tm = ids.shape[1]
    eiota = lax.broadcasted_iota(i32, (N_EXPERTS, tm), 0)
    ps = pstart_ref[...]
    rank = rank_ref[...]
    row = lax.broadcasted_iota(i32, (TOP_K * SC_ROW_PARTS, tm), 0)
    out = jnp.zeros((TOP_K * SC_ROW_PARTS, tm), i32)
    for k in range(TOP_K):
        start = jnp.sum(jnp.where(eiota == ids[k:k + 1, :], ps, 0.0), axis=0, keepdims=True).astype(i32)
        dest = start + rank[k:k + 1, :]
        for part in range(SC_ROW_PARTS):
            out = jnp.where(row == k * SC_ROW_PARTS + part, dest + part * p, out)
    idx_ref[...] = out


def _dest(ids, rank, pstart_col, p):
    t = ids.shape[1]
    tm = MOE_TILE
    return pl.pallas_call(
        functools.partial(_dest_kernel, p=p),
        out_shape=jax.ShapeDtypeStruct((TOP_K * SC_ROW_PARTS, t), i32),
        grid=(t // tm,),
        in_specs=[pl.BlockSpec((TOP_K, tm), lambda i: (0, i)), pl.BlockSpec((TOP_K, tm), lambda i: (0, i)),
                  pl.BlockSpec((N_EXPERTS, 1), lambda i: (0, 0))],
        out_specs=pl.BlockSpec((TOP_K * SC_ROW_PARTS, tm), lambda i: (0, i)),
        compiler_params=_cparams(("arbitrary",)),
        name="dest",
    )(ids, rank, pstart_col).reshape(1, -1)


def _expert_kernel(be_ref, nu_ref, nv_ref, x_ref, wg_ref, wu_ref, wd_ref, y_ref, wg_b, wu_b, wd_b):
    i = pl.program_id(0)
    fresh = jnp.logical_or(i == 0, be_ref[i] != be_ref[jnp.maximum(i - 1, 0)])

    @pl.when(jnp.logical_and(fresh, i < nu_ref[0]))
    def _():
        wg_b[...] = wg_ref[...].astype(bf16)
        wu_b[...] = wu_ref[...].astype(bf16)
        wd_b[...] = wd_ref[...].astype(bf16)

    @pl.when(i < nu_ref[0])
    def _():
        sub = x_ref.shape[1] // ROW_SPLIT
        n_valid = nv_ref[i]

        def rows_gen(r):
            rows = pl.ds(r * sub, sub)
            packed_x = _load_planes(x_ref, rows)
            row = lax.broadcasted_iota(i32, packed_x.shape, 0) + r * sub
            xb = _unpack_rows(jnp.where(row < n_valid, packed_x, jnp.uint32(0))).astype(bf16)
            yield
            g = _dot(xb, wg_b[...])
            yield
            u = _dot(xb, wu_b[...])
            yield
            hmid = (_silu(g) * u).astype(bf16)
            yield
            _store_planes(y_ref, rows, _pack_rows(_dot(hmid, wd_b[...])))

        _run_interleaved(rows_gen(r) for r in range(ROW_SPLIT))

    @pl.when(i >= nu_ref[0])
    def _():
        y_ref[...] = jnp.zeros_like(y_ref)


def _experts(block_e, n_used, block_valid, buf, w_gate, w_up, w_down, layer):
    p = buf.shape[1]
    nb = p // EXPERT_BLOCK
    d, ff = w_gate.shape[-2:]
    row_blocks = _plane_spec(EXPERT_BLOCK, lambda i, be, nu, nv: i)
    weights = lambda a, b: pl.BlockSpec((None, None, a, b), lambda i, be, nu, nv: (layer, be[i], 0, 0))
    return pl.pallas_call(
        _expert_kernel,
        out_shape=jax.ShapeDtypeStruct(buf.shape, u32),
        grid_spec=pltpu.PrefetchScalarGridSpec(
            num_scalar_prefetch=3,
            grid=(nb,),
            in_specs=[row_blocks, weights(d, ff), weights(d, ff), weights(ff, d)],
            out_specs=row_blocks,
            scratch_shapes=[pltpu.VMEM((d, ff), bf16), pltpu.VMEM((d, ff), bf16), pltpu.VMEM((ff, d), bf16)],
        ),
        compiler_params=_cparams(("arbitrary",)),
        name="experts",
    )(block_e, n_used, block_valid, buf, w_gate, w_up, w_down)


def _sc_gather_rows(planes, idx):
    parts, p, cols = planes.shape
    n = idx.shape[1]
    t = n // (TOP_K * parts)
    halves = planes.reshape(parts * p, cols)
    mesh = plsc.VectorSubcoreMesh(core_axis_name="core", subcore_axis_name="subcore")

    @pl.kernel(out_type=jax.ShapeDtypeStruct((n, cols), planes.dtype), mesh=mesh, scratch_types=[])
    def gather(x_hbm, i_hbm, o_hbm):
        def body(i_vmem, o_vmem):
            pltpu.sync_copy(x_hbm.at[i_vmem.at[0]], o_vmem)

        pltpu.emit_pipeline(
            body,
            grid=(n // SC_WINDOW,),
            in_specs=[pl.BlockSpec((1, SC_WINDOW), lambda i: (0, i))],
            out_specs=[pl.BlockSpec((SC_WINDOW, cols), lambda i: (i, 0))],
            core_axis_name=("core", "subcore"),
            dimension_semantics=(pltpu.PARALLEL,),
        )(i_hbm, o_hbm)

    return gather(halves, idx).reshape(TOP_K, parts, t, cols)


def _combine_rows_kernel(*refs, final):
    part_refs = refs[:TOP_K * SC_ROW_PARTS]
    gcol_ref, x_ref, mod_ref, fg_ref, o_ref = refs[TOP_K * SC_ROW_PARTS:]
    gc = gcol_ref[...]
    rows = [_unpack_rows(jnp.concatenate([r[...] for r in part_refs[k * SC_ROW_PARTS:(k + 1) * SC_ROW_PARTS]], axis=1))
            for k in range(TOP_K)]
    ffn = gc[:, 0:1] * rows[0] + gc[:, 1:2] * rows[1]
    x = x_ref[...] + mod_ref[...][5:6] * ffn
    if final:
        x = x * lax.rsqrt(jnp.mean(x * x, axis=-1, keepdims=True) + NORM_EPS) * fg_ref[...]
    o_ref[...] = x


def _combine_rows(yg, tile_off, gcol, xnew, mod_l, final_g, *, tiles_per_row, fixed_row, final):
    tm = MOE_TILE
    t = xnew.shape[0]
    if fixed_row is None:
        mod_map = lambda i: (i // tiles_per_row, 0, 0)
    else:
        mod_map = lambda i: (fixed_row, 0, 0)
    piece = lambda k, part: pl.BlockSpec((None, None, tm, SC_PART_COLS), lambda i: (k, part, i + tile_off, 0))
    pieces = [piece(k, part) for k in range(TOP_K) for part in range(SC_ROW_PARTS)]
    return pl.pallas_call(
        functools.partial(_combine_rows_kernel, final=final),
        out_shape=jax.ShapeDtypeStruct((t, D_MODEL), f32),
        grid=(t // tm,),
        in_specs=pieces + [
                  pl.BlockSpec((tm, LANES), lambda i: (i + tile_off, 0)),
                  pl.BlockSpec((tm, D_MODEL), lambda i: (i, 0)),
                  pl.BlockSpec((None, 6, D_MODEL), mod_map),
                  pl.BlockSpec((1, D_MODEL), lambda i: (0, 0))],
        out_specs=pl.BlockSpec((tm, D_MODEL), lambda i: (i, 0)),
        compiler_params=_cparams(("arbitrary",)),
        name="combine_rows",
    )(*([yg] * len(pieces)), gcol, xnew, mod_l, final_g)


def _sc_scatter_rows(planes, idx, p):
    parts, t, cols = planes.shape
    n = idx.shape[1]
    src = planes.reshape(parts * t, cols)
    src_windows = src.shape[0] // SC_WINDOW
    mesh = plsc.VectorSubcoreMesh(core_axis_name="core", subcore_axis_name="subcore")

    @pl.kernel(out_type=jax.ShapeDtypeStruct((parts * p, cols), planes.dtype), mesh=mesh, scratch_types=[])
    def scatter(x_hbm, i_hbm, o_hbm):
        def body(x_vmem, i_vmem):
            pltpu.sync_copy(x_vmem, o_hbm.at[i_vmem.at[0]])

        pltpu.emit_pipeline(
            body,
            grid=(n // SC_WINDOW,),
            in_specs=[pl.BlockSpec((SC_WINDOW, cols), lambda i: (i % src_windows, 0)),
                      pl.BlockSpec((1, SC_WINDOW), lambda i: (0, i))],
            out_specs=[],
            core_axis_name=("core", "subcore"),
            dimension_semantics=(pltpu.PARALLEL,),
        )(x_hbm, i_hbm)

    return scatter(src, idx).reshape(parts, p, cols)


def _moe(h2_planes, logits, router_bias_col, su, w_gate, w_up, w_down, layer):
    t = logits.shape[0]
    ids, rank, gcol, cnt = _route(logits, router_bias_col, su)
    counts = cnt[:, 0].astype(i32)
    padded = (counts + EXPERT_BLOCK - 1) // EXPERT_BLOCK * EXPERT_BLOCK
    pend = jnp.cumsum(padded)
    pstart = pend - padded
    nb = (t * TOP_K) // EXPERT_BLOCK + N_EXPERTS
    block_pos = jnp.arange(nb, dtype=i32) * EXPERT_BLOCK
    block_e = jnp.minimum(jnp.sum((pend[None, :] <= block_pos[:, None]).astype(i32), axis=1), N_EXPERTS - 1)
    block_valid = jnp.clip(pstart[block_e] + counts[block_e] - block_pos, 0, EXPERT_BLOCK).astype(i32)
    n_used = (pend[-1:] // EXPERT_BLOCK).astype(i32)
    p = nb * EXPERT_BLOCK
    idx = _dest(ids, rank, pstart.astype(f32)[:, None], p)
    buf = _sc_scatter_rows(h2_planes, idx, p)
    y = _experts(block_e, n_used, block_valid, buf, w_gate, w_up, w_down, layer)
    return idx, gcol, y


def _pack_layer(l, w_in, conv_w, conv_b, dt_bias_f, dt_bias_b, a_log_f, a_log_b, d_skip, ssd_norm_g,
                gk_up_f, gk_bias_f, gk_up_b, gk_bias_b, gla_norm_g, w_out, norm1_g, norm2_g):
    w = w_in[l]
    z, xbc, dt, q, k, v, go, gk = jnp.split(w, [512, 1280, 1288, 1544, 1800, 2312, 2824], axis=1)
    small = jnp.concatenate([dt, gk, jnp.zeros((D_MODEL, SMALL_COLS - 24), f32)], axis=1)
    w_packed = jnp.concatenate([z, xbc, q, k, v, go, small], axis=1).astype(bf16)

    def lane_row(vec):
        return jnp.zeros((1, LANES), f32).at[0, :SSD_HEADS].set(vec)

    def sub_col(vec):
        return jnp.broadcast_to(vec[:, None], (SSD_HEADS, CHUNK)).astype(f32)

    def gup(m):
        return jnp.zeros((SMALL_COLS, GLA_QK), f32).at[SSD_HEADS:SSD_HEADS + GLA_GATE_RANK].set(m).astype(bf16)

    a_f = -jnp.exp(a_log_f[l])
    a_b = -jnp.exp(a_log_b[l])
    return dict(
        w_packed=w_packed,
        conv_w8=jnp.zeros((8, SSD_XBC), f32).at[:5].set(conv_w[l]),
        conv_b=conv_b[l][None, :],
        ssd_f=(lane_row(dt_bias_f[l]), lane_row(a_f), sub_col(dt_bias_f[l]), sub_col(a_f)),
        ssd_b=(lane_row(dt_bias_b[l]), lane_row(a_b), sub_col(dt_bias_b[l]), sub_col(a_b)),
        gla_f=(gup(gk_up_f[l]), gk_bias_f[l][None, :]),
        gla_b=(gup(gk_up_b[l]), gk_bias_b[l][None, :]),
        dsk=jnp.repeat(d_skip[l], SSD_HEAD_DIM)[None, :],
        sg=ssd_norm_g[l][None, :],
        gg=jnp.tile(gla_norm_g[l], GLA_HEADS)[None, :],
        wout=w_out[l].astype(bf16),
        n1g=norm1_g[l][None, :],
        n2g=norm2_g[l][None, :],
    )


def _constants():
    r = jnp.arange(LANES)[:, None]
    c = jnp.arange(SSD_WIDTH)[None, :]
    e_mat = ((c // SSD_HEAD_DIM) == r).astype(bf16)
    ssd_mask = ((r // SSD_STATE) == (c // (SSD_WIDTH // SSD_GROUPS))).astype(f32)
    rr = jnp.arange(GLA_WIDTH)[:, None]
    cc = jnp.arange(GLA_QK)[None, :]
    gla_mask = ((rr // GLA_VAL_DIM) == (cc // GLA_KEY_DIM)).astype(f32)
    k = jnp.arange(ROUTE_TILE)
    su = (k[:, None] < k[None, :]).astype(bf16)
    return e_mat, ssd_mask, gla_mask, su


def _mixers(streams, pk, consts, nb):
    e_mat, ssd_mask, gla_mask, _ = consts
    out = {}
    h0_f = jnp.zeros((nb, LANES, SSD_WIDTH), f32)
    h0_b = h0_f
    s0_f = jnp.zeros((nb, GLA_WIDTH, GLA_QK), f32)
    s0_b = s0_f
    for name in ('ctx', 'lat'):
        z, xs, bc, q, k, v, go, sm = streams[name]
        yf, yb, h0_f, h0_b = _ssd_scan(xs, bc, sm, h0_f, h0_b, pk['ssd_f'], pk['ssd_b'], e_mat, ssd_mask, nb=nb)
        of, ob, s0_f, s0_b = _gla_scan(q, k, v, sm, s0_f, s0_b, pk['gla_f'], pk['gla_b'], gla_mask, nb=nb)
        out[name] = (yf, yb, of, ob)
    return out


def kernel(x, c, ctx, c_ctx, w_mod, b_mod, norm1_g, norm2_g, w_in, conv_w, conv_b, dt_bias_f, dt_bias_b, a_log_f, a_log_b, d_skip, ssd_norm_g, gk_up_f, gk_bias_f, gk_up_b, gk_bias_b, gla_norm_g, w_out, w_router, router_bias, w_gate, w_up, w_down, final_norm_g):
    nb, seq, d = x.shape
    ctx_len = ctx.shape[1]
    depth = w_mod.shape[0]
    consts = _constants()
    su = consts[3]

    cvec = jnp.zeros((16, d), f32).at[:nb].set(c).at[nb].set(c_ctx)
    mod = _modulation(cvec, w_mod, b_mod).reshape(depth, 16, 6, d)
    ctx_row = nb

    perm = jnp.array([g * EXPERTS_PER_GROUP + m for m in range(EXPERTS_PER_GROUP)
                      for g in range(N_EXPERT_GROUPS)], dtype=i32)
    wr32 = jnp.zeros((d, LANES), f32).at[:, :N_EXPERTS].set(w_router[:, perm])
    wr_hi = wr32.astype(bf16)
    wr = jnp.stack([wr_hi, (wr32 - wr_hi.astype(f32)).astype(bf16)])
    bias_col = router_bias[perm][:, None]

    x2 = x.reshape(nb * seq, d)
    c2 = ctx.reshape(nb * ctx_len, d)
    lat_tiles = seq // TM_LAT

    for l in range(depth):
        last = l == depth - 1
        pk = _pack_layer(l, w_in, conv_w, conv_b, dt_bias_f, dt_bias_b, a_log_f, a_log_b, d_skip, ssd_norm_g,
                         gk_up_f, gk_bias_f, gk_up_b, gk_bias_b, gla_norm_g, w_out, norm1_g, norm2_g)
        mod_l = mod[l]
        streams = {
            'ctx': _inproj(c2, mod_l, pk['n1g'], pk['w_packed'], pk['conv_w8'], pk['conv_b'],
                           tm=ctx_len, rowlen=ctx_len, tiles_per_row=1, fixed_row=ctx_row),
            'lat': _inproj(x2, mod_l, pk['n1g'], pk['w_packed'], pk['conv_w8'], pk['conv_b'],
                           tm=TM_LAT, rowlen=GRID_W, tiles_per_row=lat_tiles, fixed_row=None),
        }
        mix = _mixers(streams, pk, consts, nb)

        def merge(name, xres, tm, tiles_per_row, fixed_row):
            z, xs, bc, q, k, v, go, sm = streams[name]
            yf, yb, of, ob = mix[name]
            return _outproj(yf, yb, xs, z, of, ob, go, xres, mod_l, pk['dsk'], pk['sg'], pk['gg'],
                            pk['wout'], pk['n2g'], wr, tm=tm, tiles_per_row=tiles_per_row, fixed_row=fixed_row)

        xn_lat, h2_lat, lg_lat = merge('lat', x2, TM_LAT, lat_tiles, None)
        if last:
            h2, logits = h2_lat, lg_lat
        else:
            xn_ctx, h2_ctx, lg_ctx = merge('ctx', c2, ctx_len, 1, ctx_row)
            h2 = jnp.concatenate([h2_ctx, h2_lat], axis=1)
            logits = jnp.concatenate([lg_ctx, lg_lat], axis=0)
        row_idx, gcol, y = _moe(h2, logits, bias_col, su, w_gate, w_up, w_down, l)
        yg = _sc_gather_rows(y, row_idx)
        lat_off = 0
        if not last:
            c2 = _combine_rows(yg, 0, gcol, xn_ctx, mod_l, final_norm_g[None, :],
                               tiles_per_row=1, fixed_row=ctx_row, final=False)
            lat_off = xn_ctx.shape[0] // MOE_TILE
        x2 = _combine_rows(yg, lat_off, gcol, xn_lat, mod_l, final_norm_g[None, :],
                           tiles_per_row=seq // MOE_TILE, fixed_row=None, final=last)
    return x2.reshape(nb, seq, d)
```

```python
import functools

import jax
import jax.numpy as jnp
from jax import lax
from jax.experimental import pallas as pl
from jax.experimental.pallas import tpu as pltpu
from jax.experimental.pallas import tpu_sc as plsc

f32 = jnp.float32
bf16 = jnp.bfloat16
i32 = jnp.int32
u32 = jnp.uint32

D_MODEL = 1024
SSD_HEADS = 8
SSD_HEAD_DIM = 64
SSD_WIDTH = 512
SSD_GROUPS = 2
SSD_STATE = 64
SSD_XBC = 768
GLA_HEADS = 4
GLA_KEY_DIM = 64
GLA_VAL_DIM = 128
GLA_QK = 256
GLA_WIDTH = 512
GLA_GATE_RANK = 16
GLA_GATE_NORM = 16.0
GRID_W = 64
N_EXPERTS = 32
N_EXPERT_GROUPS = 8
EXPERTS_PER_GROUP = 4
TOP_K = 2
EXPERT_FF = 512
NORM_EPS = 1e-6

LANES = 128
CHUNK = 128
CHUNKS_PER_STEP = 8
INTERLEAVE_WAVE = 16
ROW_SPLIT = 2
SSD_SPLIT = 3
GLA_SPLIT = 2
MAIN_COLS = 2816
SMALL_COLS = LANES
TM_LAT = 512
ROUTE_TILE = 512
MOE_TILE = 512
SC_WINDOW = 128
SC_ROW_PARTS = 2
SC_PART_COLS = D_MODEL // 2 // SC_ROW_PARTS
PACKED_COLS = D_MODEL // 2
EXPERT_BLOCK = 512
VMEM_LIMIT = 48 * 1024 * 1024

_HI = lax.Precision.HIGHEST
_NT = (((1,), (1,)), ((), ()))


def _dot(a, b, precision=None):
    return jnp.dot(a, b, preferred_element_type=f32, precision=precision)


def _silu(x):
    return x * jax.nn.sigmoid(x)


def _plane_spec(rows, block_index):
    return pl.BlockSpec((SC_ROW_PARTS, rows, SC_PART_COLS), lambda *idx: (0, block_index(*idx), 0))


def _store_planes(ref, rows, packed):
    for part in range(SC_ROW_PARTS):
        ref[part, rows, :] = packed[:, part * SC_PART_COLS:(part + 1) * SC_PART_COLS]


def _load_planes(ref, rows):
    return jnp.concatenate([ref[part, rows, :] for part in range(SC_ROW_PARTS)], axis=1)


def _pack_rows(value):
    hi = pltpu.bitcast(value[:, :PACKED_COLS].astype(bf16).astype(f32), u32)
    lo = pltpu.bitcast(value[:, PACKED_COLS:].astype(bf16).astype(f32), u32)
    return hi | (lo >> 16)


def _unpack_rows(packed):
    hi = pltpu.bitcast(packed & jnp.uint32(0xFFFF0000), f32)
    lo = pltpu.bitcast(packed << 16, f32)
    return jnp.concatenate([hi, lo], axis=1)


def _cparams(sem):
    return pltpu.CompilerParams(dimension_semantics=sem, vmem_limit_bytes=VMEM_LIMIT)


def _mod_kernel(c_ref, w_ref, b_ref, o_ref):
    sc = _silu(c_ref[...]).astype(bf16)
    o_ref[...] = _dot(sc, w_ref[...].astype(bf16)) + b_ref[...]


def _modulation(cvec, w_mod, b_mod):
    depth, d, n = w_mod.shape
    tn = 1536
    return pl.pallas_call(
        _mod_kernel,
        out_shape=jax.ShapeDtypeStruct((depth, 16, n), f32),
        grid=(depth, n // tn),
        in_specs=[pl.BlockSpec((16, d), lambda l, j: (0, 0)),
                  pl.BlockSpec((None, d, tn), lambda l, j: (l, 0, j)),
                  pl.BlockSpec((None, 1, tn), lambda l, j: (l, 0, j))],
        out_specs=pl.BlockSpec((None, 16, tn), lambda l, j: (l, 0, j)),
        compiler_params=_cparams(("arbitrary", "arbitrary")),
        name="modulation",
    )(cvec, w_mod, b_mod.reshape(depth, 1, n))


def _inproj_kernel(x_ref, mod_ref, g_ref, w_ref, cw_ref, cb_ref,
                   z_ref, xs_ref, bc_ref, q_ref, k_ref, v_ref, go_ref, sm_ref, *, rowlen):
    tm = x_ref.shape[0]
    split = ROW_SPLIT if (tm // ROW_SPLIT) % rowlen == 0 else 1
    sub = tm // split
    m = mod_ref[...]
    geff = g_ref[...] * (1.0 + m[1:2])
    cw = cw_ref[...]

    def rows_gen(rows):
        x = x_ref[rows, :]
        inv = lax.rsqrt(jnp.mean(x * x, axis=-1, keepdims=True) + NORM_EPS)
        h = ((x * inv) * geff + m[0:1]).astype(bf16)
        yield

        def proj(lo, hi):
            return _dot(h, w_ref[:, lo:hi])

        z_ref[rows, :] = proj(0, 512).astype(bf16)
        yield

        xbc = proj(512, 1280)
        yield
        pos = lax.broadcasted_iota(i32, xbc.shape, 0) & (rowlen - 1)
        acc = xbc * cw[2:3]
        for d in (-2, -1, 1, 2):
            shifted = pltpu.roll(xbc, (-d) % sub, 0)
            valid = (pos >= -d) if d < 0 else (pos <= rowlen - 1 - d)
            acc = acc + jnp.where(valid, shifted, 0.0) * cw[2 + d:3 + d]
        yield
        act = _silu(acc + cb_ref[...])
        xs_ref[rows, :] = act[:, :SSD_WIDTH].astype(bf16)
        bc_ref[rows, :] = act[:, SSD_WIDTH:].astype(bf16)
        yield

        qk = proj(1280, 1792)
        q_ref[rows, :] = qk[:, :GLA_QK].astype(bf16)
        k_ref[rows, :] = qk[:, GLA_QK:].astype(bf16)
        yield
        v_ref[rows, :] = proj(1792, 2304).astype(bf16)
        yield
        go_ref[rows, :] = proj(2304, 2816).astype(bf16)
        yield
        sm_ref[rows, :] = proj(2816, 2944)

    _run_interleaved(rows_gen(pl.ds(r * sub, sub)) for r in range(split))


def _inproj(x2d, mod_l, norm_g, w_packed, conv_w8, conv_b, *, tm, rowlen, tiles_per_row, fixed_row):
    t = x2d.shape[0]
    nt = t // tm
    if fixed_row is None:
        mod_map = lambda i: (i // tiles_per_row, 0, 0)
    else:
        mod_map = lambda i: (fixed_row, 0, 0)
    tok = lambda w: pl.BlockSpec((tm, w), lambda i: (i, 0))
    const = lambda a: pl.BlockSpec(a.shape, lambda i: (0,) * a.ndim)
    widths = (512, 512, 256, 256, 256, 512, 512)
    return pl.pallas_call(
        functools.partial(_inproj_kernel, rowlen=rowlen),
        out_shape=tuple(jax.ShapeDtypeStruct((t, w), bf16) for w in widths)
        + (jax.ShapeDtypeStruct((t, SMALL_COLS), f32),),
        grid=(nt,),
        in_specs=[tok(D_MODEL), pl.BlockSpec((None, 6, D_MODEL), mod_map), const(norm_g),
                  const(w_packed), const(conv_w8), const(conv_b)],
        out_specs=tuple(tok(w) for w in widths) + (tok(SMALL_COLS),),
        compiler_params=_cparams(("arbitrary",)),
        name="inproj",
    )(x2d, mod_l, norm_g, w_packed, conv_w8, conv_b)


def _dot_split(a, b, parts, *, split_lhs):
    rest = a if split_lhs else b
    acc = None
    for _ in range(parts):
        piece = rest.astype(bf16)
        rest = rest - piece.astype(f32)
        term = _dot(piece, b) if split_lhs else _dot(a, piece)
        acc = term if acc is None else acc + term
    return acc


def _tri_masks(q):
    r = lax.broadcasted_iota(i32, (q, q), 0)
    c = lax.broadcasted_iota(i32, (q, q), 1)
    return r >= c, r <= c


def _ssd_chunk(load, store, carry, par, e, bm, rev):
    xb, bcv, sm = load()
    dtb, a_lane, dtbt, a_sub = par
    q = xb.shape[0]
    dt_col = jax.nn.softplus(sm + dtb)
    a_col = dt_col * a_lane
    dt_row = jax.nn.softplus(sm.T[0:SSD_HEADS, :] + dtbt)
    a_row = dt_row * a_sub
    yield
    lower, upper = _tri_masks(q)
    lo_b = jnp.where(lower, 1.0, 0.0).astype(bf16)
    up_b = jnp.where(upper, 1.0, 0.0).astype(bf16)
    if not rev:
        cs_col = _dot_split(lo_b, a_col, SSD_SPLIT, split_lhs=False)
        cs_row = _dot_split(a_row, up_b, SSD_SPLIT, split_lhs=True)
        mask = lower
        a_tot = cs_col[q - 1:q, :]
    else:
        cs_col = _dot_split(up_b, a_col, SSD_SPLIT, split_lhs=False)
        cs_row = _dot_split(a_row, lo_b, SSD_SPLIT, split_lhs=True)
        mask = upper
        a_tot = cs_col[0:1, :]
    row_term = cs_row - jnp.log(dt_row)
    yield

    w_exp = _dot((jnp.exp(a_tot - cs_col) * dt_col).astype(bf16), e)
    ecs_exp = _dot(jnp.exp(cs_col).astype(bf16), e)
    dec_exp = _dot_split(jnp.broadcast_to(jnp.exp(a_tot), (8, LANES)), e, SSD_SPLIT, split_lhs=True)[0:1]
    yield

    xw = (xb.astype(f32) * w_exp).astype(bf16)
    b_all = bcv[:, 0:LANES]
    c_all = bcv[:, LANES:2 * LANES]
    b_t = b_all.astype(f32).T.astype(bf16)
    yield
    s_upd = _dot(b_t, xw)
    yield

    lane = lax.broadcasted_iota(i32, (q, LANES), 1)
    zero_b = jnp.zeros((q, LANES), bf16)
    ys = []
    for g in range(SSD_GROUPS):
        in_g = (lane >= SSD_STATE * g) & (lane < SSD_STATE * (g + 1))
        cb = _dot(jnp.where(in_g, c_all, zero_b), b_t)
        yield
        for pp in range(2):
            h0 = 4 * g + 2 * pp
            ms = []
            for h in (h0, h0 + 1):
                seg = cs_col[:, h:h + 1] - row_term[h:h + 1, :]
                ms.append((cb * jnp.exp(jnp.where(mask, seg, -1e30))).astype(bf16))
            xp = xb[:, h0 * SSD_HEAD_DIM:h0 * SSD_HEAD_DIM + LANES]
            rhs = jnp.concatenate([jnp.where(lane < SSD_HEAD_DIM, xp, zero_b),
                                   jnp.where(lane >= SSD_HEAD_DIM, xp, zero_b)], axis=0)
            yield
            ys.append(_dot(jnp.concatenate(ms, axis=1), rhs))
            yield
    y_intra = jnp.concatenate(ys, axis=1)
    yield

    s_old = carry['s']
    carry['s'] = (s_old * dec_exp + s_upd) * bm
    store((y_intra + _dot(c_all, s_old.astype(bf16)) * ecs_exp).astype(bf16))


def _run_interleaved(gens):
    gens = list(gens)
    done = object()
    for w in range(0, len(gens), INTERLEAVE_WAVE):
        live = gens[w:w + INTERLEAVE_WAVE]
        while live:
            live = [g for g in live if next(g, done) is not done]


def _chunk_rows(cps):
    return [(pl.ds(ci * CHUNK, CHUNK), pl.ds((cps - 1 - ci) * CHUNK, CHUNK)) for ci in range(cps)]


def _ssd_kernel(xf_ref, bcf_ref, smf_ref, xr_ref, bcr_ref, smr_ref, h0f_ref, h0r_ref,
                dtbf_ref, af_ref, dtbtf_ref, atf_ref, dtbr_ref, ar_ref, dtbtr_ref, atr_ref, e_ref, bm_ref,
                yf_ref, yr_ref, hff_ref, hfr_ref, st_f, st_r, *, cps, nblk):
    j = pl.program_id(1)

    @pl.when(j == 0)
    def _():
        st_f[...] = h0f_ref[...]
        st_r[...] = h0r_ref[...]

    e = e_ref[...]
    bm = bm_ref[...]
    par_f = (dtbf_ref[...], af_ref[...], dtbtf_ref[...], atf_ref[...])
    par_r = (dtbr_ref[...], ar_ref[...], dtbtr_ref[...], atr_ref[...])
    carry_f = {'s': st_f[...]}
    carry_r = {'s': st_r[...]}

    def chunk(refs, out_ref, rows, carry, par, rev):
        def store(y):
            out_ref[rows, :] = y
        return _ssd_chunk(lambda: tuple(r[rows, :] for r in refs), store, carry, par, e, bm, rev)

    gens = []
    for rows_f, rows_r in _chunk_rows(cps):
        gens.append(chunk((xf_ref, bcf_ref, smf_ref), yf_ref, rows_f, carry_f, par_f, False))
        gens.append(chunk((xr_ref, bcr_ref, smr_ref), yr_ref, rows_r, carry_r, par_r, True))
    _run_interleaved(gens)
    st_f[...] = carry_f['s']
    st_r[...] = carry_r['s']

    @pl.when(j == nblk - 1)
    def _():
        hff_ref[...] = carry_f['s']
        hfr_ref[...] = carry_r['s']


def _scan_specs(t, nb):
    per_row = t // nb
    cps = min(CHUNKS_PER_STEP, per_row // CHUNK)
    nblk = per_row // (cps * CHUNK)
    fmap = lambda b, j: (b * nblk + j, 0)
    rmap = lambda b, j: (b * nblk + nblk - 1 - j, 0)
    return cps, nblk, fmap, rmap


def _ssd_scan(xs, bc, sm, h0_f, h0_r, par_f, par_r, e_mat, bmask, *, nb):
    t = xs.shape[0]
    cps, nblk, fmap, rmap = _scan_specs(t, nb)
    blk = cps * CHUNK
    const = lambda a: pl.BlockSpec(a.shape, lambda b, j: (0,) * a.ndim)
    st = pl.BlockSpec((None, LANES, SSD_WIDTH), lambda b, j: (b, 0, 0))
    widths = (SSD_WIDTH, 2 * LANES, SMALL_COLS)
    tok_in = [pl.BlockSpec((blk, w), m) for m in (fmap, rmap) for w in widths]
    consts = (*par_f, *par_r, e_mat, bmask)
    return pl.pallas_call(
        functools.partial(_ssd_kernel, cps=cps, nblk=nblk),
        out_shape=(jax.ShapeDtypeStruct((t, SSD_WIDTH), bf16), jax.ShapeDtypeStruct((t, SSD_WIDTH), bf16),
                   jax.ShapeDtypeStruct((nb, LANES, SSD_WIDTH), f32),
                   jax.ShapeDtypeStruct((nb, LANES, SSD_WIDTH), f32)),
        grid=(nb, nblk),
        in_specs=tok_in + [st, st] + [const(a) for a in consts],
        out_specs=(pl.BlockSpec((blk, SSD_WIDTH), fmap), pl.BlockSpec((blk, SSD_WIDTH), rmap), st, st),
        scratch_shapes=[pltpu.VMEM((LANES, SSD_WIDTH), f32), pltpu.VMEM((LANES, SSD_WIDTH), f32)],
        compiler_params=_cparams(("arbitrary", "arbitrary")),
        name="ssd",
    )(xs, bc, sm, xs, bc, sm, h0_f, h0_r, *consts)


def _gla_chunk(load, store, carry, par, bm, rev):
    qb, kb, vb, sm = load()
    gup, gbias = par
    q = qb.shape[0]
    gp = _dot(sm.astype(bf16), gup) + gbias
    yield
    g = jax.nn.log_sigmoid(gp) * (1.0 / GLA_GATE_NORM)
    yield
    lower, upper = _tri_masks(q)
    if not rev:
        b = _dot_split(jnp.where(lower, 1.0, 0.0).astype(bf16), g, GLA_SPLIT, split_lhs=False)
        mask, mid = lower, q // 2 - 1
        b_tot = b[q - 1:q, :]
    else:
        b = _dot_split(jnp.where(upper, 1.0, 0.0).astype(bf16), g, GLA_SPLIT, split_lhs=False)
        mask, mid = upper, q // 2
        b_tot = b[0:1, :]
    b_mid = b[mid:mid + 1, :]
    yield

    qf = qb.astype(f32) * (GLA_KEY_DIM ** -0.5)
    kf = kb.astype(f32)
    qd = (qf * jnp.exp(b - b_mid)).astype(bf16)
    yield
    ki = (kf * jnp.exp(b_mid - b)).astype(bf16)
    yield
    q_st = (qf * jnp.exp(b)).astype(bf16)
    yield
    k_end = (kf * jnp.exp(b_tot - b)).astype(bf16)
    dec = jnp.exp(b_tot)
    yield
    v_t = vb.astype(f32).T.astype(bf16)
    yield
    s_upd = _dot(v_t, k_end)
    yield

    lane = lax.broadcasted_iota(i32, (q, LANES), 1)
    zero_b = jnp.zeros((q, LANES), bf16)
    outs = []
    for h in range(GLA_HEADS):
        p, hh = divmod(h, 2)
        qp = qd[:, p * LANES:(p + 1) * LANES]
        kp = ki[:, p * LANES:(p + 1) * LANES]
        in_h = (lane >= GLA_KEY_DIM * hh) & (lane < GLA_KEY_DIM * (hh + 1))
        s = lax.dot_general(jnp.where(in_h, qp, zero_b), kp, _NT, preferred_element_type=f32)
        yield
        attn = jnp.where(mask, s, 0.0).astype(bf16)
        outs.append(_dot(attn, vb[:, h * GLA_VAL_DIM:(h + 1) * GLA_VAL_DIM]))
        yield
    o_intra = jnp.concatenate(outs, axis=1)
    yield

    s_old = carry['s']
    carry['s'] = (s_old * dec + s_upd) * bm
    o_inter = lax.dot_general(q_st, s_old.astype(bf16), _NT, preferred_element_type=f32)
    store((o_intra + o_inter).astype(bf16))


def _gla_kernel(qf_ref, kf_ref, vf_ref, smf_ref, qr_ref, kr_ref, vr_ref, smr_ref, s0f_ref, s0r_ref,
                gupf_ref, gbf_ref, gupr_ref, gbr_ref, bm_ref,
                of_ref, or_ref, sff_ref, sfr_ref, st_f, st_r, *, cps, nblk):
    j = pl.program_id(1)

    @pl.when(j == 0)
    def _():
        st_f[...] = s0f_ref[...]
        st_r[...] = s0r_ref[...]

    bm = bm_ref[...]
    par_f = (gupf_ref[...], gbf_ref[...])
    par_r = (gupr_ref[...], gbr_ref[...])
    carry_f = {'s': st_f[...]}
    carry_r = {'s': st_r[...]}

    def chunk(refs, out_ref, rows, carry, par, rev):
        def store(o):
            out_ref[rows, :] = o
        return _gla_chunk(lambda: tuple(r[rows, :] for r in refs), store, carry, par, bm, rev)

    gens = []
    for rows_f, rows_r in _chunk_rows(cps):
        gens.append(chunk((qf_ref, kf_ref, vf_ref, smf_ref), of_ref, rows_f, carry_f, par_f, False))
        gens.append(chunk((qr_ref, kr_ref, vr_ref, smr_ref), or_ref, rows_r, carry_r, par_r, True))
    _run_interleaved(gens)
    st_f[...] = carry_f['s']
    st_r[...] = carry_r['s']

    @pl.when(j == nblk - 1)
    def _():
        sff_ref[...] = carry_f['s']
        sfr_ref[...] = carry_r['s']


def _gla_scan(qa, ka, va, sm, s0_f, s0_r, par_f, par_r, bmask, *, nb):
    t = qa.shape[0]
    cps, nblk, fmap, rmap = _scan_specs(t, nb)
    blk = cps * CHUNK
    const = lambda a: pl.BlockSpec(a.shape, lambda b, j: (0,) * a.ndim)
    st = pl.BlockSpec((None, GLA_WIDTH, GLA_QK), lambda b, j: (b, 0, 0))
    widths = (GLA_QK, GLA_QK, GLA_WIDTH, SMALL_COLS)
    tok_in = [pl.BlockSpec((blk, w), m) for m in (fmap, rmap) for w in widths]
    consts = (*par_f, *par_r, bmask)
    return pl.pallas_call(
        functools.partial(_gla_kernel, cps=cps, nblk=nblk),
        out_shape=(jax.ShapeDtypeStruct((t, GLA_WIDTH), bf16), jax.ShapeDtypeStruct((t, GLA_WIDTH), bf16),
                   jax.ShapeDtypeStruct((nb, GLA_WIDTH, GLA_QK), f32),
                   jax.ShapeDtypeStruct((nb, GLA_WIDTH, GLA_QK), f32)),
        grid=(nb, nblk),
        in_specs=tok_in + [st, st] + [const(a) for a in consts],
        out_specs=(pl.BlockSpec((blk, GLA_WIDTH), fmap), pl.BlockSpec((blk, GLA_WIDTH), rmap), st, st),
        scratch_shapes=[pltpu.VMEM((GLA_WIDTH, GLA_QK), f32), pltpu.VMEM((GLA_WIDTH, GLA_QK), f32)],
        compiler_params=_cparams(("arbitrary", "arbitrary")),
        name="gla",
    )(qa, ka, va, sm, qa, ka, va, sm, s0_f, s0_r, *consts)


def _outproj_kernel(yf_ref, yb_ref, xs_ref, z_ref, of_ref, ob_ref, go_ref, xres_ref, mod_ref,
                    dsk_ref, sg_ref, gg_ref, wout_ref, n2g_ref, wr_ref,
                    xnew_ref, h2_ref, lg_ref):
    tm = xres_ref.shape[0]
    sub = tm // ROW_SPLIT
    m = mod_ref[...]
    h2_gain = n2g_ref[...] * (1.0 + m[4:5])

    def rows_gen(rows):
        up = lambda r: r[rows, :].astype(f32)
        y = up(yf_ref) + up(yb_ref) + dsk_ref[...] * up(xs_ref)
        yield
        y = y * _silu(up(z_ref))
        yield
        y = y * lax.rsqrt(jnp.mean(y * y, axis=-1, keepdims=True) + NORM_EPS) * sg_ref[...]
        yield
        o = up(of_ref) + up(ob_ref)
        parts = []
        for h in range(GLA_HEADS):
            oh = o[:, h * GLA_VAL_DIM:(h + 1) * GLA_VAL_DIM]
            parts.append(oh * lax.rsqrt(jnp.mean(oh * oh, axis=-1, keepdims=True) + NORM_EPS))
        yield
        o = jnp.concatenate(parts, axis=1) * gg_ref[...] * _silu(up(go_ref))
        yield
        cat = jnp.concatenate([y, o], axis=1).astype(bf16)
        mix = _dot(cat, wout_ref[...])
        yield
        xn = xres_ref[rows, :] + m[2:3] * mix
        xnew_ref[rows, :] = xn
        yield
        inv = lax.rsqrt(jnp.mean(xn * xn, axis=-1, keepdims=True) + NORM_EPS)
        h2 = (xn * inv) * h2_gain + m[3:4]
        _store_planes(h2_ref, rows, _pack_rows(h2))
        yield
        h2_hi = h2.astype(bf16)
        h2_lo = (h2 - h2_hi.astype(f32)).astype(bf16)
        wr_hi = wr_ref[0]
        lg_ref[rows, :] = _dot(h2_hi, wr_hi) + _dot(h2_hi, wr_ref[1]) + _dot(h2_lo, wr_hi)

    _run_interleaved(rows_gen(pl.ds(r * sub, sub)) for r in range(ROW_SPLIT))


def _outproj_kernel_into(*refs):
    _outproj_kernel(*refs[:15], *refs[17:])


def _outproj(yf, yb, xs, z, of, ob, go, xres, mod_l, dsk, sg, gg, wout, n2g, wr,
             *, tm, tiles_per_row, fixed_row, routed_tokens=None, token_off=0, into=None):
    t = xres.shape[0]
    total = t if routed_tokens is None else routed_tokens
    off = token_off // tm
    if fixed_row is None:
        mod_map = lambda i: (i // tiles_per_row, 0, 0)
    else:
        mod_map = lambda i: (fixed_row, 0, 0)
    tok = lambda w: pl.BlockSpec((tm, w), lambda i: (i, 0))
    const = lambda a: pl.BlockSpec(a.shape, lambda i: (0,) * a.ndim)
    in_specs = [tok(512)] * 7 + [tok(D_MODEL), pl.BlockSpec((None, 6, D_MODEL), mod_map),
                                 const(dsk), const(sg), const(gg), const(wout), const(n2g), const(wr)]
    args = (yf, yb, xs, z, of, ob, go, xres, mod_l, dsk, sg, gg, wout, n2g, wr)
    body, aliases = _outproj_kernel, {}
    if into is not None:
        body, aliases = _outproj_kernel_into, {15: 1, 16: 2}
        in_specs = in_specs + [pl.BlockSpec(memory_space=pl.ANY)] * 2
        args = args + tuple(into)
    return pl.pallas_call(
        body,
        out_shape=(jax.ShapeDtypeStruct((t, D_MODEL), f32),
                   jax.ShapeDtypeStruct((SC_ROW_PARTS, total, SC_PART_COLS), u32),
                   jax.ShapeDtypeStruct((total, LANES), f32)),
        grid=(t // tm,),
        in_specs=in_specs,
        out_specs=(tok(D_MODEL), _plane_spec(tm, lambda i: i + off),
                   pl.BlockSpec((tm, LANES), lambda i: (i + off, 0))),
        input_output_aliases=aliases,
        compiler_params=_cparams(("arbitrary",)),
        name="outproj",
    )(*args)


def _route_kernel(lg_ref, bias_ref, su_ref, ids_ref, rank_ref, gcol_ref, cnt_ref, carry):
    i = pl.program_id(0)
    tm = lg_ref.shape[0]

    @pl.when(i == 0)
    def _():
        carry[...] = jnp.zeros_like(carry)

    s = jax.nn.sigmoid(lg_ref[...].T[0:N_EXPERTS, :])
    sel = s + bias_ref[...]
    a = [sel[8 * m:8 * (m + 1)] for m in range(EXPERTS_PER_GROUP)]
    sv = [s[8 * m:8 * (m + 1)] for m in range(EXPERTS_PER_GROUP)]
    hi01, lo01 = jnp.maximum(a[0], a[1]), jnp.minimum(a[0], a[1])
    hi23, lo23 = jnp.maximum(a[2], a[3]), jnp.minimum(a[2], a[3])
    gscore = jnp.maximum(hi01, hi23) + jnp.maximum(jnp.minimum(hi01, hi23), jnp.maximum(lo01, lo23))
    giota = lax.broadcasted_iota(i32, gscore.shape, 0)
    gmax = jnp.max(gscore, axis=0, keepdims=True)
    gidx = jnp.min(jnp.where(gscore == gmax, giota, N_EXPERT_GROUPS), axis=0, keepdims=True)
    pick = giota == gidx
    v = [jnp.sum(jnp.where(pick, a[m], 0.0), axis=0, keepdims=True) for m in range(4)]
    w = [jnp.sum(jnp.where(pick, sv[m], 0.0), axis=0, keepdims=True) for m in range(4)]

    def first_max(vals, excluded):
        best = vals[0]
        for m in range(1, 4):
            best = jnp.maximum(best, vals[m])
        idx = jnp.full(best.shape, 3, i32)
        for m in (2, 1, 0):
            hit = vals[m] == best
            if excluded is not None:
                hit = hit & (excluded != m)
            idx = jnp.where(hit, m, idx)
        return idx

    i1 = first_max(v, None)
    v_rest = [jnp.where(i1 == m, -jnp.inf, v[m]) for m in range(4)]
    i2 = first_max(v_rest, i1)

    def take(vals, idx):
        out = vals[3]
        for m in (2, 1, 0):
            out = jnp.where(idx == m, vals[m], out)
        return out

    w1, w2 = take(w, i1), take(w, i2)
    denom = w1 + w2
    id1 = gidx * EXPERTS_PER_GROUP + i1
    id2 = gidx * EXPERTS_PER_GROUP + i2
    row2 = lax.broadcasted_iota(i32, (TOP_K, tm), 0)
    ids_ref[...] = jnp.where(row2 == 0, id1, id2)

    eiota = lax.broadcasted_iota(i32, (N_EXPERTS, tm), 0)
    hit1, hit2 = eiota == id1, eiota == id2
    onehot = jnp.where(hit1, 1.0, 0.0) + jnp.where(hit2, 1.0, 0.0)
    before = _dot(onehot.astype(bf16), su_ref[...]) + carry[...]
    r1 = jnp.sum(jnp.where(hit1, before, 0.0), axis=0, keepdims=True)
    r2 = jnp.sum(jnp.where(hit2, before, 0.0), axis=0, keepdims=True)
    rank_ref[...] = jnp.where(row2 == 0, r1, r2).astype(i32)
    new_carry = carry[...] + jnp.sum(onehot, axis=1, keepdims=True)
    carry[...] = new_carry
    cnt_ref[...] = jnp.broadcast_to(new_carry, cnt_ref.shape)

    rows = lax.broadcasted_iota(i32, (LANES, tm), 0)
    gates = jnp.where(rows == 0, w1 / denom, jnp.where(rows == 1, w2 / denom, 0.0))
    gcol_ref[...] = gates.T


def _route(logits, bias_col, su):
    t = logits.shape[0]
    tm = ROUTE_TILE
    return pl.pallas_call(
        _route_kernel,
        out_shape=(jax.ShapeDtypeStruct((2, t), i32), jax.ShapeDtypeStruct((2, t), i32),
                   jax.ShapeDtypeStruct((t, LANES), f32), jax.ShapeDtypeStruct((N_EXPERTS, LANES), f32)),
        grid=(t // tm,),
        in_specs=[pl.BlockSpec((tm, LANES), lambda i: (i, 0)),
                  pl.BlockSpec((N_EXPERTS, 1), lambda i: (0, 0)),
                  pl.BlockSpec((tm, tm), lambda i: (0, 0))],
        out_specs=(pl.BlockSpec((2, tm), lambda i: (0, i)), pl.BlockSpec((2, tm), lambda i: (0, i)),
                   pl.BlockSpec((tm, LANES), lambda i: (i, 0)),
                   pl.BlockSpec((N_EXPERTS, LANES), lambda i: (0, 0))),
        scratch_shapes=[pltpu.VMEM((N_EXPERTS, 1), f32)],
        compiler_params=_cparams(("arbitrary",)),
        name="route",
    )(logits, bias_col, su)


def _dest_kernel(ids_ref, rank_ref, pstart_ref, idx_ref, *, p):
    ids = ids_ref[...]
    tm = ids.shape[1]
    eiota = lax.broadcasted_iota(i32, (N_EXPERTS, tm), 0)
    ps = pstart_ref[...]
    rank = rank_ref[...]
    row = lax.broadcasted_iota(i32, (TOP_K * SC_ROW_PARTS, tm), 0)
    out = jnp.zeros((TOP_K * SC_ROW_PARTS, tm), i32)
    for k in range(TOP_K):
        start = jnp.sum(jnp.where(eiota == ids[k:k + 1, :], ps, 0.0), axis=0, keepdims=True).astype(i32)
        dest = start + rank[k:k + 1, :]
        for part in range(SC_ROW_PARTS):
            out = jnp.where(row == k * SC_ROW_PARTS + part, dest + part * p, out)
    idx_ref[...] = out


def _dest(ids, rank, pstart_col, p):
    t = ids.shape[1]
    tm = MOE_TILE
    return pl.pallas_call(
        functools.partial(_dest_kernel, p=p),
        out_shape=jax.ShapeDtypeStruct((TOP_K * SC_ROW_PARTS, t), i32),
        grid=(t // tm,),
        in_specs=[pl.BlockSpec((TOP_K, tm), lambda i: (0, i)), pl.BlockSpec((TOP_K, tm), lambda i: (0, i)),
                  pl.BlockSpec((N_EXPERTS, 1), lambda i: (0, 0))],
        out_specs=pl.BlockSpec((TOP_K * SC_ROW_PARTS, tm), lambda i: (0, i)),
        compiler_params=_cparams(("arbitrary",)),
        name="dest",
    )(ids, rank, pstart_col).reshape(1, -1)


def _expert_kernel(be_ref, nu_ref, nv_ref, x_ref, wg_ref, wu_ref, wd_ref, y_ref, wg_b, wu_b, wd_b):
    i = pl.program_id(0)
    fresh = jnp.logical_or(i == 0, be_ref[i] != be_ref[jnp.maximum(i - 1, 0)])

    @pl.when(jnp.logical_and(fresh, i < nu_ref[0]))
    def _():
        wg_b[...] = wg_ref[...].astype(bf16)
        wu_b[...] = wu_ref[...].astype(bf16)
        wd_b[...] = wd_ref[...].astype(bf16)

    @pl.when(i < nu_ref[0])
    def _():
        sub = x_ref.shape[1] // ROW_SPLIT
        n_valid = nv_ref[i]

        def rows_gen(r):
            rows = pl.ds(r * sub, sub)
            packed_x = _load_planes(x_ref, rows)
            row = lax.broadcasted_iota(i32, packed_x.shape, 0) + r * sub
            xb = _unpack_rows(jnp.where(row < n_valid, packed_x, jnp.uint32(0))).astype(bf16)
            yield
            g = _dot(xb, wg_b[...])
            yield
            u = _dot(xb, wu_b[...])
            yield
            hmid = (_silu(g) * u).astype(bf16)
            yield
            _store_planes(y_ref, rows, _pack_rows(_dot(hmid, wd_b[...])))

        _run_interleaved(rows_gen(r) for r in range(ROW_SPLIT))

    @pl.when(i >= nu_ref[0])
    def _():
        y_ref[...] = jnp.zeros_like(y_ref)


def _experts(block_e, n_used, block_valid, buf, w_gate, w_up, w_down, layer):
    p = buf.shape[1]
    nb = p // EXPERT_BLOCK
    d, ff = w_gate.shape[-2:]
    row_blocks = _plane_spec(EXPERT_BLOCK, lambda i, be, nu, nv: i)
    weights = lambda a, b: pl.BlockSpec((None, None, a, b), lambda i, be, nu, nv: (layer, be[i], 0, 0))
    return pl.pallas_call(
        _expert_kernel,
        out_shape=jax.ShapeDtypeStruct(buf.shape, u32),
        grid_spec=pltpu.PrefetchScalarGridSpec(
            num_scalar_prefetch=3,
            grid=(nb,),
            in_specs=[row_blocks, weights(d, ff), weights(d, ff), weights(ff, d)],
            out_specs=row_blocks,
            scratch_shapes=[pltpu.VMEM((d, ff), bf16), pltpu.VMEM((d, ff), bf16), pltpu.VMEM((ff, d), bf16)],
        ),
        compiler_params=_cparams(("arbitrary",)),
        name="experts",
    )(block_e, n_used, block_valid, buf, w_gate, w_up, w_down)


def _sc_gather_rows(planes, idx):
    parts, p, cols = planes.shape
    n = idx.shape[1]
    t = n // (TOP_K * parts)
    halves = planes.reshape(parts * p, cols)
    mesh = plsc.VectorSubcoreMesh(core_axis_name="core", subcore_axis_name="subcore")

    @pl.kernel(out_type=jax.ShapeDtypeStruct((n, cols), planes.dtype), mesh=mesh, scratch_types=[])
    def gather(x_hbm, i_hbm, o_hbm):
        def body(i_vmem, o_vmem):
            pltpu.sync_copy(x_hbm.at[i_vmem.at[0]], o_vmem)

        pltpu.emit_pipeline(
            body,
            grid=(n // SC_WINDOW,),
            in_specs=[pl.BlockSpec((1, SC_WINDOW), lambda i: (0, i))],
            out_specs=[pl.BlockSpec((SC_WINDOW, cols), lambda i: (i, 0))],
            core_axis_name=("core", "subcore"),
            dimension_semantics=(pltpu.PARALLEL,),
        )(i_hbm, o_hbm)

    return gather(halves, idx).reshape(TOP_K, parts, t, cols)


def _combine_rows_kernel(*refs, final):
    part_refs = refs[:TOP_K * SC_ROW_PARTS]
    gcol_ref, x_ref, mod_ref, fg_ref, o_ref = refs[TOP_K * SC_ROW_PARTS:]
    gc = gcol_ref[...]
    rows = [_unpack_rows(jnp.concatenate([r[...] for r in part_refs[k * SC_ROW_PARTS:(k + 1) * SC_ROW_PARTS]], axis=1))
            for k in range(TOP_K)]
    ffn = gc[:, 0:1] * rows[0] + gc[:, 1:2] * rows[1]
    x = x_ref[...] + mod_ref[...][5:6] * ffn
    if final:
        x = x * lax.rsqrt(jnp.mean(x * x, axis=-1, keepdims=True) + NORM_EPS) * fg_ref[...]
    o_ref[...] = x


def _combine_rows(yg, tile_off, gcol, xnew, mod_l, final_g, *, tiles_per_row, fixed_row, final):
    tm = MOE_TILE
    t = xnew.shape[0]
    if fixed_row is None:
        mod_map = lambda i: (i // tiles_per_row, 0, 0)
    else:
        mod_map = lambda i: (fixed_row, 0, 0)
    yg_off = tile_off if yg.shape[2] != t else 0
    piece = lambda k, part: pl.BlockSpec((None, None, tm, SC_PART_COLS), lambda i: (k, part, i + yg_off, 0))
    pieces = [piece(k, part) for k in range(TOP_K) for part in range(SC_ROW_PARTS)]
    return pl.pallas_call(
        functools.partial(_combine_rows_kernel, final=final),
        out_shape=jax.ShapeDtypeStruct((t, D_MODEL), f32),
        grid=(t // tm,),
        in_specs=pieces + [
                  pl.BlockSpec((tm, LANES), lambda i: (i + tile_off, 0)),
                  pl.BlockSpec((tm, D_MODEL), lambda i: (i, 0)),
                  pl.BlockSpec((None, 6, D_MODEL), mod_map),
                  pl.BlockSpec((1, D_MODEL), lambda i: (0, 0))],
        out_specs=pl.BlockSpec((tm, D_MODEL), lambda i: (i, 0)),
        compiler_params=_cparams(("arbitrary",)),
        name="combine_rows",
    )(*([yg] * len(pieces)), gcol, xnew, mod_l, final_g)


def _sc_scatter_rows(planes, idx, p):
    parts, t, cols = planes.shape
    n = idx.shape[1]
    src = planes.reshape(parts * t, cols)
    src_windows = src.shape[0] // SC_WINDOW
    mesh = plsc.VectorSubcoreMesh(core_axis_name="core", subcore_axis_name="subcore")

    @pl.kernel(out_type=jax.ShapeDtypeStruct((parts * p, cols), planes.dtype), mesh=mesh, scratch_types=[])
    def scatter(x_hbm, i_hbm, o_hbm):
        def body(x_vmem, i_vmem):
            pltpu.sync_copy(x_vmem, o_hbm.at[i_vmem.at[0]])

        pltpu.emit_pipeline(
            body,
            grid=(n // SC_WINDOW,),
            in_specs=[pl.BlockSpec((SC_WINDOW, cols), lambda i: (i % src_windows, 0)),
                      pl.BlockSpec((1, SC_WINDOW), lambda i: (0, i))],
            out_specs=[],
            core_axis_name=("core", "subcore"),
            dimension_semantics=(pltpu.PARALLEL,),
        )(x_hbm, i_hbm)

    return scatter(src, idx).reshape(parts, p, cols)


def _moe(h2_planes, logits, router_bias_col, su, w_gate, w_up, w_down, layer):
    t = logits.shape[0]
    ids, rank, gcol, cnt = _route(logits, router_bias_col, su)
    counts = cnt[:, 0].astype(i32)
    padded = (counts + EXPERT_BLOCK - 1) // EXPERT_BLOCK * EXPERT_BLOCK
    pend = jnp.cumsum(padded)
    pstart = pend - padded
    nb = (t * TOP_K) // EXPERT_BLOCK + N_EXPERTS
    block_pos = jnp.arange(nb, dtype=i32) * EXPERT_BLOCK
    block_e = jnp.minimum(jnp.sum((pend[None, :] <= block_pos[:, None]).astype(i32), axis=1), N_EXPERTS - 1)
    block_valid = jnp.clip(pstart[block_e] + counts[block_e] - block_pos, 0, EXPERT_BLOCK).astype(i32)
    n_used = (pend[-1:] // EXPERT_BLOCK).astype(i32)
    p = nb * EXPERT_BLOCK
    idx = _dest(ids, rank, pstart.astype(f32)[:, None], p)
    buf = _sc_scatter_rows(h2_planes, idx, p)
    y = _experts(block_e, n_used, block_valid, buf, w_gate, w_up, w_down, layer)
    return idx, gcol, y


def _pack_layer(l, w_in, conv_w, conv_b, dt_bias_f, dt_bias_b, a_log_f, a_log_b, d_skip, ssd_norm_g,
                gk_up_f, gk_bias_f, gk_up_b, gk_bias_b, gla_norm_g, w_out, norm1_g, norm2_g):
    w = w_in[l]
    z, xbc, dt, q, k, v, go, gk = jnp.split(w, [512, 1280, 1288, 1544, 1800, 2312, 2824], axis=1)
    small = jnp.concatenate([dt, gk, jnp.zeros((D_MODEL, SMALL_COLS - 24), f32)], axis=1)
    w_packed = jnp.concatenate([z, xbc, q, k, v, go, small], axis=1).astype(bf16)

    def lane_row(vec):
        return jnp.zeros((1, LANES), f32).at[0, :SSD_HEADS].set(vec)

    def sub_col(vec):
        return jnp.broadcast_to(vec[:, None], (SSD_HEADS, CHUNK)).astype(f32)

    def gup(m):
        return jnp.zeros((SMALL_COLS, GLA_QK), f32).at[SSD_HEADS:SSD_HEADS + GLA_GATE_RANK].set(m).astype(bf16)

    a_f = -jnp.exp(a_log_f[l])
    a_b = -jnp.exp(a_log_b[l])
    return dict(
        w_packed=w_packed,
        conv_w8=jnp.zeros((8, SSD_XBC), f32).at[:5].set(conv_w[l]),
        conv_b=conv_b[l][None, :],
        ssd_f=(lane_row(dt_bias_f[l]), lane_row(a_f), sub_col(dt_bias_f[l]), sub_col(a_f)),
        ssd_b=(lane_row(dt_bias_b[l]), lane_row(a_b), sub_col(dt_bias_b[l]), sub_col(a_b)),
        gla_f=(gup(gk_up_f[l]), gk_bias_f[l][None, :]),
        gla_b=(gup(gk_up_b[l]), gk_bias_b[l][None, :]),
        dsk=jnp.repeat(d_skip[l], SSD_HEAD_DIM)[None, :],
        sg=ssd_norm_g[l][None, :],
        gg=jnp.tile(gla_norm_g[l], GLA_HEADS)[None, :],
        wout=w_out[l].astype(bf16),
        n1g=norm1_g[l][None, :],
        n2g=norm2_g[l][None, :],
    )


def _constants():
    r = jnp.arange(LANES)[:, None]
    c = jnp.arange(SSD_WIDTH)[None, :]
    e_mat = ((c // SSD_HEAD_DIM) == r).astype(bf16)
    ssd_mask = ((r // SSD_STATE) == (c // (SSD_WIDTH // SSD_GROUPS))).astype(f32)
    rr = jnp.arange(GLA_WIDTH)[:, None]
    cc = jnp.arange(GLA_QK)[None, :]
    gla_mask = ((rr // GLA_VAL_DIM) == (cc // GLA_KEY_DIM)).astype(f32)
    k = jnp.arange(ROUTE_TILE)
    su = (k[:, None] < k[None, :]).astype(bf16)
    return e_mat, ssd_mask, gla_mask, su


def _mixers(streams, pk, consts, nb):
    e_mat, ssd_mask, gla_mask, _ = consts
    out = {}
    h0_f = jnp.zeros((nb, LANES, SSD_WIDTH), f32)
    h0_b = h0_f
    s0_f = jnp.zeros((nb, GLA_WIDTH, GLA_QK), f32)
    s0_b = s0_f
    for name in ('ctx', 'lat'):
        z, xs, bc, q, k, v, go, sm = streams[name]
        yf, yb, h0_f, h0_b = _ssd_scan(xs, bc, sm, h0_f, h0_b, pk['ssd_f'], pk['ssd_b'], e_mat, ssd_mask, nb=nb)
        of, ob, s0_f, s0_b = _gla_scan(q, k, v, sm, s0_f, s0_b, pk['gla_f'], pk['gla_b'], gla_mask, nb=nb)
        out[name] = (yf, yb, of, ob)
    return out


def kernel(x, c, ctx, c_ctx, w_mod, b_mod, norm1_g, norm2_g, w_in, conv_w, conv_b, dt_bias_f, dt_bias_b, a_log_f, a_log_b, d_skip, ssd_norm_g, gk_up_f, gk_bias_f, gk_up_b, gk_bias_b, gla_norm_g, w_out, w_router, router_bias, w_gate, w_up, w_down, final_norm_g):
    nb, seq, d = x.shape
    ctx_len = ctx.shape[1]
    depth = w_mod.shape[0]
    consts = _constants()
    su = consts[3]

    cvec = jnp.zeros((16, d), f32).at[:nb].set(c).at[nb].set(c_ctx)
    mod = _modulation(cvec, w_mod, b_mod).reshape(depth, 16, 6, d)
    ctx_row = nb

    perm = jnp.array([g * EXPERTS_PER_GROUP + m for m in range(EXPERTS_PER_GROUP)
                      for g in range(N_EXPERT_GROUPS)], dtype=i32)
    wr32 = jnp.zeros((d, LANES), f32).at[:, :N_EXPERTS].set(w_router[:, perm])
    wr_hi = wr32.astype(bf16)
    wr = jnp.stack([wr_hi, (wr32 - wr_hi.astype(f32)).astype(bf16)])
    bias_col = router_bias[perm][:, None]

    x2 = x.reshape(nb * seq, d)
    c2 = ctx.reshape(nb * ctx_len, d)
    lat_tiles = seq // TM_LAT

    for l in range(depth):
        last = l == depth - 1
        pk = _pack_layer(l, w_in, conv_w, conv_b, dt_bias_f, dt_bias_b, a_log_f, a_log_b, d_skip, ssd_norm_g,
                         gk_up_f, gk_bias_f, gk_up_b, gk_bias_b, gla_norm_g, w_out, norm1_g, norm2_g)
        mod_l = mod[l]
        streams = {
            'ctx': _inproj(c2, mod_l, pk['n1g'], pk['w_packed'], pk['conv_w8'], pk['conv_b'],
                           tm=ctx_len, rowlen=ctx_len, tiles_per_row=1, fixed_row=ctx_row),
            'lat': _inproj(x2, mod_l, pk['n1g'], pk['w_packed'], pk['conv_w8'], pk['conv_b'],
                           tm=TM_LAT, rowlen=GRID_W, tiles_per_row=lat_tiles, fixed_row=None),
        }
        mix = _mixers(streams, pk, consts, nb)

        def merge(name, xres, tm, tiles_per_row, fixed_row, **where):
            z, xs, bc, q, k, v, go, sm = streams[name]
            yf, yb, of, ob = mix[name]
            return _outproj(yf, yb, xs, z, of, ob, go, xres, mod_l, pk['dsk'], pk['sg'], pk['gg'],
                            pk['wout'], pk['n2g'], wr, tm=tm, tiles_per_row=tiles_per_row, fixed_row=fixed_row,
                            **where)

        if last:
            xn_lat, h2, logits = merge('lat', x2, TM_LAT, lat_tiles, None)
        else:
            n_ctx, n_all = c2.shape[0], c2.shape[0] + x2.shape[0]
            xn_ctx, h2, logits = merge('ctx', c2, ctx_len, 1, ctx_row, routed_tokens=n_all)
            xn_lat, h2, logits = merge('lat', x2, TM_LAT, lat_tiles, None, routed_tokens=n_all, token_off=n_ctx,
                                       into=(h2, logits))
        row_idx, gcol, y = _moe(h2, logits, bias_col, su, w_gate, w_up, w_down, l)
        lat_off = 0
        if not last:
            n_ctx = xn_ctx.shape[0]
            idx_kpt = row_idx.reshape(TOP_K * SC_ROW_PARTS, -1)
            yg_ctx = _sc_gather_rows(y, idx_kpt[:, :n_ctx].reshape(1, -1))
            yg = _sc_gather_rows(y, idx_kpt[:, n_ctx:].reshape(1, -1))
            c2 = _combine_rows(yg_ctx, 0, gcol, xn_ctx, mod_l, final_norm_g[None, :],
                               tiles_per_row=1, fixed_row=ctx_row, final=False)
            lat_off = n_ctx // MOE_TILE
        else:
            yg = _sc_gather_rows(y, row_idx)
        x2 = _combine_rows(yg, lat_off, gcol, xn_lat, mod_l, final_norm_g[None, :],
                           tiles_per_row=seq // MOE_TILE, fixed_row=None, final=last)
    return x2.reshape(nb, seq, d)
```
